```python
import math
import jax, jax.numpy as jnp
from jax import lax
import numpy as np

D_MODEL = 4096
BATCH = 4
SEQ = 2048
DEPTH = 4
DEC_BATCH = 32
DEC_SEQ = 1
PAST_LEN = 8192
PAGE_SIZE = 128

N_A_LAYERS = DEPTH // 2
N_B_LAYERS = DEPTH - N_A_LAYERS
HEAD_DIM = 128
MIX_WIDTH = D_MODEL
N_MEM = 256
N_MEM_HEADS = 4
MEM_HEAD_DIM = D_MODEL // 16
MEM_WIDTH = N_MEM_HEADS * MEM_HEAD_DIM
TOKEN_WIDTH = MIX_WIDTH - MEM_WIDTH
N_DELTA_HEADS = TOKEN_WIDTH // HEAD_DIM
CONV_W = 4
CONV_CH = 3 * TOKEN_WIDTH
DELTA_CHUNK = 64
N_Q_HEADS = TOKEN_WIDTH // HEAD_DIM
N_KV_HEADS = 8
GQA_GROUP = N_Q_HEADS // N_KV_HEADS
WINDOW = 128
D_FF = -(-8 * D_MODEL // (3 * 256)) * 256
EPS = 1e-6
OFF_Z = CONV_CH
OFF_B = OFF_Z + TOKEN_WIDTH
OFF_A = OFF_B + N_DELTA_HEADS
OFF_QM = OFF_A + N_DELTA_HEADS
A_COLS = OFF_QM + MEM_WIDTH

kernel_name = 'gdn_swa_sink_yoco_step'


def _rmsnorm(x, w):
    xf = x.astype(jnp.float32)
    y = xf * lax.rsqrt(jnp.mean(xf * xf, axis=-1, keepdims=True) + EPS)
    return (y * w.astype(jnp.float32)).astype(x.dtype)


def _l2norm(x):
    return x * lax.rsqrt(jnp.sum(x * x, axis=-1, keepdims=True) + EPS)


def _alibi_slopes(n):
    def pow2_slopes(m):
        start = 2.0 ** (-8.0 / m)
        return [start ** (i + 1) for i in range(m)]
    c = 2 ** int(math.floor(math.log2(n)))
    s = pow2_slopes(c)
    if c < n:
        s = s + pow2_slopes(2 * c)[0::2][: n - c]
    return np.asarray(s, np.float32)


def _causal_conv(ext, w):
    L = ext.shape[1] - (CONV_W - 1)
    out = ext[:, 0:L] * w[0]
    for j in range(1, CONV_W):
        out = out + ext[:, j:j + L] * w[j]
    return out


def _sink_softmax(s, sink):
    m = jnp.maximum(jnp.max(s, axis=-1), sink)
    p = jnp.exp(s - m[..., None])
    return p / (jnp.sum(p, axis=-1) + jnp.exp(sink - m))[..., None]


def _gated_delta_chunked(q, k, v, beta, g):
    B, L, H, DK = q.shape
    C = DELTA_CHUNK
    N = L // C
    def blocks(t):
        return jnp.moveaxis(t.reshape(B, N, C, H, -1), 3, 2)
    qc = blocks(q) * (DK ** -0.5)
    kc = blocks(k)
    vc = blocks(v)
    bc = jnp.moveaxis(beta.reshape(B, N, C, H), 3, 2)
    gc = jnp.cumsum(jnp.moveaxis(g.reshape(B, N, C, H), 3, 2), axis=-1)
    incl = jnp.tril(jnp.ones((C, C), bool))
    strict = jnp.tril(jnp.ones((C, C), bool), -1)
    diff = gc[..., :, None] - gc[..., None, :]
    decay = jnp.where(incl, jnp.exp(jnp.where(incl, diff, 0.0)), 0.0)
    eye = jnp.eye(C, dtype=jnp.float32)
    lmat = jnp.where(strict, jnp.einsum('bnhcd,bnhsd->bnhcs', kc, kc) * decay * bc[..., None], 0.0)
    tinv = lax.linalg.triangular_solve(eye + lmat, jnp.broadcast_to(eye, lmat.shape),
                                       left_side=True, lower=True, unit_diagonal=True)
    u = tinv @ (vc * bc[..., None])
    w = tinv @ (kc * (bc * jnp.exp(gc))[..., None])
    qk = jnp.einsum('bnhcd,bnhsd->bnhcs', qc, kc) * decay
    qg = qc * jnp.exp(gc)[..., None]
    g_last = gc[..., -1]
    ktail = kc * jnp.exp(g_last[..., None] - gc)[..., None]

    def step(S, xs):
        qg_i, qk_i, u_i, w_i, kt_i, gl_i = xs
        vn = u_i - w_i @ S
        o = qg_i @ S + qk_i @ vn
        S = S * jnp.exp(gl_i)[..., None, None] + jnp.swapaxes(kt_i, -1, -2) @ vn
        return S, o

    xs = tuple(jnp.moveaxis(t, 1, 0) for t in (qg, qk, u, w, ktail, g_last))
    S0 = jnp.zeros((B, H, DK, v.shape[-1]), jnp.float32)
    S, o = lax.scan(step, S0, xs)
    o = jnp.moveaxis(jnp.moveaxis(o, 0, 1), 2, 3).reshape(B, L, H, -1)
    return o, S


def _gated_delta_recurrent(q, k, v, beta, g, S0):
    q = q * (q.shape[-1] ** -0.5)
    def step(S, xs):
        q_t, k_t, v_t, b_t, g_t = xs
        S = S * jnp.exp(g_t)[..., None, None]
        vn = b_t[..., None] * (v_t - jnp.einsum('bhkv,bhk->bhv', S, k_t))
        S = S + jnp.einsum('bhk,bhv->bhkv', k_t, vn)
        return S, jnp.einsum('bhkv,bhk->bhv', S, q_t)
    xs = tuple(jnp.moveaxis(t, 1, 0) for t in (q, k, v, beta, g))
    S, o = lax.scan(step, S0.astype(jnp.float32), xs)
    return jnp.moveaxis(o, 0, 1), S


def _swa_prompt(q, k, v, sink, slopes):
    B, L, _ = q.shape
    W = WINDOW
    NB = L // W
    qb = q.reshape(B, NB, W, N_KV_HEADS, GQA_GROUP, HEAD_DIM)
    def band(t):
        tp = jnp.concatenate([jnp.zeros_like(t[:, :W]), t], axis=1)
        prev = tp[:, :L].reshape(B, NB, W, N_KV_HEADS, HEAD_DIM)
        cur = tp[:, W:].reshape(B, NB, W, N_KV_HEADS, HEAD_DIM)
        return jnp.concatenate([prev, cur], axis=2)
    kb, vb = band(k), band(v)
    s = jnp.einsum('bnqhgd,bnshd->bnhgqs', qb, kb, preferred_element_type=jnp.float32) * (HEAD_DIM ** -0.5)
    qi = jnp.arange(W)[:, None]
    kj = jnp.arange(2 * W)[None, :]
    dist = W + qi - kj
    blk = jnp.arange(NB)[:, None, None]
    valid = (dist >= 0) & (dist < WINDOW) & ((blk - 1) * W + kj >= 0)
    s = s - slopes[None, None, :, :, None, None] * dist.astype(jnp.float32)
    s = jnp.where(valid[None, :, None, None], s, -jnp.inf)
    p = _sink_softmax(s, sink[None, None, :, :, None]).astype(v.dtype)
    o = jnp.einsum('bnhgqs,bnshd->bnqhgd', p, vb)
    return o.reshape(B, L, TOKEN_WIDTH)


def _swa_sample(q, kk, vv, sink, slopes):
    B, T, _ = q.shape
    WB = kk.shape[1] - T
    qh = q.reshape(B, T, N_KV_HEADS, GQA_GROUP, HEAD_DIM)
    s = jnp.einsum('bqhgd,bshd->bhgqs', qh, kk, preferred_element_type=jnp.float32) * (HEAD_DIM ** -0.5)
    dist = (WB + jnp.arange(T))[:, None] - jnp.arange(WB + T)[None, :]
    valid = (dist >= 0) & (dist < WINDOW)
    s = s - slopes[None, :, :, None, None] * dist.astype(jnp.float32)
    s = jnp.where(valid, s, -jnp.inf)
    p = _sink_softmax(s, sink[None, :, :, None]).astype(vv.dtype)
    o = jnp.einsum('bhgqs,bshd->bqhgd', p, vv)
    return o.reshape(B, T, TOKEN_WIDTH)


def _mem_attn(q, mk, mv):
    B, L, _ = q.shape
    qh = q.reshape(B, L, N_MEM_HEADS, MEM_HEAD_DIM)
    s = jnp.einsum('blhd,bmhd->bhlm', qh, mk, preferred_element_type=jnp.float32) * (MEM_HEAD_DIM ** -0.5)
    p = jax.nn.softmax(s, axis=-1).astype(mv.dtype)
    return jnp.einsum('bhlm,bmhd->blhd', p, mv).reshape(B, L, MEM_WIDTH)


def _memory_kv(mem, norm_mem, w_mem_kv):
    memn = _rmsnorm(mem[None], norm_mem[:, None, None, :])
    kv = jnp.einsum('lbmd,ldc->lbmc', memn, w_mem_kv)
    shp = kv.shape[:3] + (N_MEM_HEADS, MEM_HEAD_DIM)
    return kv[..., :MEM_WIDTH].reshape(shp), kv[..., MEM_WIDTH:].reshape(shp)


def _trunk(x, mem_k, mem_v, conv_state, delta_state, buf_k, buf_v,
           w_in_a, w_conv, a_log, dt_bias, w_gate_norm, w_in_b, sinks, norm_kv, w_kv,
           w_out, norm_mix_pre, norm_mix_post, norm_ffn_pre, norm_ffn_post, w_gate_up, w_down):
    prompt = conv_state is None
    B, L, _ = x.shape
    slopes = jnp.asarray(_alibi_slopes(N_Q_HEADS)).reshape(N_KV_HEADS, GQA_GROUP)
    new_conv, new_delta = [], []
    kk = vv = None
    new_bk = new_bv = None
    for l in range(DEPTH):
        if l == N_A_LAYERS:
            kv = _rmsnorm(x, norm_kv) @ w_kv
            k_sh = kv[..., :N_KV_HEADS * HEAD_DIM].reshape(B, L, N_KV_HEADS, HEAD_DIM)
            v_sh = kv[..., N_KV_HEADS * HEAD_DIM:].reshape(B, L, N_KV_HEADS, HEAD_DIM)
            if prompt:
                wb = min(WINDOW, L)
                kk, vv = k_sh, v_sh
                new_bk, new_bv = k_sh[:, L - wb:], v_sh[:, L - wb:]
            else:
                wb = buf_k.shape[1]
                kk = jnp.concatenate([buf_k.astype(x.dtype), k_sh], axis=1)
                vv = jnp.concatenate([buf_v.astype(x.dtype), v_sh], axis=1)
                new_bk, new_bv = kk[:, kk.shape[1] - wb:], vv[:, vv.shape[1] - wb:]
        xn = _rmsnorm(x, norm_mix_pre[l])
        if l < N_A_LAYERS:
            proj = xn @ w_in_a[l]
            qkv = proj[..., :CONV_CH]
            z = proj[..., OFF_Z:OFF_B].astype(jnp.float32).reshape(B, L, N_DELTA_HEADS, HEAD_DIM)
            b_raw = proj[..., OFF_B:OFF_A].astype(jnp.float32)
            a_raw = proj[..., OFF_A:OFF_QM].astype(jnp.float32)
            q_mem = proj[..., OFF_QM:]
            hist = jnp.zeros((B, CONV_W - 1, CONV_CH), x.dtype) if prompt else conv_state[l].astype(x.dtype)
            ext = jnp.concatenate([hist, qkv], axis=1)
            new_conv.append(ext[:, ext.shape[1] - (CONV_W - 1):])
            c = jax.nn.silu(_causal_conv(ext, w_conv[l]).astype(jnp.float32))
            q = _l2norm(c[..., :TOKEN_WIDTH].reshape(B, L, N_DELTA_HEADS, HEAD_DIM))
            k = _l2norm(c[..., TOKEN_WIDTH:2 * TOKEN_WIDTH].reshape(B, L, N_DELTA_HEADS, HEAD_DIM))
            v = c[..., 2 * TOKEN_WIDTH:].reshape(B, L, N_DELTA_HEADS, HEAD_DIM)
            beta = jax.nn.sigmoid(b_raw)
            g = -jnp.exp(a_log[l].astype(jnp.float32)) * jax.nn.softplus(a_raw + dt_bias[l].astype(jnp.float32))
            if prompt:
                o, S = _gated_delta_chunked(q, k, v, beta, g)
            else:
                o, S = _gated_delta_recurrent(q, k, v, beta, g, delta_state[l])
            new_delta.append(S.astype(x.dtype))
            o = o * lax.rsqrt(jnp.mean(o * o, axis=-1, keepdims=True) + EPS) * w_gate_norm[l].astype(jnp.float32)
            tok = (o * jax.nn.silu(z)).reshape(B, L, TOKEN_WIDTH).astype(x.dtype)
        else:
            lb = l - N_A_LAYERS
            proj = xn @ w_in_b[lb]
            q = proj[..., :TOKEN_WIDTH]
            q_mem = proj[..., TOKEN_WIDTH:]
            sink = sinks[lb].astype(jnp.float32).reshape(N_KV_HEADS, GQA_GROUP)
            if prompt:
                tok = _swa_prompt(q, kk, vv, sink, slopes)
            else:
                tok = _swa_sample(q, kk, vv, sink, slopes)
        mo = _mem_attn(q_mem, mem_k[l].astype(x.dtype), mem_v[l].astype(x.dtype))
        mix = jnp.concatenate([tok, mo], axis=-1) @ w_out[l]
        x = x + _rmsnorm(mix, norm_mix_post[l])
        hn = _rmsnorm(x, norm_ffn_pre[l])
        gu = hn @ w_gate_up[l]
        f = (jax.nn.silu(gu[..., :D_FF]) * gu[..., D_FF:]) @ w_down[l]
        x = x + _rmsnorm(f, norm_ffn_post[l])
    return x, jnp.stack(new_conv), jnp.stack(new_delta), new_bk, new_bv


def setup_inputs(seed: int = 0) -> dict:
    key = jax.random.key(seed)
    ks = iter(jax.random.split(key, 40))
    def nrm(shape, scale):
        return jax.random.normal(next(ks), shape, jnp.float32) * scale
    wb = min(WINDOW, PAST_LEN)
    dt = jnp.exp(jax.random.uniform(next(ks), (N_A_LAYERS, N_DELTA_HEADS), jnp.float32,
                                    math.log(1e-3), math.log(1e-1)))
    return {
        'x_prompt': nrm((BATCH, SEQ, D_MODEL), 1.0),
        'x_sample': nrm((DEC_BATCH, DEC_SEQ, D_MODEL), 1.0),
        'cache_mem_k': nrm((DEPTH, DEC_BATCH, N_MEM, N_MEM_HEADS, MEM_HEAD_DIM), 1.0),
        'cache_mem_v': nrm((DEPTH, DEC_BATCH, N_MEM, N_MEM_HEADS, MEM_HEAD_DIM), 1.0),
        'cache_swa_k': nrm((DEC_BATCH, wb, N_KV_HEADS, HEAD_DIM), 1.0),
        'cache_swa_v': nrm((DEC_BATCH, wb, N_KV_HEADS, HEAD_DIM), 1.0),
        'state_conv': nrm((N_A_LAYERS, DEC_BATCH, CONV_W - 1, CONV_CH), 1.0),
        'state_delta': nrm((N_A_LAYERS, DEC_BATCH, N_DELTA_HEADS, HEAD_DIM, HEAD_DIM), 0.1),
        'mem_prompt': nrm((BATCH, N_MEM, D_MODEL), 1.0),
        'w_in_a': nrm((N_A_LAYERS, D_MODEL, A_COLS), D_MODEL ** -0.5),
        'w_conv': nrm((N_A_LAYERS, CONV_W, CONV_CH), CONV_W ** -0.5),
        'a_log': jnp.log(jax.random.uniform(next(ks), (N_A_LAYERS, N_DELTA_HEADS), jnp.float32, 1.0, 16.0)),
        'dt_bias': dt + jnp.log(-jnp.expm1(-dt)),
        'w_gate_norm': 1.0 + nrm((N_A_LAYERS, HEAD_DIM), 0.02),
        'w_in_b': nrm((N_B_LAYERS, D_MODEL, MIX_WIDTH), D_MODEL ** -0.5),
        'sinks': nrm((N_B_LAYERS, N_Q_HEADS), 0.5),
        'norm_kv': 1.0 + nrm((D_MODEL,), 0.02),
        'w_kv': nrm((D_MODEL, 2 * N_KV_HEADS * HEAD_DIM), D_MODEL ** -0.5),
        'norm_mem': 1.0 + nrm((DEPTH, D_MODEL), 0.02),
        'w_mem_kv': nrm((DEPTH, D_MODEL, 2 * MEM_WIDTH), D_MODEL ** -0.5),
        'w_out': nrm((DEPTH, MIX_WIDTH, D_MODEL), MIX_WIDTH ** -0.5),
        'norm_mix_pre': 1.0 + nrm((DEPTH, D_MODEL), 0.02),
        'norm_mix_post': 1.0 + nrm((DEPTH, D_MODEL), 0.02),
        'norm_ffn_pre': 1.0 + nrm((DEPTH, D_MODEL), 0.02),
        'norm_ffn_post': 1.0 + nrm((DEPTH, D_MODEL), 0.02),
        'w_gate_up': nrm((DEPTH, D_MODEL, 2 * D_FF), D_MODEL ** -0.5),
        'w_down': nrm((DEPTH, D_FF, D_MODEL), D_FF ** -0.5),
    }


def reference(x_prompt, x_sample, cache_mem_k, cache_mem_v, cache_swa_k, cache_swa_v, state_conv, state_delta,
              mem_prompt, w_in_a, w_conv, a_log, dt_bias, w_gate_norm, w_in_b, sinks, norm_kv, w_kv,
              norm_mem, w_mem_kv, w_out, norm_mix_pre, norm_mix_post, norm_ffn_pre, norm_ffn_post,
              w_gate_up, w_down):
    weights = dict(w_in_a=w_in_a, w_conv=w_conv, a_log=a_log, dt_bias=dt_bias, w_gate_norm=w_gate_norm,
                   w_in_b=w_in_b, sinks=sinks, norm_kv=norm_kv, w_kv=w_kv, w_out=w_out,
                   norm_mix_pre=norm_mix_pre, norm_mix_post=norm_mix_post, norm_ffn_pre=norm_ffn_pre,
                   norm_ffn_post=norm_ffn_post, w_gate_up=w_gate_up, w_down=w_down)
    mem_k_p, mem_v_p = _memory_kv(mem_prompt, norm_mem, w_mem_kv)
    y_prompt, conv_p, delta_p, swk_p, swv_p = _trunk(x_prompt, mem_k_p, mem_v_p, None, None, None, None, **weights)
    y_sample, conv_s, delta_s, swk_s, swv_s = _trunk(x_sample, cache_mem_k, cache_mem_v, state_conv, state_delta,
                                                     cache_swa_k, cache_swa_v, **weights)
    return (y_prompt, y_sample, mem_k_p, mem_v_p, swk_p, swv_p, conv_p, delta_p, swk_s, swv_s, conv_s, delta_s)
```

```python
import functools
import math

import numpy as np
import jax
import jax.numpy as jnp
from jax import lax
from jax.experimental import pallas as pl
from jax.experimental.pallas import tpu as pltpu

F32 = jnp.float32
BF16 = jnp.bfloat16
EPS = 1e-6

HEAD_DIM = 128
N_MEM_HEADS = 4
N_KV_HEADS = 8
WINDOW = 128
CONV_W = 4
DELTA_CHUNK = 64
LANES = 128
VMEM_LIMIT_BYTES = 56 * 1024 * 1024


def _params(*sem):
    return pltpu.CompilerParams(dimension_semantics=sem, vmem_limit_bytes=VMEM_LIMIT_BYTES)


def _alibi_slopes(n):
    def pow2_slopes(m):
        start = 2.0 ** (-8.0 / m)
        return [start ** (i + 1) for i in range(m)]
    c = 2 ** int(math.floor(math.log2(n)))
    s = pow2_slopes(c)
    if c < n:
        s = s + pow2_slopes(2 * c)[0::2][: n - c]
    return np.asarray(s, np.float32)


def _rms_rows(x, w):
    return x * lax.rsqrt(jnp.mean(x * x, axis=-1, keepdims=True) + EPS) * w


def _silu(x):
    return x * jax.nn.sigmoid(x)


def _softplus(x):
    return jnp.maximum(x, 0.0) + jnp.log1p(jnp.exp(-jnp.abs(x)))


def _row_block(m, target):
    b = min(m, target)
    assert m % b == 0, (m, b)
    return b


def _norm_kernel(x_ref, w_ref, *o_refs):
    x = x_ref[...]
    for j, o_ref in enumerate(o_refs):
        o_ref[...] = _rms_rows(x, w_ref[j:j + 1, :]).astype(o_ref.dtype)


def _norm_cast(x, ws):
    m, d = x.shape
    n = ws.shape[0]
    br = _row_block(m, 256)
    return pl.pallas_call(
        _norm_kernel,
        grid=(m // br,),
        in_specs=[pl.BlockSpec((br, d), lambda i: (i, 0)),
                  pl.BlockSpec((n, d), lambda i: (0, 0))],
        out_specs=[pl.BlockSpec((br, d), lambda i: (i, 0))] * n,
        out_shape=[jax.ShapeDtypeStruct((m, d), BF16)] * n,
        compiler_params=_params("parallel"),
        name="norm_cast",
    )(x, ws)


def _resid_kernel(n_next, x_ref, f_ref, wpost_ref, *refs):
    if n_next:
        wnext_ref, xo_ref = refs[0], refs[1]
        xn_refs = refs[2:]
    else:
        xo_ref = refs[0]
        xn_refs = ()
    xnew = x_ref[...] + _rms_rows(f_ref[...], wpost_ref[...])
    xo_ref[...] = xnew
    for j, o_ref in enumerate(xn_refs):
        o_ref[...] = _rms_rows(xnew, wnext_ref[j:j + 1, :]).astype(o_ref.dtype)


def _resid_norm(x, f, w_post, w_next):
    m, d = x.shape
    n_next = 0 if w_next is None else w_next.shape[0]
    br = _row_block(m, 256)
    row = pl.BlockSpec((br, d), lambda i: (i, 0))
    in_specs = [row, row, pl.BlockSpec((1, d), lambda i: (0, 0))]
    args = [x, f, w_post.reshape(1, d)]
    if n_next:
        in_specs.append(pl.BlockSpec((n_next, d), lambda i: (0, 0)))
        args.append(w_next)
    outs = pl.pallas_call(
        functools.partial(_resid_kernel, n_next),
        grid=(m // br,),
        in_specs=in_specs,
        out_specs=[row] * (1 + n_next),
        out_shape=[jax.ShapeDtypeStruct((m, d), F32)] + [jax.ShapeDtypeStruct((m, d), BF16)] * n_next,
        compiler_params=_params("parallel"),
        name="resid_norm",
    )(*args)
    return outs[0], list(outs[1:])


def _w_spec(w, layer, k, bn, col_block):
    if w.ndim == 3:
        return pl.BlockSpec((None, k, bn), lambda n, m: (layer, 0, col_block(n)))
    return pl.BlockSpec((k, bn), lambda n, m: (0, col_block(n)))


def _mm_kernel(a_ref, w_ref, o_ref, wb_ref):
    @pl.when(pl.program_id(1) == 0)
    def _():
        wb_ref[...] = w_ref[...].astype(BF16)
    o_ref[...] = jnp.dot(a_ref[...], wb_ref[...], preferred_element_type=F32).astype(o_ref.dtype)


def _matmul(a, w, *, layer=0, col_off=0, n_cols=None, out_dtype=F32, bn=512, bm=1024):
    m, k = a.shape
    n_total = w.shape[-1]
    n_cols = n_total - col_off if n_cols is None else n_cols
    bn = min(bn, n_cols)
    assert n_cols % bn == 0 and col_off % bn == 0, (n_cols, col_off, bn)
    bm = _row_block(m, bm)
    off = col_off // bn
    return pl.pallas_call(
        _mm_kernel,
        grid=(n_cols // bn, m // bm),
        in_specs=[pl.BlockSpec((bm, k), lambda n, i: (i, 0)),
                  _w_spec(w, layer, k, bn, lambda n: n + off)],
        out_specs=pl.BlockSpec((bm, bn), lambda n, i: (i, n)),
        out_shape=jax.ShapeDtypeStruct((m, n_cols), out_dtype),
        scratch_shapes=[pltpu.VMEM((k, bn), BF16)],
        compiler_params=_params("arbitrary", "arbitrary"),
        name="matmul",
    )(a, w)


def _swiglu_kernel(a_ref, wg_ref, wu_ref, o_ref, wgb_ref, wub_ref):
    @pl.when(pl.program_id(1) == 0)
    def _():
        wgb_ref[...] = wg_ref[...].astype(BF16)
        wub_ref[...] = wu_ref[...].astype(BF16)
    a = a_ref[...]
    g = jnp.dot(a, wgb_ref[...], preferred_element_type=F32)
    u = jnp.dot(a, wub_ref[...], preferred_element_type=F32)
    o_ref[...] = (_silu(g) * u).astype(o_ref.dtype)


def _swiglu_matmul(a, w, layer, d_ff, *, bn=256, bm=1024):
    m, k = a.shape
    assert d_ff % bn == 0
    bm = _row_block(m, bm)
    nb = d_ff // bn
    return pl.pallas_call(
        _swiglu_kernel,
        grid=(nb, m // bm),
        in_specs=[pl.BlockSpec((bm, k), lambda n, i: (i, 0)),
                  _w_spec(w, layer, k, bn, lambda n: n),
                  _w_spec(w, layer, k, bn, lambda n: n + nb)],
        out_specs=pl.BlockSpec((bm, bn), lambda n, i: (i, n)),
        out_shape=jax.ShapeDtypeStruct((m, d_ff), BF16),
        scratch_shapes=[pltpu.VMEM((k, bn), BF16), pltpu.VMEM((k, bn), BF16)],
        compiler_params=_params("arbitrary", "arbitrary"),
        name="swiglu_matmul",
    )(a, w, w)


def _mm_kacc_kernel(n_chunk, a_ref, w_ref, o_ref):
    kstep = pl.program_id(1)
    a = a_ref[...]
    n = o_ref.shape[1]
    for j in range(n // n_chunk):
        cols = slice(j * n_chunk, (j + 1) * n_chunk)
        p = jnp.dot(a, w_ref[:, cols].astype(BF16), preferred_element_type=F32)

        @pl.when(kstep == 0)
        def _():
            o_ref[:, cols] = p

        @pl.when(kstep > 0)
        def _():
            o_ref[:, cols] += p


def _matmul_kacc(a, w, layer, *, bk=256, bm=1024):
    m, k = a.shape
    n = w.shape[-1]
    assert k % bk == 0
    bm = _row_block(m, bm)
    return pl.pallas_call(
        functools.partial(_mm_kacc_kernel, min(n, 512)),
        grid=(m // bm, k // bk),
        in_specs=[pl.BlockSpec((bm, bk), lambda i, kk: (i, kk)),
                  pl.BlockSpec((None, bk, n), lambda i, kk: (layer, kk, 0))],
        out_specs=pl.BlockSpec((bm, n), lambda i, kk: (i, 0)),
        out_shape=jax.ShapeDtypeStruct((m, n), F32),
        compiler_params=_params("parallel", "arbitrary"),
        name="matmul_kacc",
    )(a, w)


def _mem_attn_kernel(head_dim, q_ref, k_ref, v_ref, o_ref):
    scale = head_dim ** -0.5
    for h in range(N_MEM_HEADS):
        cols = slice(h * head_dim, (h + 1) * head_dim)
        q = q_ref[:, cols]
        k = k_ref[:, cols].astype(BF16)
        v = v_ref[:, cols].astype(BF16)
        s = lax.dot_general(q, k, (((1,), (1,)), ((), ())), preferred_element_type=F32) * scale
        m = jnp.max(s, axis=-1, keepdims=True)
        p = jnp.exp(s - m)
        p = p / jnp.sum(p, axis=-1, keepdims=True)
        o_ref[:, cols] = jnp.dot(p.astype(BF16), v, preferred_element_type=F32).astype(o_ref.dtype)


def _mem_attn(q, q_col, width, mk, mv, layer, k_col, v_col):
    b, l, _ = q.shape
    n_mem = mk.shape[2]
    head_dim = width // N_MEM_HEADS
    bl = _row_block(l, 512)
    return pl.pallas_call(
        functools.partial(_mem_attn_kernel, head_dim),
        grid=(b, l // bl),
        in_specs=[pl.BlockSpec((None, bl, width), lambda i, j: (i, j, q_col)),
                  pl.BlockSpec((None, None, n_mem, width), lambda i, j: (layer, i, 0, k_col)),
                  pl.BlockSpec((None, None, n_mem, width), lambda i, j: (layer, i, 0, v_col))],
        out_specs=pl.BlockSpec((None, bl, width), lambda i, j: (i, j, 0)),
        out_shape=jax.ShapeDtypeStruct((b, l, width), BF16),
        compiler_params=_params("parallel", "parallel"),
        name="mem_attn",
    )(q, mk, mv)


def _sink_softmax(s, sink):
    m = jnp.maximum(jnp.max(s, axis=-1, keepdims=True), sink)
    p = jnp.exp(s - m)
    return p / (jnp.sum(p, axis=-1, keepdims=True) + jnp.exp(sink - m))


def _swa_prompt_kernel(slopes, sink_ref, q_ref, kp_ref, kc_ref, vp_ref, vc_ref, o_ref):
    w = WINDOW
    blk = pl.program_id(1)
    qi = lax.broadcasted_iota(jnp.int32, (w, 2 * w), 0)
    kj = lax.broadcasted_iota(jnp.int32, (w, 2 * w), 1)
    dist = w + qi - kj
    valid = (dist >= 0) & (dist < WINDOW) & ((blk - 1) * w + kj >= 0)
    distf = dist.astype(F32)
    scale = HEAD_DIM ** -0.5
    group = len(slopes) // N_KV_HEADS
    for hk in range(N_KV_HEADS):
        kcols = slice(hk * HEAD_DIM, (hk + 1) * HEAD_DIM)
        kcat = jnp.concatenate([kp_ref[:, kcols], kc_ref[:, kcols]], axis=0).astype(BF16)
        vcat = jnp.concatenate([vp_ref[:, kcols], vc_ref[:, kcols]], axis=0).astype(BF16)
        for g in range(group):
            h = hk * group + g
            cols = slice(h * HEAD_DIM, (h + 1) * HEAD_DIM)
            s = lax.dot_general(q_ref[:, cols], kcat, (((1,), (1,)), ((), ())),
                                preferred_element_type=F32) * scale
            s = s - float(slopes[h]) * distf
            s = jnp.where(valid, s, -jnp.inf)
            p = _sink_softmax(s, sink_ref[h])
            o_ref[:, cols] = jnp.dot(p.astype(BF16), vcat, preferred_element_type=F32).astype(o_ref.dtype)


def _swa_prompt(q, kv, sinks, slopes):
    b, l, _ = q.shape
    w = WINDOW
    kvw = N_KV_HEADS * HEAD_DIM
    tw = len(slopes) * HEAD_DIM
    assert l % w == 0
    prev = lambda i, j: jnp.maximum(j - 1, 0)
    return pl.pallas_call(
        functools.partial(_swa_prompt_kernel, slopes),
        grid=(b, l // w),
        in_specs=[pl.BlockSpec(memory_space=pltpu.SMEM),
                  pl.BlockSpec((None, w, tw), lambda i, j: (i, j, 0)),
                  pl.BlockSpec((None, w, kvw), lambda i, j: (i, prev(i, j), 0)),
                  pl.BlockSpec((None, w, kvw), lambda i, j: (i, j, 0)),
                  pl.BlockSpec((None, w, kvw), lambda i, j: (i, prev(i, j), 1)),
                  pl.BlockSpec((None, w, kvw), lambda i, j: (i, j, 1))],
        out_specs=pl.BlockSpec((None, w, tw), lambda i, j: (i, j, 0)),
        out_shape=jax.ShapeDtypeStruct((b, l, tw), BF16),
        compiler_params=_params("parallel", "parallel"),
        name="swa_prompt",
    )(sinks, q, kv, kv, kv, kv)


def _swa_sample_kernel(slopes, sink_ref, q_ref, k_ref, v_ref, o_ref):
    n_heads = len(slopes)
    group = n_heads // N_KV_HEADS
    wb = k_ref.shape[0]
    q = q_ref[...].astype(BF16)
    head = lax.broadcasted_iota(jnp.int32, (n_heads, wb), 0)
    pos = lax.broadcasted_iota(jnp.int32, (n_heads, wb), 1)
    distf = (wb - 1 - pos).astype(F32)
    slope = jnp.zeros((n_heads, wb), F32)
    sink = jnp.zeros((n_heads, 1), F32)
    head_col = lax.broadcasted_iota(jnp.int32, (n_heads, 1), 0)
    for h in range(n_heads):
        slope = jnp.where(head == h, float(slopes[h]), slope)
        sink = jnp.where(head_col == h, sink_ref[h], sink)
    scale = HEAD_DIM ** -0.5
    s = jnp.zeros((n_heads, wb), F32)
    for hk in range(N_KV_HEADS):
        cols = slice(hk * HEAD_DIM, (hk + 1) * HEAD_DIM)
        s_hk = lax.dot_general(q, k_ref[:, cols].astype(BF16), (((1,), (1,)), ((), ())),
                               preferred_element_type=F32)
        s = jnp.where(head // group == hk, s_hk, s)
    s = s * scale - slope * distf
    p = _sink_softmax(s, sink).astype(BF16)
    o = jnp.zeros((n_heads, HEAD_DIM), F32)
    head_o = lax.broadcasted_iota(jnp.int32, (n_heads, HEAD_DIM), 0)
    for hk in range(N_KV_HEADS):
        cols = slice(hk * HEAD_DIM, (hk + 1) * HEAD_DIM)
        o_hk = jnp.dot(p, v_ref[:, cols].astype(BF16), preferred_element_type=F32)
        o = jnp.where(head_o // group == hk, o_hk, o)
    o_ref[...] = o.astype(o_ref.dtype)


def _swa_sample(q, win_k, win_v, sinks, slopes):
    b, n_heads, _ = q.shape
    wb, kvw = win_k.shape[1:]
    return pl.pallas_call(
        functools.partial(_swa_sample_kernel, slopes),
        grid=(b,),
        in_specs=[pl.BlockSpec(memory_space=pltpu.SMEM),
                  pl.BlockSpec((None, n_heads, HEAD_DIM), lambda i: (i, 0, 0)),
                  pl.BlockSpec((None, wb, kvw), lambda i: (i, 0, 0)),
                  pl.BlockSpec((None, wb, kvw), lambda i: (i, 0, 0))],
        out_specs=pl.BlockSpec((None, n_heads, HEAD_DIM), lambda i: (i, 0, 0)),
        out_shape=jax.ShapeDtypeStruct((b, n_heads, HEAD_DIM), BF16),
        compiler_params=_params("parallel"),
        name="swa_sample",
    )(sinks, q, win_k, win_v)


def _lane_pick(x, lane):
    idx = lax.broadcasted_iota(jnp.int32, x.shape, x.ndim - 1)
    return jnp.sum(jnp.where(idx == lane, x, 0.0), axis=-1, keepdims=True)


def _gates(ba, alog, dtb):
    beta = jax.nn.sigmoid(ba)
    g = -jnp.exp(alog) * _softplus(ba + dtb)
    return beta, g


def _bdot(a, b, contract_b_last, precision=None):
    dims = (((2,), (2 if contract_b_last else 1,)), ((0,), (0,)))
    return lax.dot_general(a, b, dims, preferred_element_type=F32, precision=precision)


def _delta_prompt_kernel(n_heads, q_ref, k_ref, v_ref, z_ref, ba_ref, wq_ref, wk_ref, wv_ref,
                         alog_ref, dtb_ref, gn_ref, o_ref, s_ref, pad_ref):
    l = q_ref.shape[0]
    c = DELTA_CHUNK
    n = l // c
    h = pl.program_id(1)

    def conv(x_ref, w_ref):
        pad_ref[0:8, :] = jnp.zeros((8, HEAD_DIM), F32)
        pad_ref[8:8 + l, :] = x_ref[...]
        base = 8 - (CONV_W - 1)
        out = pad_ref[base:base + l, :] * w_ref[0:1, :]
        for j in range(1, CONV_W):
            out = out + pad_ref[base + j:base + j + l, :] * w_ref[j:j + 1, :]
        return _silu(out)

    def l2norm(x):
        return x * lax.rsqrt(jnp.sum(x * x, axis=-1, keepdims=True) + EPS)

    q = l2norm(conv(q_ref, wq_ref)) * (HEAD_DIM ** -0.5)
    k = l2norm(conv(k_ref, wk_ref))
    v = conv(v_ref, wv_ref)

    beta_all, g_all = _gates(ba_ref[...], alog_ref[...], dtb_ref[...])
    beta = _lane_pick(beta_all, h).reshape(n, c, 1)
    g = _lane_pick(g_all, n_heads + h).reshape(n, c, 1)

    row = lax.broadcasted_iota(jnp.int32, (n, c, c), 1)
    col = lax.broadcasted_iota(jnp.int32, (n, c, c), 2)
    incl = row >= col
    strict = row > col
    eye = row == col
    gc_row = jnp.sum(jnp.where(row <= col, jnp.broadcast_to(g, (n, c, c)), 0.0), axis=1, keepdims=True)
    gc_col = jnp.sum(jnp.where(eye, jnp.broadcast_to(gc_row, (n, c, c)), 0.0), axis=2, keepdims=True)
    diff = gc_col - gc_row
    decay = jnp.where(incl, jnp.exp(jnp.where(incl, diff, 0.0)), 0.0)

    q3 = q.reshape(n, c, HEAD_DIM)
    k3 = k.reshape(n, c, HEAD_DIM)
    v3 = v.reshape(n, c, HEAD_DIM)
    kb = k3.astype(BF16)
    kk = _bdot(kb, kb, True)
    lmat = jnp.where(strict, kk * decay * beta, 0.0)
    x = -lmat
    tinv = jnp.where(eye, 1.0, x)
    p = x
    steps = int(math.log2(c)) - 1
    for _ in range(steps):
        p = _bdot(p, p, False, precision=lax.Precision.HIGHEST)
        tinv = tinv + _bdot(tinv, p, False, precision=lax.Precision.HIGHEST)
    tb = tinv.astype(BF16)
    egc = jnp.exp(gc_col)
    u = _bdot(tb, (v3 * beta).astype(BF16), False)
    w = _bdot(tb, (k3 * (beta * egc)).astype(BF16), False)
    qk = _bdot(q3.astype(BF16), kb, True) * decay
    qg = q3 * egc
    g_last = gc_col[:, c - 1:c, :]
    ktail = k3 * jnp.exp(g_last - gc_col)
    eg_last = jnp.exp(g_last)

    w_b = w.astype(BF16)
    qg_b = qg.astype(BF16)
    qk_b = qk.astype(BF16)
    kt_b = ktail.astype(BF16)
    state = jnp.zeros((HEAD_DIM, HEAD_DIM), F32)
    outs = []
    for i in range(n):
        sb = state.astype(BF16)
        vn = u[i] - jnp.dot(w_b[i], sb, preferred_element_type=F32)
        vnb = vn.astype(BF16)
        outs.append(jnp.dot(qg_b[i], sb, preferred_element_type=F32)
                    + jnp.dot(qk_b[i], vnb, preferred_element_type=F32))
        state = state * eg_last[i] + lax.dot_general(kt_b[i], vnb, (((0,), (0,)), ((), ())),
                                                     preferred_element_type=F32)
    s_ref[...] = state
    o = jnp.concatenate(outs, axis=0)
    o = o * lax.rsqrt(jnp.mean(o * o, axis=-1, keepdims=True) + EPS) * gn_ref[...]
    o_ref[...] = (o * _silu(z_ref[...])).astype(o_ref.dtype)


def _delta_prompt(proj, ba, w_conv, layer, alog_pad, dtb_pad, gate_norm, n_heads):
    b, l, _ = proj.shape
    hd = HEAD_DIM
    tw = n_heads * hd
    col = lambda part: pl.BlockSpec((None, l, hd), lambda i, h: (i, 0, part * n_heads + h))
    wcol = lambda part: pl.BlockSpec((None, CONV_W, hd), lambda i, h: (layer, 0, part * n_heads + h))
    vec = pl.BlockSpec((1, LANES), lambda i, h: (0, 0))
    return pl.pallas_call(
        functools.partial(_delta_prompt_kernel, n_heads),
        grid=(b, n_heads),
        in_specs=[col(0), col(1), col(2), col(3),
                  pl.BlockSpec((None, l, LANES), lambda i, h: (i, 0, 0)),
                  wcol(0), wcol(1), wcol(2), vec, vec, vec],
        out_specs=[pl.BlockSpec((None, l, hd), lambda i, h: (i, 0, h)),
                   pl.BlockSpec((None, None, hd, hd), lambda i, h: (i, h, 0, 0))],
        out_shape=[jax.ShapeDtypeStruct((b, l, tw), BF16),
                   jax.ShapeDtypeStruct((b, n_heads, hd, hd), F32)],
        scratch_shapes=[pltpu.VMEM((l + 8, hd), F32)],
        compiler_params=_params("parallel", "arbitrary"),
        name="delta_prompt",
    )(proj, proj, proj, proj, ba, w_conv, w_conv, w_conv, alog_pad, dtb_pad, gate_norm)


def _delta_sample_kernel(n_heads, x_ref, z_ref, ba_ref, hist_ref, wc_ref, s_ref, alog_ref, dtb_ref, gn_ref,
                         o_ref, hist_o_ref, s_o_ref):
    hd = HEAD_DIM
    tw = n_heads * hd
    x = x_ref[...]
    acc = hist_ref[0:1, :] * wc_ref[0:1, :]
    for j in range(1, CONV_W - 1):
        acc = acc + hist_ref[j:j + 1, :] * wc_ref[j:j + 1, :]
    acc = _silu(acc + x * wc_ref[CONV_W - 1:CONV_W, :])
    hist_o_ref[0:CONV_W - 2, :] = hist_ref[1:CONV_W - 1, :]
    hist_o_ref[CONV_W - 2:CONV_W - 1, :] = x

    beta_all, g_all = _gates(ba_ref[...], alog_ref[...], dtb_ref[...])
    row = lax.broadcasted_iota(jnp.int32, (hd, hd), 0)
    col = lax.broadcasted_iota(jnp.int32, (hd, hd), 1)
    eye = row == col

    def to_col(r):
        return jnp.sum(jnp.where(eye, jnp.broadcast_to(r, (hd, hd)), 0.0), axis=1, keepdims=True)

    def l2norm(r):
        return r * lax.rsqrt(jnp.sum(r * r, axis=-1, keepdims=True) + EPS)

    for h in range(n_heads):
        cols = slice(h * hd, (h + 1) * hd)
        q = l2norm(acc[:, cols]) * (hd ** -0.5)
        k = l2norm(acc[:, tw + h * hd: tw + (h + 1) * hd])
        v = acc[:, 2 * tw + h * hd: 2 * tw + (h + 1) * hd]
        beta = _lane_pick(beta_all, h)
        g = _lane_pick(g_all, n_heads + h)
        k_col = to_col(k)
        s = s_ref[h] * jnp.exp(g)
        vn = beta * (v - jnp.sum(s * k_col, axis=0, keepdims=True))
        s = s + k_col * vn
        s_o_ref[h] = s
        o = jnp.sum(s * to_col(q), axis=0, keepdims=True)
        o = o * lax.rsqrt(jnp.mean(o * o, axis=-1, keepdims=True) + EPS) * gn_ref[...]
        o_ref[:, cols] = (o * _silu(z_ref[:, cols])).astype(o_ref.dtype)


def _delta_sample(proj, ba, state_conv, w_conv, state_delta, layer, alog_pad, dtb_pad, gate_norm, n_heads):
    b = proj.shape[0]
    hd = HEAD_DIM
    tw = n_heads * hd
    hist = CONV_W - 1
    vec = pl.BlockSpec((1, LANES), lambda i: (0, 0))
    return pl.pallas_call(
        functools.partial(_delta_sample_kernel, n_heads),
        grid=(b,),
        in_specs=[pl.BlockSpec((None, 1, 3 * tw), lambda i: (i, 0, 0)),
                  pl.BlockSpec((None, 1, tw), lambda i: (i, 0, 3)),
                  pl.BlockSpec((None, 1, LANES), lambda i: (i, 0, 0)),
                  pl.BlockSpec((None, None, hist, 3 * tw), lambda i: (layer, i, 0, 0)),
                  pl.BlockSpec((None, CONV_W, 3 * tw), lambda i: (layer, 0, 0)),
                  pl.BlockSpec((None, None, n_heads, hd, hd), lambda i: (layer, i, 0, 0, 0)),
                  vec, vec, vec],
        out_specs=[pl.BlockSpec((None, 1, tw), lambda i: (i, 0, 0)),
                   pl.BlockSpec((None, hist, 3 * tw), lambda i: (i, 0, 0)),
                   pl.BlockSpec((None, n_heads, hd, hd), lambda i: (i, 0, 0, 0))],
        out_shape=[jax.ShapeDtypeStruct((b, 1, tw), BF16),
                   jax.ShapeDtypeStruct((b, hist, 3 * tw), F32),
                   jax.ShapeDtypeStruct((b, n_heads, hd, hd), F32)],
        compiler_params=_params("parallel"),
        name="delta_sample",
    )(proj, proj, ba, state_conv, w_conv, state_delta, alog_pad, dtb_pad, gate_norm)


def _trunk(x, mem_k, mem_v, mem_cols, conv_state, delta_state, buf_k, buf_v, wts):
    prompt = conv_state is None
    b, l, d = x.shape
    m = b * l
    depth = wts["w_out"].shape[0]
    n_a = wts["w_in_a"].shape[0]
    d_ff = wts["w_down"].shape[1]
    tw = wts["w_conv"].shape[-1] // 3
    n_heads = tw // HEAD_DIM
    mem_w = d - tw
    kvw = N_KV_HEADS * HEAD_DIM
    slopes = _alibi_slopes(n_heads)
    off_b = 4 * tw
    off_qm = off_b + 2 * n_heads

    def lane_pad(vals):
        return jnp.zeros((1, LANES), F32).at[0, n_heads:2 * n_heads].set(vals.astype(F32))

    x2 = x.reshape(m, d)
    (xn,) = _norm_cast(x2, wts["norm_mix_pre"][0:1])
    new_conv, new_delta = [], []
    kv3 = win_k = win_v = None
    new_bk = new_bv = None
    for layer in range(depth):
        if layer < n_a:
            w_in = wts["w_in_a"]
            proj = _matmul(xn, w_in, layer=layer, n_cols=off_b).reshape(b, l, off_b)
            w_ba = jnp.pad(w_in[layer, :, off_b:off_qm], ((0, 0), (0, LANES - 2 * n_heads)))
            ba = _matmul(xn, w_ba).reshape(b, l, LANES)
            q_mem = _matmul(xn, w_in[layer, :, off_qm:], out_dtype=BF16).reshape(b, l, mem_w)
            alog_pad = lane_pad(wts["a_log"][layer])
            dtb_pad = lane_pad(wts["dt_bias"][layer])
            gate_norm = wts["w_gate_norm"][layer].reshape(1, HEAD_DIM)
            if prompt:
                tok, s_new = _delta_prompt(proj, ba, wts["w_conv"], layer, alog_pad, dtb_pad, gate_norm, n_heads)
                new_conv.append(proj[:, l - (CONV_W - 1):, :3 * tw])
            else:
                tok, hist_new, s_new = _delta_sample(proj, ba, conv_state, wts["w_conv"], delta_state, layer,
                                                     alog_pad, dtb_pad, gate_norm, n_heads)
                new_conv.append(hist_new)
            new_delta.append(s_new)
            mo = _mem_attn(q_mem, 0, mem_w, mem_k, mem_v, layer, *mem_cols)
        else:
            lb = layer - n_a
            sinks = wts["sinks"][lb].astype(F32)
            if prompt:
                proj = _matmul(xn, wts["w_in_b"], layer=lb, out_dtype=BF16).reshape(b, l, d)
                tok = _swa_prompt(proj, kv3, sinks, slopes)
                mo = _mem_attn(proj, tw // mem_w, mem_w, mem_k, mem_v, layer, *mem_cols)
            else:
                q = _matmul(xn, wts["w_in_b"], layer=lb, n_cols=tw).reshape(b, n_heads, HEAD_DIM)
                q_mem = _matmul(xn, wts["w_in_b"], layer=lb, col_off=tw, out_dtype=BF16).reshape(b, l, mem_w)
                tok = _swa_sample(q, win_k, win_v, sinks, slopes).reshape(b, l, tw)
                mo = _mem_attn(q_mem, 0, mem_w, mem_k, mem_v, layer, *mem_cols)
        mixed = jnp.concatenate([tok, mo], axis=-1).reshape(m, d)
        mix = _matmul(mixed, wts["w_out"], layer=layer)
        x2, (hn,) = _resid_norm(x2, mix, wts["norm_mix_post"][layer], wts["norm_ffn_pre"][layer:layer + 1])
        hidden = _swiglu_matmul(hn, wts["w_gate_up"], layer, d_ff)
        f = _matmul_kacc(hidden, wts["w_down"], layer)
        if layer + 1 == depth:
            x2, _ = _resid_norm(x2, f, wts["norm_ffn_post"][layer], None)
        elif layer + 1 == n_a:
            w_next = jnp.stack([wts["norm_mix_pre"][layer + 1], wts["norm_kv"]])
            x2, (xn, xkv) = _resid_norm(x2, f, wts["norm_ffn_post"][layer], w_next)
            kv3 = _matmul(xkv, wts["w_kv"]).reshape(b, l, 2 * kvw)
            k_sh, v_sh = kv3[..., :kvw], kv3[..., kvw:]
            if prompt:
                wb = min(WINDOW, l)
                new_bk, new_bv = k_sh[:, l - wb:], v_sh[:, l - wb:]
            else:
                wb = buf_k.shape[1]
                win_k = jnp.concatenate([buf_k.reshape(b, wb, kvw), k_sh], axis=1)[:, l:]
                win_v = jnp.concatenate([buf_v.reshape(b, wb, kvw), v_sh], axis=1)[:, l:]
                new_bk, new_bv = win_k, win_v
        else:
            x2, (xn,) = _resid_norm(x2, f, wts["norm_ffn_post"][layer], wts["norm_mix_pre"][layer + 1:layer + 2])
    kv_shape = (b, -1, N_KV_HEADS, HEAD_DIM)
    return (x2.reshape(b, l, d), jnp.stack(new_conv), jnp.stack(new_delta),
            new_bk.reshape(kv_shape), new_bv.reshape(kv_shape))


def kernel(x_prompt, x_sample, cache_mem_k, cache_mem_v, cache_swa_k, cache_swa_v, state_conv, state_delta, mem_prompt, w_in_a, w_conv, a_log, dt_bias, w_gate_norm, w_in_b, sinks, norm_kv, w_kv, norm_mem, w_mem_kv, w_out, norm_mix_pre, norm_mix_post, norm_ffn_pre, norm_ffn_post, w_gate_up, w_down):
    wts = dict(w_in_a=w_in_a, w_conv=w_conv, a_log=a_log, dt_bias=dt_bias, w_gate_norm=w_gate_norm,
               w_in_b=w_in_b, sinks=sinks, norm_kv=norm_kv, w_kv=w_kv, w_out=w_out,
               norm_mix_pre=norm_mix_pre, norm_mix_post=norm_mix_post, norm_ffn_pre=norm_ffn_pre,
               norm_ffn_post=norm_ffn_post, w_gate_up=w_gate_up, w_down=w_down)
    depth = w_out.shape[0]
    bp, n_mem, d = mem_prompt.shape
    mem_w = w_mem_kv.shape[-1] // 2
    mem_shape = (depth, bp, n_mem, N_MEM_HEADS, mem_w // N_MEM_HEADS)

    memn = _norm_cast(mem_prompt.reshape(bp * n_mem, d), norm_mem)
    mem_kv = jnp.stack([_matmul(memn[i], w_mem_kv, layer=i) for i in range(depth)])
    mem_kv = mem_kv.reshape(depth, bp, n_mem, 2 * mem_w)
    mem_k_p = mem_kv[..., :mem_w].reshape(mem_shape)
    mem_v_p = mem_kv[..., mem_w:].reshape(mem_shape)
    y_p, conv_p, delta_p, swk_p, swv_p = _trunk(x_prompt, mem_kv, mem_kv, (0, 1), None, None, None, None, wts)

    bs = x_sample.shape[0]
    cmk = cache_mem_k.reshape(depth, bs, n_mem, mem_w)
    cmv = cache_mem_v.reshape(depth, bs, n_mem, mem_w)
    y_s, conv_s, delta_s, swk_s, swv_s = _trunk(x_sample, cmk, cmv, (0, 0), state_conv, state_delta,
                                                cache_swa_k, cache_swa_v, wts)
    return (y_p, y_s, mem_k_p, mem_v_p, swk_p, swv_p, conv_p, delta_p, swk_s, swv_s, conv_s, delta_s)
```

```python
import functools
import math

import numpy as np
import jax
import jax.numpy as jnp
from jax import lax
from jax.experimental import pallas as pl
from jax.experimental.pallas import tpu as pltpu

F32 = jnp.float32
BF16 = jnp.bfloat16
EPS = 1e-6

HEAD_DIM = 128
N_MEM_HEADS = 4
N_KV_HEADS = 8
WINDOW = 128
CONV_W = 4
DELTA_CHUNK = 64
DELTA_HEADS_PER_STEP = 2
LANES = 128
VMEM_LIMIT_BYTES = 56 * 1024 * 1024


def _params(*sem):
    return pltpu.CompilerParams(dimension_semantics=sem, vmem_limit_bytes=VMEM_LIMIT_BYTES)


def _alibi_slopes(n):
    def pow2_slopes(m):
        start = 2.0 ** (-8.0 / m)
        return [start ** (i + 1) for i in range(m)]
    c = 2 ** int(math.floor(math.log2(n)))
    s = pow2_slopes(c)
    if c < n:
        s = s + pow2_slopes(2 * c)[0::2][: n - c]
    return np.asarray(s, np.float32)


def _rms_rows(x, w):
    return x * lax.rsqrt(jnp.mean(x * x, axis=-1, keepdims=True) + EPS) * w


def _silu(x):
    return x * jax.nn.sigmoid(x)


def _softplus(x):
    return jnp.maximum(x, 0.0) + jnp.log1p(jnp.exp(-jnp.abs(x)))


def _row_block(m, target):
    b = min(m, target)
    assert m % b == 0, (m, b)
    return b


def _norm_kernel(x_ref, w_ref, *o_refs):
    x = x_ref[...]
    for j, o_ref in enumerate(o_refs):
        o_ref[...] = _rms_rows(x, w_ref[j:j + 1, :]).astype(o_ref.dtype)


def _norm_cast(x, ws):
    m, d = x.shape
    n = ws.shape[0]
    br = _row_block(m, 256)
    return pl.pallas_call(
        _norm_kernel,
        grid=(m // br,),
        in_specs=[pl.BlockSpec((br, d), lambda i: (i, 0)),
                  pl.BlockSpec((n, d), lambda i: (0, 0))],
        out_specs=[pl.BlockSpec((br, d), lambda i: (i, 0))] * n,
        out_shape=[jax.ShapeDtypeStruct((m, d), BF16)] * n,
        compiler_params=_params("parallel"),
        name="norm_cast",
    )(x, ws)


def _resid_kernel(n_next, x_ref, f_ref, wpost_ref, *refs):
    if n_next:
        wnext_ref, xo_ref = refs[0], refs[1]
        xn_refs = refs[2:]
    else:
        xo_ref = refs[0]
        xn_refs = ()
    xnew = x_ref[...] + _rms_rows(f_ref[...], wpost_ref[...])
    xo_ref[...] = xnew
    for j, o_ref in enumerate(xn_refs):
        o_ref[...] = _rms_rows(xnew, wnext_ref[j:j + 1, :]).astype(o_ref.dtype)


def _resid_norm(x, f, w_post, w_next):
    m, d = x.shape
    n_next = 0 if w_next is None else w_next.shape[0]
    br = _row_block(m, 256)
    row = pl.BlockSpec((br, d), lambda i: (i, 0))
    in_specs = [row, row, pl.BlockSpec((1, d), lambda i: (0, 0))]
    args = [x, f, w_post.reshape(1, d)]
    if n_next:
        in_specs.append(pl.BlockSpec((n_next, d), lambda i: (0, 0)))
        args.append(w_next)
    outs = pl.pallas_call(
        functools.partial(_resid_kernel, n_next),
        grid=(m // br,),
        in_specs=in_specs,
        out_specs=[row] * (1 + n_next),
        out_shape=[jax.ShapeDtypeStruct((m, d), F32)] + [jax.ShapeDtypeStruct((m, d), BF16)] * n_next,
        compiler_params=_params("parallel"),
        name="resid_norm",
    )(*args)
    return outs[0], list(outs[1:])


def _w_spec(w, layer, k, bn, col_block):
    if w.ndim == 3:
        return pl.BlockSpec((None, k, bn), lambda n, m: (layer, 0, col_block(n)))
    return pl.BlockSpec((k, bn), lambda n, m: (0, col_block(n)))


def _mm_kernel(a_ref, w_ref, o_ref, wb_ref):
    @pl.when(pl.program_id(1) == 0)
    def _():
        wb_ref[...] = w_ref[...].astype(BF16)
    o_ref[...] = jnp.dot(a_ref[...], wb_ref[...], preferred_element_type=F32).astype(o_ref.dtype)


def _matmul(a, w, *, layer=0, col_off=0, n_cols=None, out_dtype=F32, bn=512, bm=1024):
    m, k = a.shape
    n_total = w.shape[-1]
    n_cols = n_total - col_off if n_cols is None else n_cols
    bn = min(bn, n_cols)
    assert n_cols % bn == 0 and col_off % bn == 0, (n_cols, col_off, bn)
    bm = _row_block(m, bm)
    off = col_off // bn
    return pl.pallas_call(
        _mm_kernel,
        grid=(n_cols // bn, m // bm),
        in_specs=[pl.BlockSpec((bm, k), lambda n, i: (i, 0)),
                  _w_spec(w, layer, k, bn, lambda n: n + off)],
        out_specs=pl.BlockSpec((bm, bn), lambda n, i: (i, n)),
        out_shape=jax.ShapeDtypeStruct((m, n_cols), out_dtype),
        scratch_shapes=[pltpu.VMEM((k, bn), BF16)],
        compiler_params=_params("arbitrary", "arbitrary"),
        name="matmul",
    )(a, w)


def _swiglu_kernel(a_ref, wg_ref, wu_ref, o_ref, wgb_ref, wub_ref):
    @pl.when(pl.program_id(1) == 0)
    def _():
        wgb_ref[...] = wg_ref[...].astype(BF16)
        wub_ref[...] = wu_ref[...].astype(BF16)
    a = a_ref[...]
    g = jnp.dot(a, wgb_ref[...], preferred_element_type=F32)
    u = jnp.dot(a, wub_ref[...], preferred_element_type=F32)
    o_ref[...] = (_silu(g) * u).astype(o_ref.dtype)


def _swiglu_matmul(a, w, layer, d_ff, *, bn=256, bm=1024):
    m, k = a.shape
    assert d_ff % bn == 0
    bm = _row_block(m, bm)
    nb = d_ff // bn
    return pl.pallas_call(
        _swiglu_kernel,
        grid=(nb, m // bm),
        in_specs=[pl.BlockSpec((bm, k), lambda n, i: (i, 0)),
                  _w_spec(w, layer, k, bn, lambda n: n),
                  _w_spec(w, layer, k, bn, lambda n: n + nb)],
        out_specs=pl.BlockSpec((bm, bn), lambda n, i: (i, n)),
        out_shape=jax.ShapeDtypeStruct((m, d_ff), BF16),
        scratch_shapes=[pltpu.VMEM((k, bn), BF16), pltpu.VMEM((k, bn), BF16)],
        compiler_params=_params("arbitrary", "arbitrary"),
        name="swiglu_matmul",
    )(a, w, w)


def _cast_kernel(x_ref, o_ref):
    o_ref[...] = x_ref[...].astype(o_ref.dtype)


def _cast_bf16(w):
    lyr, k, n = w.shape
    budget = 12 * 1024 * 1024
    bk = max(r for r in range(16, k + 1, 16) if k % r == 0 and r * n * 4 <= budget)
    return pl.pallas_call(
        _cast_kernel,
        grid=(lyr, k // bk),
        in_specs=[pl.BlockSpec((None, bk, n), lambda i, j: (i, j, 0))],
        out_specs=pl.BlockSpec((None, bk, n), lambda i, j: (i, j, 0)),
        out_shape=jax.ShapeDtypeStruct(w.shape, BF16),
        compiler_params=_params("parallel", "parallel"),
        name="cast_bf16",
    )(w)


def _mm_bf16w_kernel(a_ref, w_ref, o_ref):
    o_ref[...] = jnp.dot(a_ref[...], w_ref[...], preferred_element_type=F32).astype(o_ref.dtype)


def _matmul_bf16w(a, w, layer, *, bn=512, bm=512):
    m, k = a.shape
    n = w.shape[-1]
    assert n % bn == 0
    bm = _row_block(m, bm)
    return pl.pallas_call(
        _mm_bf16w_kernel,
        grid=(n // bn, m // bm),
        in_specs=[pl.BlockSpec((bm, k), lambda j, i: (i, 0)),
                  pl.BlockSpec((None, k, bn), lambda j, i: (layer, 0, j))],
        out_specs=pl.BlockSpec((bm, bn), lambda j, i: (i, j)),
        out_shape=jax.ShapeDtypeStruct((m, n), F32),
        compiler_params=_params("parallel", "parallel"),
        name="matmul_bf16w",
    )(a, w)


def _mem_attn_kernel(head_dim, q_ref, k_ref, v_ref, o_ref):
    scale = head_dim ** -0.5
    for h in range(N_MEM_HEADS):
        cols = slice(h * head_dim, (h + 1) * head_dim)
        q = q_ref[:, cols]
        k = k_ref[:, cols].astype(BF16)
        v = v_ref[:, cols].astype(BF16)
        s = lax.dot_general(q, k, (((1,), (1,)), ((), ())), preferred_element_type=F32) * scale
        m = jnp.max(s, axis=-1, keepdims=True)
        p = jnp.exp(s - m)
        p = p / jnp.sum(p, axis=-1, keepdims=True)
        o_ref[:, cols] = jnp.dot(p.astype(BF16), v, preferred_element_type=F32).astype(o_ref.dtype)


def _mem_attn(q, q_col, width, mk, mv, layer, k_col, v_col):
    b, l, _ = q.shape
    n_mem = mk.shape[2]
    head_dim = width // N_MEM_HEADS
    bl = _row_block(l, 512)
    return pl.pallas_call(
        functools.partial(_mem_attn_kernel, head_dim),
        grid=(b, l // bl),
        in_specs=[pl.BlockSpec((None, bl, width), lambda i, j: (i, j, q_col)),
                  pl.BlockSpec((None, None, n_mem, width), lambda i, j: (layer, i, 0, k_col)),
                  pl.BlockSpec((None, None, n_mem, width), lambda i, j: (layer, i, 0, v_col))],
        out_specs=pl.BlockSpec((None, bl, width), lambda i, j: (i, j, 0)),
        out_shape=jax.ShapeDtypeStruct((b, l, width), BF16),
        compiler_params=_params("parallel", "parallel"),
        name="mem_attn",
    )(q, mk, mv)


def _sink_softmax(s, sink):
    m = jnp.maximum(jnp.max(s, axis=-1, keepdims=True), sink)
    p = jnp.exp(s - m)
    return p / (jnp.sum(p, axis=-1, keepdims=True) + jnp.exp(sink - m))


def _swa_prompt_kernel(slopes, sink_ref, q_ref, kp_ref, kc_ref, vp_ref, vc_ref, o_ref):
    w = WINDOW
    blk = pl.program_id(1)
    qi = lax.broadcasted_iota(jnp.int32, (w, 2 * w), 0)
    kj = lax.broadcasted_iota(jnp.int32, (w, 2 * w), 1)
    dist = w + qi - kj
    valid = (dist >= 0) & (dist < WINDOW) & ((blk - 1) * w + kj >= 0)
    distf = dist.astype(F32)
    scale = HEAD_DIM ** -0.5
    group = len(slopes) // N_KV_HEADS
    for hk in range(N_KV_HEADS):
        kcols = slice(hk * HEAD_DIM, (hk + 1) * HEAD_DIM)
        kcat = jnp.concatenate([kp_ref[:, kcols], kc_ref[:, kcols]], axis=0).astype(BF16)
        vcat = jnp.concatenate([vp_ref[:, kcols], vc_ref[:, kcols]], axis=0).astype(BF16)
        for g in range(group):
            h = hk * group + g
            cols = slice(h * HEAD_DIM, (h + 1) * HEAD_DIM)
            s = lax.dot_general(q_ref[:, cols], kcat, (((1,), (1,)), ((), ())),
                                preferred_element_type=F32) * scale
            s = s - float(slopes[h]) * distf
            s = jnp.where(valid, s, -jnp.inf)
            p = _sink_softmax(s, sink_ref[h])
            o_ref[:, cols] = jnp.dot(p.astype(BF16), vcat, preferred_element_type=F32).astype(o_ref.dtype)


def _swa_prompt(q, kv, sinks, slopes):
    b, l, _ = q.shape
    w = WINDOW
    kvw = N_KV_HEADS * HEAD_DIM
    tw = len(slopes) * HEAD_DIM
    assert l % w == 0
    prev = lambda i, j: jnp.maximum(j - 1, 0)
    return pl.pallas_call(
        functools.partial(_swa_prompt_kernel, slopes),
        grid=(b, l // w),
        in_specs=[pl.BlockSpec(memory_space=pltpu.SMEM),
                  pl.BlockSpec((None, w, tw), lambda i, j: (i, j, 0)),
                  pl.BlockSpec((None, w, kvw), lambda i, j: (i, prev(i, j), 0)),
                  pl.BlockSpec((None, w, kvw), lambda i, j: (i, j, 0)),
                  pl.BlockSpec((None, w, kvw), lambda i, j: (i, prev(i, j), 1)),
                  pl.BlockSpec((None, w, kvw), lambda i, j: (i, j, 1))],
        out_specs=pl.BlockSpec((None, w, tw), lambda i, j: (i, j, 0)),
        out_shape=jax.ShapeDtypeStruct((b, l, tw), BF16),
        compiler_params=_params("parallel", "parallel"),
        name="swa_prompt",
    )(sinks, q, kv, kv, kv, kv)


def _swa_sample_kernel(slopes, sink_ref, q_ref, k_ref, v_ref, o_ref):
    n_heads = len(slopes)
    group = n_heads // N_KV_HEADS
    wb = k_ref.shape[0]
    q = q_ref[...].astype(BF16)
    head = lax.broadcasted_iota(jnp.int32, (n_heads, wb), 0)
    pos = lax.broadcasted_iota(jnp.int32, (n_heads, wb), 1)
    distf = (wb - 1 - pos).astype(F32)
    slope = jnp.zeros((n_heads, wb), F32)
    sink = jnp.zeros((n_heads, 1), F32)
    head_col = lax.broadcasted_iota(jnp.int32, (n_heads, 1), 0)
    for h in range(n_heads):
        slope = jnp.where(head == h, float(slopes[h]), slope)
        sink = jnp.where(head_col == h, sink_ref[h], sink)
    scale = HEAD_DIM ** -0.5
    s = jnp.zeros((n_heads, wb), F32)
    for hk in range(N_KV_HEADS):
        cols = slice(hk * HEAD_DIM, (hk + 1) * HEAD_DIM)
        s_hk = lax.dot_general(q, k_ref[:, cols].astype(BF16), (((1,), (1,)), ((), ())),
                               preferred_element_type=F32)
        s = jnp.where(head // group == hk, s_hk, s)
    s = s * scale - slope * distf
    p = _sink_softmax(s, sink).astype(BF16)
    o = jnp.zeros((n_heads, HEAD_DIM), F32)
    head_o = lax.broadcasted_iota(jnp.int32, (n_heads, HEAD_DIM), 0)
    for hk in range(N_KV_HEADS):
        cols = slice(hk * HEAD_DIM, (hk + 1) * HEAD_DIM)
        o_hk = jnp.dot(p, v_ref[:, cols].astype(BF16), preferred_element_type=F32)
        o = jnp.where(head_o // group == hk, o_hk, o)
    o_ref[...] = o.astype(o_ref.dtype)


def _swa_sample(q, win_k, win_v, sinks, slopes):
    b, n_heads, _ = q.shape
    wb, kvw = win_k.shape[1:]
    return pl.pallas_call(
        functools.partial(_swa_sample_kernel, slopes),
        grid=(b,),
        in_specs=[pl.BlockSpec(memory_space=pltpu.SMEM),
                  pl.BlockSpec((None, n_heads, HEAD_DIM), lambda i: (i, 0, 0)),
                  pl.BlockSpec((None, wb, kvw), lambda i: (i, 0, 0)),
                  pl.BlockSpec((None, wb, kvw), lambda i: (i, 0, 0))],
        out_specs=pl.BlockSpec((None, n_heads, HEAD_DIM), lambda i: (i, 0, 0)),
        out_shape=jax.ShapeDtypeStruct((b, n_heads, HEAD_DIM), BF16),
        compiler_params=_params("parallel"),
        name="swa_sample",
    )(sinks, q, win_k, win_v)


def _lane_pick(x, lane):
    idx = lax.broadcasted_iota(jnp.int32, x.shape, x.ndim - 1)
    return jnp.sum(jnp.where(idx == lane, x, 0.0), axis=-1, keepdims=True)


def _gates(ba, alog, dtb):
    beta = jax.nn.sigmoid(ba)
    g = -jnp.exp(alog) * _softplus(ba + dtb)
    return beta, g


def _bdot(a, b, contract_b_last, precision=None):
    dims = (((2,), (2 if contract_b_last else 1,)), ((0,), (0,)))
    return lax.dot_general(a, b, dims, preferred_element_type=F32, precision=precision)


def _delta_prompt_kernel(n_heads, hb, q_ref, k_ref, v_ref, z_ref, ba_ref, wq_ref, wk_ref, wv_ref,
                         alog_ref, dtb_ref, gn_ref, o_ref, s_ref, pad_ref, beta_ref, g_ref):
    l = q_ref.shape[0]
    c = DELTA_CHUNK
    n = l // c
    hd = HEAD_DIM
    hstep = pl.program_id(1)

    @pl.when(hstep == 0)
    def _():
        beta_all, g_all = _gates(ba_ref[...], alog_ref[...], dtb_ref[...])
        beta_ref[...] = beta_all
        g_ref[...] = g_all

    pad_ref[0:8, :] = jnp.zeros((8, hd), F32)

    def conv(x_ref, w_ref, cols):
        pad_ref[8:8 + l, :] = x_ref[:, cols]
        base = 8 - (CONV_W - 1)
        out = pad_ref[base:base + l, :] * w_ref[0:1, cols]
        for j in range(1, CONV_W):
            out = out + pad_ref[base + j:base + j + l, :] * w_ref[j:j + 1, cols]
        return _silu(out)

    def l2norm(x):
        return x * lax.rsqrt(jnp.sum(x * x, axis=-1, keepdims=True) + EPS)

    row = lax.broadcasted_iota(jnp.int32, (1, c, c), 1)
    col = lax.broadcasted_iota(jnp.int32, (1, c, c), 2)
    incl = row >= col
    strict = row > col
    eye = row == col

    def lower_left(shift):
        return (((row >> (shift + 1)) == (col >> (shift + 1)))
                & (((row >> shift) & 1) == 1) & (((col >> shift) & 1) == 0))

    def prepare(hh):
        cols = slice(hh * hd, (hh + 1) * hd)
        head = hstep * hb + hh
        q3 = (l2norm(conv(q_ref, wq_ref, cols)) * (hd ** -0.5)).reshape(n, c, hd)
        k3 = l2norm(conv(k_ref, wk_ref, cols)).reshape(n, c, hd)
        v3 = conv(v_ref, wv_ref, cols).reshape(n, c, hd)
        beta = _lane_pick(beta_ref[...], head).reshape(n, c, 1)
        g = _lane_pick(g_ref[...], n_heads + head).reshape(n, c, 1)
        gc_row = jnp.sum(jnp.where(row <= col, jnp.broadcast_to(g, (n, c, c)), 0.0), axis=1, keepdims=True)
        gc_col = jnp.sum(jnp.where(eye, jnp.broadcast_to(gc_row, (n, c, c)), 0.0), axis=2, keepdims=True)
        decay = jnp.where(incl, jnp.exp(jnp.where(incl, gc_col - gc_row, 0.0)), 0.0)

        kb = k3.astype(BF16)
        qkk = _bdot(jnp.concatenate([q3, k3], axis=1).astype(BF16), kb, True)
        qk = qkk[:, :c] * decay
        lmat = jnp.where(strict, qkk[:, c:] * decay * beta, 0.0)
        tinv = jnp.where(eye, 1.0, 0.0) - jnp.where(lower_left(0), lmat, 0.0)
        for shift in range(1, int(math.log2(c))):
            cs = jnp.where(lower_left(shift), lmat, 0.0).astype(BF16)
            tb = tinv.astype(BF16)
            tinv = tinv - _bdot(_bdot(tb, cs, False).astype(BF16), tb, False)
        egc = jnp.exp(gc_col)
        rhs = jnp.concatenate([v3 * beta, k3 * (beta * egc)], axis=2).astype(BF16)
        uw = _bdot(tinv.astype(BF16), rhs, False)
        g_last = gc_col[:, c - 1:c, :]
        return dict(
            u=uw[:, :, :hd],
            wq=jnp.concatenate([uw[:, :, hd:], q3 * egc], axis=1).astype(BF16),
            qk=qk.astype(BF16),
            kt=(k3 * jnp.exp(g_last - gc_col)).astype(BF16),
            eg_last=jnp.exp(g_last))

    heads = [prepare(hh) for hh in range(hb)]
    states = [jnp.zeros((hd, hd), F32) for _ in range(hb)]
    outs = [[] for _ in range(hb)]
    for i in range(n):
        for hh, p in enumerate(heads):
            sb = states[hh].astype(BF16)
            ws = jnp.dot(p["wq"][i], sb, preferred_element_type=F32)
            vnb = (p["u"][i] - ws[:c]).astype(BF16)
            outs[hh].append(ws[c:] + jnp.dot(p["qk"][i], vnb, preferred_element_type=F32))
            states[hh] = states[hh] * p["eg_last"][i] + lax.dot_general(
                p["kt"][i], vnb, (((0,), (0,)), ((), ())), preferred_element_type=F32)
    for hh in range(hb):
        cols = slice(hh * hd, (hh + 1) * hd)
        s_ref[hh] = states[hh]
        o = jnp.concatenate(outs[hh], axis=0)
        o = o * lax.rsqrt(jnp.mean(o * o, axis=-1, keepdims=True) + EPS) * gn_ref[...]
        o_ref[:, cols] = (o * _silu(z_ref[:, cols])).astype(o_ref.dtype)


def _delta_prompt(proj, ba, w_conv, layer, alog_pad, dtb_pad, gate_norm, n_heads):
    b, l, _ = proj.shape
    hd = HEAD_DIM
    tw = n_heads * hd
    hb = DELTA_HEADS_PER_STEP
    assert n_heads % hb == 0
    steps = n_heads // hb
    col = lambda part: pl.BlockSpec((None, l, hb * hd), lambda i, h: (i, 0, part * steps + h))
    wcol = lambda part: pl.BlockSpec((None, CONV_W, hb * hd), lambda i, h: (layer, 0, part * steps + h))
    vec = pl.BlockSpec((1, LANES), lambda i, h: (0, 0))
    return pl.pallas_call(
        functools.partial(_delta_prompt_kernel, n_heads, hb),
        grid=(b, steps),
        in_specs=[col(0), col(1), col(2), col(3),
                  pl.BlockSpec((None, l, LANES), lambda i, h: (i, 0, 0)),
                  wcol(0), wcol(1), wcol(2), vec, vec, vec],
        out_specs=[pl.BlockSpec((None, l, hb * hd), lambda i, h: (i, 0, h)),
                   pl.BlockSpec((None, hb, hd, hd), lambda i, h: (i, h, 0, 0))],
        out_shape=[jax.ShapeDtypeStruct((b, l, tw), BF16),
                   jax.ShapeDtypeStruct((b, n_heads, hd, hd), F32)],
        scratch_shapes=[pltpu.VMEM((l + 8, hd), F32), pltpu.VMEM((l, LANES), F32), pltpu.VMEM((l, LANES), F32)],
        compiler_params=_params("parallel", "arbitrary"),
        name="delta_prompt",
    )(proj, proj, proj, proj, ba, w_conv, w_conv, w_conv, alog_pad, dtb_pad, gate_norm)


def _delta_sample_kernel(n_heads, x_ref, z_ref, ba_ref, hist_ref, wc_ref, s_ref, alog_ref, dtb_ref, gn_ref,
                         o_ref, hist_o_ref, s_o_ref):
    hd = HEAD_DIM
    tw = n_heads * hd
    x = x_ref[...]
    acc = hist_ref[0:1, :] * wc_ref[0:1, :]
    for j in range(1, CONV_W - 1):
        acc = acc + hist_ref[j:j + 1, :] * wc_ref[j:j + 1, :]
    acc = _silu(acc + x * wc_ref[CONV_W - 1:CONV_W, :])
    hist_o_ref[0:CONV_W - 2, :] = hist_ref[1:CONV_W - 1, :]
    hist_o_ref[CONV_W - 2:CONV_W - 1, :] = x

    beta_all, g_all = _gates(ba_ref[...], alog_ref[...], dtb_ref[...])
    row = lax.broadcasted_iota(jnp.int32, (hd, hd), 0)
    col = lax.broadcasted_iota(jnp.int32, (hd, hd), 1)
    eye = row == col

    def to_col(r):
        return jnp.sum(jnp.where(eye, jnp.broadcast_to(r, (hd, hd)), 0.0), axis=1, keepdims=True)

    def l2norm(r):
        return r * lax.rsqrt(jnp.sum(r * r, axis=-1, keepdims=True) + EPS)

    for h in range(n_heads):
        cols = slice(h * hd, (h + 1) * hd)
        q = l2norm(acc[:, cols]) * (hd ** -0.5)
        k = l2norm(acc[:, tw + h * hd: tw + (h + 1) * hd])
        v = acc[:, 2 * tw + h * hd: 2 * tw + (h + 1) * hd]
        beta = _lane_pick(beta_all, h)
        g = _lane_pick(g_all, n_heads + h)
        k_col = to_col(k)
        s = s_ref[h] * jnp.exp(g)
        vn = beta * (v - jnp.sum(s * k_col, axis=0, keepdims=True))
        s = s + k_col * vn
        s_o_ref[h] = s
        o = jnp.sum(s * to_col(q), axis=0, keepdims=True)
        o = o * lax.rsqrt(jnp.mean(o * o, axis=-1, keepdims=True) + EPS) * gn_ref[...]
        o_ref[:, cols] = (o * _silu(z_ref[:, cols])).astype(o_ref.dtype)


def _delta_sample(proj, ba, state_conv, w_conv, state_delta, layer, alog_pad, dtb_pad, gate_norm, n_heads):
    b = proj.shape[0]
    hd = HEAD_DIM
    tw = n_heads * hd
    hist = CONV_W - 1
    vec = pl.BlockSpec((1, LANES), lambda i: (0, 0))
    return pl.pallas_call(
        functools.partial(_delta_sample_kernel, n_heads),
        grid=(b,),
        in_specs=[pl.BlockSpec((None, 1, 3 * tw), lambda i: (i, 0, 0)),
                  pl.BlockSpec((None, 1, tw), lambda i: (i, 0, 3)),
                  pl.BlockSpec((None, 1, LANES), lambda i: (i, 0, 0)),
                  pl.BlockSpec((None, None, hist, 3 * tw), lambda i: (layer, i, 0, 0)),
                  pl.BlockSpec((None, CONV_W, 3 * tw), lambda i: (layer, 0, 0)),
                  pl.BlockSpec((None, None, n_heads, hd, hd), lambda i: (layer, i, 0, 0, 0)),
                  vec, vec, vec],
        out_specs=[pl.BlockSpec((None, 1, tw), lambda i: (i, 0, 0)),
                   pl.BlockSpec((None, hist, 3 * tw), lambda i: (i, 0, 0)),
                   pl.BlockSpec((None, n_heads, hd, hd), lambda i: (i, 0, 0, 0))],
        out_shape=[jax.ShapeDtypeStruct((b, 1, tw), BF16),
                   jax.ShapeDtypeStruct((b, hist, 3 * tw), F32),
                   jax.ShapeDtypeStruct((b, n_heads, hd, hd), F32)],
        compiler_params=_params("parallel"),
        name="delta_sample",
    )(proj, proj, ba, state_conv, w_conv, state_delta, alog_pad, dtb_pad, gate_norm)


def _trunk(x, mem_k, mem_v, mem_cols, conv_state, delta_state, buf_k, buf_v, wts):
    prompt = conv_state is None
    b, l, d = x.shape
    m = b * l
    depth = wts["w_out"].shape[0]
    n_a = wts["w_in_a"].shape[0]
    d_ff = wts["w_down"].shape[1]
    tw = wts["w_conv"].shape[-1] // 3
    n_heads = tw // HEAD_DIM
    mem_w = d - tw
    kvw = N_KV_HEADS * HEAD_DIM
    slopes = _alibi_slopes(n_heads)
    off_b = 4 * tw
    off_qm = off_b + 2 * n_heads

    def lane_pad(vals):
        return jnp.zeros((1, LANES), F32).at[0, n_heads:2 * n_heads].set(vals.astype(F32))

    x2 = x.reshape(m, d)
    (xn,) = _norm_cast(x2, wts["norm_mix_pre"][0:1])
    new_conv, new_delta = [], []
    kv3 = win_k = win_v = None
    new_bk = new_bv = None
    for layer in range(depth):
        if layer < n_a:
            w_in = wts["w_in_a"]
            proj = _matmul(xn, w_in, layer=layer, n_cols=off_b).reshape(b, l, off_b)
            w_ba = jnp.pad(w_in[layer, :, off_b:off_qm], ((0, 0), (0, LANES - 2 * n_heads)))
            ba = _matmul(xn, w_ba).reshape(b, l, LANES)
            q_mem = _matmul(xn, w_in[layer, :, off_qm:], out_dtype=BF16).reshape(b, l, mem_w)
            alog_pad = lane_pad(wts["a_log"][layer])
            dtb_pad = lane_pad(wts["dt_bias"][layer])
            gate_norm = wts["w_gate_norm"][layer].reshape(1, HEAD_DIM)
            if prompt:
                tok, s_new = _delta_prompt(proj, ba, wts["w_conv"], layer, alog_pad, dtb_pad, gate_norm, n_heads)
                new_conv.append(proj[:, l - (CONV_W - 1):, :3 * tw])
            else:
                tok, hist_new, s_new = _delta_sample(proj, ba, conv_state, wts["w_conv"], delta_state, layer,
                                                     alog_pad, dtb_pad, gate_norm, n_heads)
                new_conv.append(hist_new)
            new_delta.append(s_new)
            mo = _mem_attn(q_mem, 0, mem_w, mem_k, mem_v, layer, *mem_cols)
        else:
            lb = layer - n_a
            sinks = wts["sinks"][lb].astype(F32)
            if prompt:
                proj = _matmul(xn, wts["w_in_b"], layer=lb, out_dtype=BF16).reshape(b, l, d)
                tok = _swa_prompt(proj, kv3, sinks, slopes)
                mo = _mem_attn(proj, tw // mem_w, mem_w, mem_k, mem_v, layer, *mem_cols)
            else:
                q = _matmul(xn, wts["w_in_b"], layer=lb, n_cols=tw).reshape(b, n_heads, HEAD_DIM)
                q_mem = _matmul(xn, wts["w_in_b"], layer=lb, col_off=tw, out_dtype=BF16).reshape(b, l, mem_w)
                tok = _swa_sample(q, win_k, win_v, sinks, slopes).reshape(b, l, tw)
                mo = _mem_attn(q_mem, 0, mem_w, mem_k, mem_v, layer, *mem_cols)
        mixed = jnp.concatenate([tok, mo], axis=-1).reshape(m, d)
        mix = _matmul(mixed, wts["w_out"], layer=layer)
        x2, (hn,) = _resid_norm(x2, mix, wts["norm_mix_post"][layer], wts["norm_ffn_pre"][layer:layer + 1])
        hidden = _swiglu_matmul(hn, wts["w_gate_up"], layer, d_ff)
        f = _matmul_bf16w(hidden, wts["w_down_bf16"], layer)
        if layer + 1 == depth:
            x2, _ = _resid_norm(x2, f, wts["norm_ffn_post"][layer], None)
        elif layer + 1 == n_a:
            w_next = jnp.stack([wts["norm_mix_pre"][layer + 1], wts["norm_kv"]])
            x2, (xn, xkv) = _resid_norm(x2, f, wts["norm_ffn_post"][layer], w_next)
            kv3 = _matmul(xkv, wts["w_kv"]).reshape(b, l, 2 * kvw)
            k_sh, v_sh = kv3[..., :kvw], kv3[..., kvw:]
            if prompt:
                wb = min(WINDOW, l)
                new_bk, new_bv = k_sh[:, l - wb:], v_sh[:, l - wb:]
            else:
                wb = buf_k.shape[1]
                win_k = jnp.concatenate([buf_k.reshape(b, wb, kvw), k_sh], axis=1)[:, l:]
                win_v = jnp.concatenate([buf_v.reshape(b, wb, kvw), v_sh], axis=1)[:, l:]
                new_bk, new_bv = win_k, win_v
        else:
            x2, (xn,) = _resid_norm(x2, f, wts["norm_ffn_post"][layer], wts["norm_mix_pre"][layer + 1:layer + 2])
    kv_shape = (b, -1, N_KV_HEADS, HEAD_DIM)
    return (x2.reshape(b, l, d), jnp.stack(new_conv), jnp.stack(new_delta),
            new_bk.reshape(kv_shape), new_bv.reshape(kv_shape))


def kernel(x_prompt, x_sample, cache_mem_k, cache_mem_v, cache_swa_k, cache_swa_v, state_conv, state_delta, mem_prompt, w_in_a, w_conv, a_log, dt_bias, w_gate_norm, w_in_b, sinks, norm_kv, w_kv, norm_mem, w_mem_kv, w_out, norm_mix_pre, norm_mix_post, norm_ffn_pre, norm_ffn_post, w_gate_up, w_down):
    wts = dict(w_in_a=w_in_a, w_conv=w_conv, a_log=a_log, dt_bias=dt_bias, w_gate_norm=w_gate_norm,
               w_in_b=w_in_b, sinks=sinks, norm_kv=norm_kv, w_kv=w_kv, w_out=w_out,
               norm_mix_pre=norm_mix_pre, norm_mix_post=norm_mix_post, norm_ffn_pre=norm_ffn_pre,
               norm_ffn_post=norm_ffn_post, w_gate_up=w_gate_up, w_down=w_down,
               w_down_bf16=_cast_bf16(w_down))
    depth = w_out.shape[0]
    bp, n_mem, d = mem_prompt.shape
    mem_w = w_mem_kv.shape[-1] // 2
    mem_shape = (depth, bp, n_mem, N_MEM_HEADS, mem_w // N_MEM_HEADS)

    memn = _norm_cast(mem_prompt.reshape(bp * n_mem, d), norm_mem)
    mem_kv = jnp.stack([_matmul(memn[i], w_mem_kv, layer=i) for i in range(depth)])
    mem_kv = mem_kv.reshape(depth, bp, n_mem, 2 * mem_w)
    mem_k_p = mem_kv[..., :mem_w].reshape(mem_shape)
    mem_v_p = mem_kv[..., mem_w:].reshape(mem_shape)
    y_p, conv_p, delta_p, swk_p, swv_p = _trunk(x_prompt, mem_kv, mem_kv, (0, 1), None, None, None, None, wts)

    bs = x_sample.shape[0]
    cmk = cache_mem_k.reshape(depth, bs, n_mem, mem_w)
    cmv = cache_mem_v.reshape(depth, bs, n_mem, mem_w)
    y_s, conv_s, delta_s, swk_s, swv_s = _trunk(x_sample, cmk, cmv, (0, 0), state_conv, state_delta,
                                                cache_swa_k, cache_swa_v, wts)
    return (y_p, y_s, mem_k_p, mem_v_p, swk_p, swv_p, conv_p, delta_p, swk_s, swv_s, conv_s, delta_s)
```

```python
import functools
import math

import numpy as np
import jax
import jax.numpy as jnp
from jax import lax
from jax.experimental import pallas as pl
from jax.experimental.pallas import tpu as pltpu

F32 = jnp.float32
BF16 = jnp.bfloat16
EPS = 1e-6

HEAD_DIM = 128
N_MEM_HEADS = 4
N_KV_HEADS = 8
WINDOW = 128
CONV_W = 4
DELTA_CHUNK = 64
DELTA_HEADS_PER_STEP = 2
LANES = 128
VMEM_LIMIT_BYTES = 56 * 1024 * 1024


def _params(*sem):
    return pltpu.CompilerParams(dimension_semantics=sem, vmem_limit_bytes=VMEM_LIMIT_BYTES)


def _alibi_slopes(n):
    def pow2_slopes(m):
        start = 2.0 ** (-8.0 / m)
        return [start ** (i + 1) for i in range(m)]
    c = 2 ** int(math.floor(math.log2(n)))
    s = pow2_slopes(c)
    if c < n:
        s = s + pow2_slopes(2 * c)[0::2][: n - c]
    return np.asarray(s, np.float32)


def _rms_rows(x, w):
    return x * lax.rsqrt(jnp.mean(x * x, axis=-1, keepdims=True) + EPS) * w


def _silu(x):
    return x * jax.nn.sigmoid(x)


def _softplus(x):
    return jnp.maximum(x, 0.0) + jnp.log1p(jnp.exp(-jnp.abs(x)))


def _row_block(m, target):
    b = min(m, target)
    assert m % b == 0, (m, b)
    return b


def _norm_kernel(x_ref, w_ref, *o_refs):
    x = x_ref[...]
    for j, o_ref in enumerate(o_refs):
        o_ref[...] = _rms_rows(x, w_ref[j:j + 1, :]).astype(o_ref.dtype)


def _norm_cast(x, ws):
    m, d = x.shape
    n = ws.shape[0]
    br = _row_block(m, 256)
    return pl.pallas_call(
        _norm_kernel,
        grid=(m // br,),
        in_specs=[pl.BlockSpec((br, d), lambda i: (i, 0)),
                  pl.BlockSpec((n, d), lambda i: (0, 0))],
        out_specs=[pl.BlockSpec((br, d), lambda i: (i, 0))] * n,
        out_shape=[jax.ShapeDtypeStruct((m, d), BF16)] * n,
        compiler_params=_params("parallel"),
        name="norm_cast",
    )(x, ws)


def _resid_kernel(n_next, x_ref, f_ref, wpost_ref, *refs):
    if n_next:
        wnext_ref, xo_ref = refs[0], refs[1]
        xn_refs = refs[2:]
    else:
        xo_ref = refs[0]
        xn_refs = ()
    xnew = x_ref[...] + _rms_rows(f_ref[...], wpost_ref[...])
    xo_ref[...] = xnew
    for j, o_ref in enumerate(xn_refs):
        o_ref[...] = _rms_rows(xnew, wnext_ref[j:j + 1, :]).astype(o_ref.dtype)


def _resid_norm(x, f, w_post, w_next):
    m, d = x.shape
    n_next = 0 if w_next is None else w_next.shape[0]
    br = _row_block(m, 256)
    row = pl.BlockSpec((br, d), lambda i: (i, 0))
    in_specs = [row, row, pl.BlockSpec((1, d), lambda i: (0, 0))]
    args = [x, f, w_post.reshape(1, d)]
    if n_next:
        in_specs.append(pl.BlockSpec((n_next, d), lambda i: (0, 0)))
        args.append(w_next)
    outs = pl.pallas_call(
        functools.partial(_resid_kernel, n_next),
        grid=(m // br,),
        in_specs=in_specs,
        out_specs=[row] * (1 + n_next),
        out_shape=[jax.ShapeDtypeStruct((m, d), F32)] + [jax.ShapeDtypeStruct((m, d), BF16)] * n_next,
        compiler_params=_params("parallel"),
        name="resid_norm",
    )(*args)
    return outs[0], list(outs[1:])


def _w_spec(w, layer, k, bn, col_block, w_is_nk=False):
    shape = (bn, k) if w_is_nk else (k, bn)
    pos = (lambda n: (col_block(n), 0)) if w_is_nk else (lambda n: (0, col_block(n)))
    if w.ndim == 3:
        return pl.BlockSpec((None,) + shape, lambda n, m: (layer,) + pos(n))
    return pl.BlockSpec(shape, lambda n, m: pos(n))


def _rows_with_tail(a_ref, tail_ref):
    return jnp.concatenate([a_ref[...], tail_ref[...]], axis=0)


def _mm_kernel(has_tail, w_is_nk, *refs):
    if has_tail:
        a_ref, tail_ref, w_ref, o_ref, otail_ref, wb_ref = refs
    else:
        a_ref, w_ref, o_ref, wb_ref = refs
    step = pl.program_id(1)

    @pl.when(step == 0)
    def _():
        wb_ref[...] = w_ref[...].astype(BF16)

    def mm(rows):
        if w_is_nk:
            return lax.dot_general(rows, wb_ref[...], (((1,), (1,)), ((), ())), preferred_element_type=F32)
        return jnp.dot(rows, wb_ref[...], preferred_element_type=F32)

    if not has_tail:
        o_ref[...] = mm(a_ref[...]).astype(o_ref.dtype)
        return
    last = pl.num_programs(1) - 1

    @pl.when(step < last)
    def _():
        o_ref[...] = mm(a_ref[...]).astype(o_ref.dtype)

    @pl.when(step == last)
    def _():
        bm = a_ref.shape[0]
        r = mm(_rows_with_tail(a_ref, tail_ref))
        o_ref[...] = r[:bm].astype(o_ref.dtype)
        otail_ref[...] = r[bm:].astype(otail_ref.dtype)


def _matmul(a, w, *, tail=None, layer=0, col_off=0, n_cols=None, out_dtype=F32, bn=512, bm=1024, w_is_nk=False):
    m, k = a.shape
    n_total = w.shape[-2] if w_is_nk else w.shape[-1]
    n_cols = n_total - col_off if n_cols is None else n_cols
    bn = min(bn, n_cols)
    assert n_cols % bn == 0 and col_off % bn == 0, (n_cols, col_off, bn)
    bm = _row_block(m, bm)
    off = col_off // bn
    in_specs = [pl.BlockSpec((bm, k), lambda n, i: (i, 0))]
    out_specs = [pl.BlockSpec((bm, bn), lambda n, i: (i, n))]
    out_shape = [jax.ShapeDtypeStruct((m, n_cols), out_dtype)]
    args = [a]
    if tail is not None:
        mt = tail.shape[0]
        in_specs.append(pl.BlockSpec((mt, k), lambda n, i: (0, 0)))
        out_specs.append(pl.BlockSpec((mt, bn), lambda n, i: (0, n)))
        out_shape.append(jax.ShapeDtypeStruct((mt, n_cols), out_dtype))
        args.append(tail)
    in_specs.append(_w_spec(w, layer, k, bn, lambda n: n + off, w_is_nk))
    outs = pl.pallas_call(
        functools.partial(_mm_kernel, tail is not None, w_is_nk),
        grid=(n_cols // bn, m // bm),
        in_specs=in_specs,
        out_specs=out_specs,
        out_shape=out_shape,
        scratch_shapes=[pltpu.VMEM((bn, k) if w_is_nk else (k, bn), BF16)],
        compiler_params=_params("arbitrary", "arbitrary"),
        name="matmul",
    )(*args, w)
    return outs[0] if tail is None else tuple(outs)


def _swiglu_kernel(a_ref, tail_ref, wg_ref, wu_ref, o_ref, otail_ref, wgb_ref, wub_ref):
    step = pl.program_id(1)
    last = pl.num_programs(1) - 1

    @pl.when(step == 0)
    def _():
        wgb_ref[...] = wg_ref[...].astype(BF16)
        wub_ref[...] = wu_ref[...].astype(BF16)

    def gated(rows):
        g = jnp.dot(rows, wgb_ref[...], preferred_element_type=F32)
        u = jnp.dot(rows, wub_ref[...], preferred_element_type=F32)
        return _silu(g) * u

    @pl.when(step < last)
    def _():
        o_ref[...] = gated(a_ref[...]).astype(o_ref.dtype)

    @pl.when(step == last)
    def _():
        bm = a_ref.shape[0]
        r = gated(_rows_with_tail(a_ref, tail_ref))
        o_ref[...] = r[:bm].astype(o_ref.dtype)
        otail_ref[...] = r[bm:].astype(otail_ref.dtype)


def _swiglu_matmul(a, tail, w, layer, d_ff, *, bn=256, bm=1024):
    m, k = a.shape
    mt = tail.shape[0]
    assert d_ff % bn == 0
    bm = _row_block(m, bm)
    nb = d_ff // bn
    return pl.pallas_call(
        _swiglu_kernel,
        grid=(nb, m // bm),
        in_specs=[pl.BlockSpec((bm, k), lambda n, i: (i, 0)),
                  pl.BlockSpec((mt, k), lambda n, i: (0, 0)),
                  _w_spec(w, layer, k, bn, lambda n: n),
                  _w_spec(w, layer, k, bn, lambda n: n + nb)],
        out_specs=[pl.BlockSpec((bm, bn), lambda n, i: (i, n)),
                   pl.BlockSpec((mt, bn), lambda n, i: (0, n))],
        out_shape=[jax.ShapeDtypeStruct((m, d_ff), BF16), jax.ShapeDtypeStruct((mt, d_ff), BF16)],
        scratch_shapes=[pltpu.VMEM((k, bn), BF16), pltpu.VMEM((k, bn), BF16)],
        compiler_params=_params("arbitrary", "arbitrary"),
        name="swiglu_matmul",
    )(a, tail, w, w)


def _cast_kernel(x_ref, o_ref):
    o_ref[...] = x_ref[...].astype(o_ref.dtype)


def _cast_bf16(w):
    lyr, k, n = w.shape
    budget = 12 * 1024 * 1024
    bk = max(r for r in range(16, k + 1, 16) if k % r == 0 and r * n * 4 <= budget)
    return pl.pallas_call(
        _cast_kernel,
        grid=(lyr, k // bk),
        in_specs=[pl.BlockSpec((None, bk, n), lambda i, j: (i, j, 0))],
        out_specs=pl.BlockSpec((None, bk, n), lambda i, j: (i, j, 0)),
        out_shape=jax.ShapeDtypeStruct(w.shape, BF16),
        compiler_params=_params("parallel", "parallel"),
        name="cast_bf16",
    )(w)


def _mm_bf16w_kernel(a_ref, w_ref, o_ref):
    o_ref[...] = jnp.dot(a_ref[...], w_ref[...], preferred_element_type=F32).astype(o_ref.dtype)


def _matmul_bf16w(a, w, layer, *, bn=512, bm=512):
    m, k = a.shape
    n = w.shape[-1]
    assert n % bn == 0
    bm = _row_block(m, bm)
    return pl.pallas_call(
        _mm_bf16w_kernel,
        grid=(n // bn, m // bm),
        in_specs=[pl.BlockSpec((bm, k), lambda j, i: (i, 0)),
                  pl.BlockSpec((None, k, bn), lambda j, i: (layer, 0, j))],
        out_specs=pl.BlockSpec((bm, bn), lambda j, i: (i, j)),
        out_shape=jax.ShapeDtypeStruct((m, n), F32),
        compiler_params=_params("parallel", "parallel"),
        name="matmul_bf16w",
    )(a, w)


def _mem_attn_kernel(head_dim, q_ref, k_ref, v_ref, o_ref):
    scale = head_dim ** -0.5
    for h in range(N_MEM_HEADS):
        cols = slice(h * head_dim, (h + 1) * head_dim)
        q = q_ref[:, cols]
        k = k_ref[:, cols].astype(BF16)
        v = v_ref[:, cols].astype(BF16)
        s = lax.dot_general(q, k, (((1,), (1,)), ((), ())), preferred_element_type=F32) * scale
        m = jnp.max(s, axis=-1, keepdims=True)
        p = jnp.exp(s - m)
        p = p / jnp.sum(p, axis=-1, keepdims=True)
        o_ref[:, cols] = jnp.dot(p.astype(BF16), v, preferred_element_type=F32).astype(o_ref.dtype)


def _mem_attn(q, q_col, width, mk, mv, layer, k_col, v_col):
    b, l, _ = q.shape
    n_mem = mk.shape[2]
    head_dim = width // N_MEM_HEADS
    bl = _row_block(l, 512)
    return pl.pallas_call(
        functools.partial(_mem_attn_kernel, head_dim),
        grid=(b, l // bl),
        in_specs=[pl.BlockSpec((None, bl, width), lambda i, j: (i, j, q_col)),
                  pl.BlockSpec((None, None, n_mem, width), lambda i, j: (layer, i, 0, k_col)),
                  pl.BlockSpec((None, None, n_mem, width), lambda i, j: (layer, i, 0, v_col))],
        out_specs=pl.BlockSpec((None, bl, width), lambda i, j: (i, j, 0)),
        out_shape=jax.ShapeDtypeStruct((b, l, width), BF16),
        compiler_params=_params("parallel", "parallel"),
        name="mem_attn",
    )(q, mk, mv)


def _sink_softmax(s, sink):
    m = jnp.maximum(jnp.max(s, axis=-1, keepdims=True), sink)
    p = jnp.exp(s - m)
    return p / (jnp.sum(p, axis=-1, keepdims=True) + jnp.exp(sink - m))


def _swa_prompt_kernel(slopes, sink_ref, q_ref, kp_ref, kc_ref, vp_ref, vc_ref, o_ref):
    w = WINDOW
    blk = pl.program_id(1)
    qi = lax.broadcasted_iota(jnp.int32, (w, 2 * w), 0)
    kj = lax.broadcasted_iota(jnp.int32, (w, 2 * w), 1)
    dist = w + qi - kj
    valid = (dist >= 0) & (dist < WINDOW) & ((blk - 1) * w + kj >= 0)
    distf = dist.astype(F32)
    scale = HEAD_DIM ** -0.5
    group = len(slopes) // N_KV_HEADS
    for hk in range(N_KV_HEADS):
        kcols = slice(hk * HEAD_DIM, (hk + 1) * HEAD_DIM)
        kcat = jnp.concatenate([kp_ref[:, kcols], kc_ref[:, kcols]], axis=0).astype(BF16)
        vcat = jnp.concatenate([vp_ref[:, kcols], vc_ref[:, kcols]], axis=0).astype(BF16)
        for g in range(group):
            h = hk * group + g
            cols = slice(h * HEAD_DIM, (h + 1) * HEAD_DIM)
            s = lax.dot_general(q_ref[:, cols], kcat, (((1,), (1,)), ((), ())),
                                preferred_element_type=F32) * scale
            s = s - float(slopes[h]) * distf
            s = jnp.where(valid, s, -jnp.inf)
            p = _sink_softmax(s, sink_ref[h])
            o_ref[:, cols] = jnp.dot(p.astype(BF16), vcat, preferred_element_type=F32).astype(o_ref.dtype)


def _swa_prompt(q, kv, sinks, slopes):
    b, l, _ = q.shape
    w = WINDOW
    kvw = N_KV_HEADS * HEAD_DIM
    tw = len(slopes) * HEAD_DIM
    assert l % w == 0
    prev = lambda i, j: jnp.maximum(j - 1, 0)
    return pl.pallas_call(
        functools.partial(_swa_prompt_kernel, slopes),
        grid=(b, l // w),
        in_specs=[pl.BlockSpec(memory_space=pltpu.SMEM),
                  pl.BlockSpec((None, w, tw), lambda i, j: (i, j, 0)),
                  pl.BlockSpec((None, w, kvw), lambda i, j: (i, prev(i, j), 0)),
                  pl.BlockSpec((None, w, kvw), lambda i, j: (i, j, 0)),
                  pl.BlockSpec((None, w, kvw), lambda i, j: (i, prev(i, j), 1)),
                  pl.BlockSpec((None, w, kvw), lambda i, j: (i, j, 1))],
        out_specs=pl.BlockSpec((None, w, tw), lambda i, j: (i, j, 0)),
        out_shape=jax.ShapeDtypeStruct((b, l, tw), BF16),
        compiler_params=_params("parallel", "parallel"),
        name="swa_prompt",
    )(sinks, q, kv, kv, kv, kv)


def _swa_sample_kernel(slopes, sink_ref, q_ref, k_ref, v_ref, o_ref):
    n_heads = len(slopes)
    group = n_heads // N_KV_HEADS
    wb = k_ref.shape[0]
    q = q_ref[...].astype(BF16)
    head = lax.broadcasted_iota(jnp.int32, (n_heads, wb), 0)
    pos = lax.broadcasted_iota(jnp.int32, (n_heads, wb), 1)
    distf = (wb - 1 - pos).astype(F32)
    slope = jnp.zeros((n_heads, wb), F32)
    sink = jnp.zeros((n_heads, 1), F32)
    head_col = lax.broadcasted_iota(jnp.int32, (n_heads, 1), 0)
    for h in range(n_heads):
        slope = jnp.where(head == h, float(slopes[h]), slope)
        sink = jnp.where(head_col == h, sink_ref[h], sink)
    scale = HEAD_DIM ** -0.5
    s = jnp.zeros((n_heads, wb), F32)
    for hk in range(N_KV_HEADS):
        cols = slice(hk * HEAD_DIM, (hk + 1) * HEAD_DIM)
        s_hk = lax.dot_general(q, k_ref[:, cols].astype(BF16), (((1,), (1,)), ((), ())),
                               preferred_element_type=F32)
        s = jnp.where(head // group == hk, s_hk, s)
    s = s * scale - slope * distf
    p = _sink_softmax(s, sink).astype(BF16)
    o = jnp.zeros((n_heads, HEAD_DIM), F32)
    head_o = lax.broadcasted_iota(jnp.int32, (n_heads, HEAD_DIM), 0)
    for hk in range(N_KV_HEADS):
        cols = slice(hk * HEAD_DIM, (hk + 1) * HEAD_DIM)
        o_hk = jnp.dot(p, v_ref[:, cols].astype(BF16), preferred_element_type=F32)
        o = jnp.where(head_o // group == hk, o_hk, o)
    o_ref[...] = o.astype(o_ref.dtype)


def _swa_sample(q, win_k, win_v, sinks, slopes):
    b, n_heads, _ = q.shape
    wb, kvw = win_k.shape[1:]
    return pl.pallas_call(
        functools.partial(_swa_sample_kernel, slopes),
        grid=(b,),
        in_specs=[pl.BlockSpec(memory_space=pltpu.SMEM),
                  pl.BlockSpec((None, n_heads, HEAD_DIM), lambda i: (i, 0, 0)),
                  pl.BlockSpec((None, wb, kvw), lambda i: (i, 0, 0)),
                  pl.BlockSpec((None, wb, kvw), lambda i: (i, 0, 0))],
        out_specs=pl.BlockSpec((None, n_heads, HEAD_DIM), lambda i: (i, 0, 0)),
        out_shape=jax.ShapeDtypeStruct((b, n_heads, HEAD_DIM), BF16),
        compiler_params=_params("parallel"),
        name="swa_sample",
    )(sinks, q, win_k, win_v)


def _lane_pick(x, lane):
    idx = lax.broadcasted_iota(jnp.int32, x.shape, x.ndim - 1)
    return jnp.sum(jnp.where(idx == lane, x, 0.0), axis=-1, keepdims=True)


def _gates(ba, alog, dtb):
    beta = jax.nn.sigmoid(ba)
    g = -jnp.exp(alog) * _softplus(ba + dtb)
    return beta, g


def _bdot(a, b, contract_b_last, precision=None):
    dims = (((2,), (2 if contract_b_last else 1,)), ((0,), (0,)))
    return lax.dot_general(a, b, dims, preferred_element_type=F32, precision=precision)


def _delta_prompt_kernel(n_heads, hb, q_ref, k_ref, v_ref, z_ref, ba_ref, wq_ref, wk_ref, wv_ref,
                         alog_ref, dtb_ref, gn_ref, o_ref, s_ref, pad_ref, beta_ref, g_ref):
    l = q_ref.shape[0]
    c = DELTA_CHUNK
    n = l // c
    hd = HEAD_DIM
    hstep = pl.program_id(1)

    @pl.when(hstep == 0)
    def _():
        beta_all, g_all = _gates(ba_ref[...], alog_ref[...], dtb_ref[...])
        beta_ref[...] = beta_all
        g_ref[...] = g_all

    pad_ref[0:8, :] = jnp.zeros((8, hd), F32)

    def conv(x_ref, w_ref, cols):
        pad_ref[8:8 + l, :] = x_ref[:, cols]
        base = 8 - (CONV_W - 1)
        out = pad_ref[base:base + l, :] * w_ref[0:1, cols]
        for j in range(1, CONV_W):
            out = out + pad_ref[base + j:base + j + l, :] * w_ref[j:j + 1, cols]
        return _silu(out)

    def l2norm(x):
        return x * lax.rsqrt(jnp.sum(x * x, axis=-1, keepdims=True) + EPS)

    row = lax.broadcasted_iota(jnp.int32, (1, c, c), 1)
    col = lax.broadcasted_iota(jnp.int32, (1, c, c), 2)
    incl = row >= col
    strict = row > col
    eye = row == col

    def lower_left(shift):
        return (((row >> (shift + 1)) == (col >> (shift + 1)))
                & (((row >> shift) & 1) == 1) & (((col >> shift) & 1) == 0))

    def prepare(hh):
        cols = slice(hh * hd, (hh + 1) * hd)
        head = hstep * hb + hh
        q3 = (l2norm(conv(q_ref, wq_ref, cols)) * (hd ** -0.5)).reshape(n, c, hd)
        k3 = l2norm(conv(k_ref, wk_ref, cols)).reshape(n, c, hd)
        v3 = conv(v_ref, wv_ref, cols).reshape(n, c, hd)
        beta = _lane_pick(beta_ref[...], head).reshape(n, c, 1)
        g = _lane_pick(g_ref[...], n_heads + head).reshape(n, c, 1)
        gc_row = jnp.sum(jnp.where(row <= col, jnp.broadcast_to(g, (n, c, c)), 0.0), axis=1, keepdims=True)
        gc_col = jnp.sum(jnp.where(eye, jnp.broadcast_to(gc_row, (n, c, c)), 0.0), axis=2, keepdims=True)
        decay = jnp.where(incl, jnp.exp(jnp.where(incl, gc_col - gc_row, 0.0)), 0.0)

        kb = k3.astype(BF16)
        qkk = _bdot(jnp.concatenate([q3, k3], axis=1).astype(BF16), kb, True)
        qk = qkk[:, :c] * decay
        lmat = jnp.where(strict, qkk[:, c:] * decay * beta, 0.0)
        tinv = jnp.where(eye, 1.0, 0.0) - jnp.where(lower_left(0), lmat, 0.0)
        for shift in range(1, int(math.log2(c))):
            cs = jnp.where(lower_left(shift), lmat, 0.0).astype(BF16)
            tb = tinv.astype(BF16)
            tinv = tinv - _bdot(_bdot(tb, cs, False).astype(BF16), tb, False)
        egc = jnp.exp(gc_col)
        rhs = jnp.concatenate([v3 * beta, k3 * (beta * egc)], axis=2).astype(BF16)
        uw = _bdot(tinv.astype(BF16), rhs, False)
        g_last = gc_col[:, c - 1:c, :]
        return dict(
            u=uw[:, :, :hd],
            wq=jnp.concatenate([uw[:, :, hd:], q3 * egc], axis=1).astype(BF16),
            qk=qk.astype(BF16),
            kt=(k3 * jnp.exp(g_last - gc_col)).astype(BF16),
            eg_last=jnp.exp(g_last))

    heads = [prepare(hh) for hh in range(hb)]
    states = [jnp.zeros((hd, hd), F32) for _ in range(hb)]
    outs = [[] for _ in range(hb)]
    for i in range(n):
        for hh, p in enumerate(heads):
            sb = states[hh].astype(BF16)
            ws = jnp.dot(p["wq"][i], sb, preferred_element_type=F32)
            vnb = (p["u"][i] - ws[:c]).astype(BF16)
            outs[hh].append(ws[c:] + jnp.dot(p["qk"][i], vnb, preferred_element_type=F32))
            states[hh] = states[hh] * p["eg_last"][i] + lax.dot_general(
                p["kt"][i], vnb, (((0,), (0,)), ((), ())), preferred_element_type=F32)
    for hh in range(hb):
        cols = slice(hh * hd, (hh + 1) * hd)
        s_ref[hh] = states[hh]
        o = jnp.concatenate(outs[hh], axis=0)
        o = o * lax.rsqrt(jnp.mean(o * o, axis=-1, keepdims=True) + EPS) * gn_ref[...]
        o_ref[:, cols] = (o * _silu(z_ref[:, cols])).astype(o_ref.dtype)


def _delta_prompt(proj, ba, w_conv, layer, alog_pad, dtb_pad, gate_norm, n_heads):
    b, l, _ = proj.shape
    hd = HEAD_DIM
    tw = n_heads * hd
    hb = DELTA_HEADS_PER_STEP
    assert n_heads % hb == 0
    steps = n_heads // hb
    col = lambda part: pl.BlockSpec((None, l, hb * hd), lambda i, h: (i, 0, part * steps + h))
    wcol = lambda part: pl.BlockSpec((None, CONV_W, hb * hd), lambda i, h: (layer, 0, part * steps + h))
    vec = pl.BlockSpec((1, LANES), lambda i, h: (0, 0))
    return pl.pallas_call(
        functools.partial(_delta_prompt_kernel, n_heads, hb),
        grid=(b, steps),
        in_specs=[col(0), col(1), col(2), col(3),
                  pl.BlockSpec((None, l, LANES), lambda i, h: (i, 0, 0)),
                  wcol(0), wcol(1), wcol(2), vec, vec, vec],
        out_specs=[pl.BlockSpec((None, l, hb * hd), lambda i, h: (i, 0, h)),
                   pl.BlockSpec((None, hb, hd, hd), lambda i, h: (i, h, 0, 0))],
        out_shape=[jax.ShapeDtypeStruct((b, l, tw), BF16),
                   jax.ShapeDtypeStruct((b, n_heads, hd, hd), F32)],
        scratch_shapes=[pltpu.VMEM((l + 8, hd), F32), pltpu.VMEM((l, LANES), F32), pltpu.VMEM((l, LANES), F32)],
        compiler_params=_params("parallel", "arbitrary"),
        name="delta_prompt",
    )(proj, proj, proj, proj, ba, w_conv, w_conv, w_conv, alog_pad, dtb_pad, gate_norm)


def _delta_sample_kernel(n_heads, x_ref, z_ref, ba_ref, hist_ref, wc_ref, s_ref, alog_ref, dtb_ref, gn_ref,
                         o_ref, hist_o_ref, s_o_ref):
    hd = HEAD_DIM
    tw = n_heads * hd
    x = x_ref[...]
    acc = hist_ref[0:1, :] * wc_ref[0:1, :]
    for j in range(1, CONV_W - 1):
        acc = acc + hist_ref[j:j + 1, :] * wc_ref[j:j + 1, :]
    acc = _silu(acc + x * wc_ref[CONV_W - 1:CONV_W, :])
    hist_o_ref[0:CONV_W - 2, :] = hist_ref[1:CONV_W - 1, :]
    hist_o_ref[CONV_W - 2:CONV_W - 1, :] = x

    beta_all, g_all = _gates(ba_ref[...], alog_ref[...], dtb_ref[...])
    row = lax.broadcasted_iota(jnp.int32, (hd, hd), 0)
    col = lax.broadcasted_iota(jnp.int32, (hd, hd), 1)
    eye = row == col

    def to_col(r):
        return jnp.sum(jnp.where(eye, jnp.broadcast_to(r, (hd, hd)), 0.0), axis=1, keepdims=True)

    def l2norm(r):
        return r * lax.rsqrt(jnp.sum(r * r, axis=-1, keepdims=True) + EPS)

    for h in range(n_heads):
        cols = slice(h * hd, (h + 1) * hd)
        q = l2norm(acc[:, cols]) * (hd ** -0.5)
        k = l2norm(acc[:, tw + h * hd: tw + (h + 1) * hd])
        v = acc[:, 2 * tw + h * hd: 2 * tw + (h + 1) * hd]
        beta = _lane_pick(beta_all, h)
        g = _lane_pick(g_all, n_heads + h)
        k_col = to_col(k)
        s = s_ref[h] * jnp.exp(g)
        vn = beta * (v - jnp.sum(s * k_col, axis=0, keepdims=True))
        s = s + k_col * vn
        s_o_ref[h] = s
        o = jnp.sum(s * to_col(q), axis=0, keepdims=True)
        o = o * lax.rsqrt(jnp.mean(o * o, axis=-1, keepdims=True) + EPS) * gn_ref[...]
        o_ref[:, cols] = (o * _silu(z_ref[:, cols])).astype(o_ref.dtype)


def _delta_sample(proj, ba, state_conv, w_conv, state_delta, layer, alog_pad, dtb_pad, gate_norm, n_heads):
    b = proj.shape[0]
    hd = HEAD_DIM
    tw = n_heads * hd
    hist = CONV_W - 1
    vec = pl.BlockSpec((1, LANES), lambda i: (0, 0))
    return pl.pallas_call(
        functools.partial(_delta_sample_kernel, n_heads),
        grid=(b,),
        in_specs=[pl.BlockSpec((None, 1, 3 * tw), lambda i: (i, 0, 0)),
                  pl.BlockSpec((None, 1, tw), lambda i: (i, 0, 3)),
                  pl.BlockSpec((None, 1, LANES), lambda i: (i, 0, 0)),
                  pl.BlockSpec((None, None, hist, 3 * tw), lambda i: (layer, i, 0, 0)),
                  pl.BlockSpec((None, CONV_W, 3 * tw), lambda i: (layer, 0, 0)),
                  pl.BlockSpec((None, None, n_heads, hd, hd), lambda i: (layer, i, 0, 0, 0)),
                  vec, vec, vec],
        out_specs=[pl.BlockSpec((None, 1, tw), lambda i: (i, 0, 0)),
                   pl.BlockSpec((None, hist, 3 * tw), lambda i: (i, 0, 0)),
                   pl.BlockSpec((None, n_heads, hd, hd), lambda i: (i, 0, 0, 0))],
        out_shape=[jax.ShapeDtypeStruct((b, 1, tw), BF16),
                   jax.ShapeDtypeStruct((b, hist, 3 * tw), F32),
                   jax.ShapeDtypeStruct((b, n_heads, hd, hd), F32)],
        compiler_params=_params("parallel"),
        name="delta_sample",
    )(proj, proj, ba, state_conv, w_conv, state_delta, alog_pad, dtb_pad, gate_norm)


def _trunk(x_p, x_s, mem_kv_p, cache_mk, cache_mv, conv_state, delta_state, buf_k, buf_v, wts):
    bp, lp, d = x_p.shape
    bs, ls, _ = x_s.shape
    assert ls == 1
    depth = wts["w_out"].shape[0]
    n_a = wts["w_in_a_t"].shape[0]
    d_ff = wts["w_down"].shape[1]
    tw = wts["w_conv"].shape[-1] // 3
    n_heads = tw // HEAD_DIM
    mem_w = d - tw
    kvw = N_KV_HEADS * HEAD_DIM
    slopes = _alibi_slopes(n_heads)
    off_b = 4 * tw
    off_qm = off_b + 2 * n_heads

    def lane_pad(vals):
        return jnp.zeros((1, LANES), F32).at[0, n_heads:2 * n_heads].set(vals.astype(F32))

    def resid(xp, fp, xs, fs, w_post, w_next):
        xp, np_ = _resid_norm(xp, fp, w_post, w_next)
        xs, ns_ = _resid_norm(xs, fs, w_post, w_next)
        return xp, xs, np_, ns_

    xp2 = x_p.reshape(bp * lp, d)
    xs2 = x_s.reshape(bs, d)
    (xn_p,) = _norm_cast(xp2, wts["norm_mix_pre"][0:1])
    (xn_s,) = _norm_cast(xs2, wts["norm_mix_pre"][0:1])
    conv_p, conv_s, delta_p, delta_s = [], [], [], []
    kv_p = win_k = win_v = None
    for layer in range(depth):
        if layer < n_a:
            wt = wts["w_in_a_t"]
            proj_p, proj_s = _matmul(xn_p, wt, tail=xn_s, layer=layer, n_cols=off_b, w_is_nk=True)
            w_ba = jnp.pad(wt[layer, off_b:off_qm, :], ((0, LANES - 2 * n_heads), (0, 0)))
            ba_p, ba_s = _matmul(xn_p, w_ba, tail=xn_s, w_is_nk=True)
            qm_p, qm_s = _matmul(xn_p, wt[layer, off_qm:, :], tail=xn_s, out_dtype=BF16, w_is_nk=True)
            alog_pad = lane_pad(wts["a_log"][layer])
            dtb_pad = lane_pad(wts["dt_bias"][layer])
            gate_norm = wts["w_gate_norm"][layer].reshape(1, HEAD_DIM)
            proj_p = proj_p.reshape(bp, lp, off_b)
            tok_p, s_p = _delta_prompt(proj_p, ba_p.reshape(bp, lp, LANES), wts["w_conv"], layer,
                                       alog_pad, dtb_pad, gate_norm, n_heads)
            conv_p.append(proj_p[:, lp - (CONV_W - 1):, :3 * tw])
            tok_s, hist_s, s_s = _delta_sample(proj_s.reshape(bs, 1, off_b), ba_s.reshape(bs, 1, LANES), conv_state,
                                               wts["w_conv"], delta_state, layer, alog_pad, dtb_pad, gate_norm,
                                               n_heads)
            conv_s.append(hist_s)
            delta_p.append(s_p)
            delta_s.append(s_s)
            mo_p = _mem_attn(qm_p.reshape(bp, lp, mem_w), 0, mem_w, mem_kv_p, mem_kv_p, layer, 0, 1)
            mo_s = _mem_attn(qm_s.reshape(bs, 1, mem_w), 0, mem_w, cache_mk, cache_mv, layer, 0, 0)
        else:
            lb = layer - n_a
            sinks = wts["sinks"][lb].astype(F32)
            proj_p, proj_s = _matmul(xn_p, wts["w_in_b"], tail=xn_s, layer=lb, out_dtype=BF16)
            proj_p = proj_p.reshape(bp, lp, d)
            tok_p = _swa_prompt(proj_p, kv_p, sinks, slopes)
            mo_p = _mem_attn(proj_p, tw // mem_w, mem_w, mem_kv_p, mem_kv_p, layer, 0, 1)
            q_s = proj_s[:, :tw].reshape(bs, n_heads, HEAD_DIM)
            tok_s = _swa_sample(q_s, win_k, win_v, sinks, slopes).reshape(bs, 1, tw)
            mo_s = _mem_attn(proj_s[:, tw:].reshape(bs, 1, mem_w), 0, mem_w, cache_mk, cache_mv, layer, 0, 0)
        mixed_p = jnp.concatenate([tok_p, mo_p], axis=-1).reshape(bp * lp, d)
        mixed_s = jnp.concatenate([tok_s, mo_s], axis=-1).reshape(bs, d)
        mix_p, mix_s = _matmul(mixed_p, wts["w_out"], tail=mixed_s, layer=layer)
        xp2, xs2, (hn_p,), (hn_s,) = resid(xp2, mix_p, xs2, mix_s, wts["norm_mix_post"][layer],
                                           wts["norm_ffn_pre"][layer:layer + 1])
        hid_p, hid_s = _swiglu_matmul(hn_p, hn_s, wts["w_gate_up"], layer, d_ff)
        f_p = _matmul_bf16w(hid_p, wts["w_down_bf16"], layer)
        f_s = _matmul_bf16w(hid_s, wts["w_down_bf16"], layer)
        w_post = wts["norm_ffn_post"][layer]
        if layer + 1 == depth:
            xp2, xs2, _, _ = resid(xp2, f_p, xs2, f_s, w_post, None)
        elif layer + 1 == n_a:
            w_next = jnp.stack([wts["norm_mix_pre"][layer + 1], wts["norm_kv"]])
            xp2, xs2, (xn_p, xkv_p), (xn_s, xkv_s) = resid(xp2, f_p, xs2, f_s, w_post, w_next)
            kv_p, kv_s = _matmul(xkv_p, wts["w_kv"], tail=xkv_s)
            kv_p = kv_p.reshape(bp, lp, 2 * kvw)
            wb = buf_k.shape[1]
            win_k = jnp.concatenate([buf_k.reshape(bs, wb, kvw), kv_s[:, None, :kvw]], axis=1)[:, 1:]
            win_v = jnp.concatenate([buf_v.reshape(bs, wb, kvw), kv_s[:, None, kvw:]], axis=1)[:, 1:]
        else:
            xp2, xs2, (xn_p,), (xn_s,) = resid(xp2, f_p, xs2, f_s, w_post,
                                               wts["norm_mix_pre"][layer + 1:layer + 2])
    wp = min(WINDOW, lp)
    heads = lambda t: t.reshape(t.shape[0], -1, N_KV_HEADS, HEAD_DIM)
    return (xp2.reshape(bp, lp, d), xs2.reshape(bs, 1, d),
            heads(kv_p[:, lp - wp:, :kvw]), heads(kv_p[:, lp - wp:, kvw:]), jnp.stack(conv_p), jnp.stack(delta_p),
            heads(win_k), heads(win_v), jnp.stack(conv_s), jnp.stack(delta_s))


def kernel(x_prompt, x_sample, cache_mem_k, cache_mem_v, cache_swa_k, cache_swa_v, state_conv, state_delta, mem_prompt, w_in_a, w_conv, a_log, dt_bias, w_gate_norm, w_in_b, sinks, norm_kv, w_kv, norm_mem, w_mem_kv, w_out, norm_mix_pre, norm_mix_post, norm_ffn_pre, norm_ffn_post, w_gate_up, w_down):
    wts = dict(w_in_a_t=jnp.swapaxes(w_in_a, 1, 2), w_conv=w_conv, a_log=a_log, dt_bias=dt_bias,
               w_gate_norm=w_gate_norm, w_in_b=w_in_b, sinks=sinks, norm_kv=norm_kv, w_kv=w_kv, w_out=w_out,
               norm_mix_pre=norm_mix_pre, norm_mix_post=norm_mix_post, norm_ffn_pre=norm_ffn_pre,
               norm_ffn_post=norm_ffn_post, w_gate_up=w_gate_up, w_down=w_down,
               w_down_bf16=_cast_bf16(w_down))
    depth = w_out.shape[0]
    bp, n_mem, d = mem_prompt.shape
    bs = x_sample.shape[0]
    mem_w = w_mem_kv.shape[-1] // 2
    mem_shape = (depth, bp, n_mem, N_MEM_HEADS, mem_w // N_MEM_HEADS)

    memn = _norm_cast(mem_prompt.reshape(bp * n_mem, d), norm_mem)
    mem_kv = jnp.stack([_matmul(memn[i], w_mem_kv, layer=i) for i in range(depth)])
    mem_kv = mem_kv.reshape(depth, bp, n_mem, 2 * mem_w)
    mem_k_p = mem_kv[..., :mem_w].reshape(mem_shape)
    mem_v_p = mem_kv[..., mem_w:].reshape(mem_shape)
    cmk = cache_mem_k.reshape(depth, bs, n_mem, mem_w)
    cmv = cache_mem_v.reshape(depth, bs, n_mem, mem_w)
    y_p, y_s, swk_p, swv_p, conv_p, delta_p, swk_s, swv_s, conv_s, delta_s = _trunk(
        x_prompt, x_sample, mem_kv, cmk, cmv, state_conv, state_delta, cache_swa_k, cache_swa_v, wts)
    return (y_p, y_s, mem_k_p, mem_v_p, swk_p, swv_p, conv_p, delta_p, swk_s, swv_s, conv_s, delta_s)
```

```python
import functools
import math

import numpy as np
import jax
import jax.numpy as jnp
from jax import lax
from jax.experimental import pallas as pl
from jax.experimental.pallas import tpu as pltpu

F32 = jnp.float32
BF16 = jnp.bfloat16
EPS = 1e-6

HEAD_DIM = 128
N_MEM_HEADS = 4
N_KV_HEADS = 8
WINDOW = 128
CONV_W = 4
DELTA_CHUNK = 128
DELTA_HEADS_PER_STEP = 2
LANES = 128
VMEM_LIMIT_BYTES = 56 * 1024 * 1024


def _params(*sem):
    return pltpu.CompilerParams(dimension_semantics=sem, vmem_limit_bytes=VMEM_LIMIT_BYTES)


def _alibi_slopes(n):
    def pow2_slopes(m):
        start = 2.0 ** (-8.0 / m)
        return [start ** (i + 1) for i in range(m)]
    c = 2 ** int(math.floor(math.log2(n)))
    s = pow2_slopes(c)
    if c < n:
        s = s + pow2_slopes(2 * c)[0::2][: n - c]
    return np.asarray(s, np.float32)


def _rms_rows(x, w):
    return x * lax.rsqrt(jnp.mean(x * x, axis=-1, keepdims=True) + EPS) * w


def _silu(x):
    return x * jax.nn.sigmoid(x)


def _softplus(x):
    return jnp.maximum(x, 0.0) + jnp.log1p(jnp.exp(-jnp.abs(x)))


def _row_block(m, target):
    b = min(m, target)
    assert m % b == 0, (m, b)
    return b


def _norm_kernel(x_ref, w_ref, *o_refs):
    x = x_ref[...]
    for j, o_ref in enumerate(o_refs):
        o_ref[...] = _rms_rows(x, w_ref[j:j + 1, :]).astype(o_ref.dtype)


def _norm_cast(x, ws):
    m, d = x.shape
    n = ws.shape[0]
    br = _row_block(m, 256)
    return pl.pallas_call(
        _norm_kernel,
        grid=(m // br,),
        in_specs=[pl.BlockSpec((br, d), lambda i: (i, 0)),
                  pl.BlockSpec((n, d), lambda i: (0, 0))],
        out_specs=[pl.BlockSpec((br, d), lambda i: (i, 0))] * n,
        out_shape=[jax.ShapeDtypeStruct((m, d), BF16)] * n,
        compiler_params=_params("parallel"),
        name="norm_cast",
    )(x, ws)


def _resid_kernel(n_next, x_ref, f_ref, wpost_ref, *refs):
    if n_next:
        wnext_ref, xo_ref = refs[0], refs[1]
        xn_refs = refs[2:]
    else:
        xo_ref = refs[0]
        xn_refs = ()
    xnew = x_ref[...] + _rms_rows(f_ref[...], wpost_ref[...])
    xo_ref[...] = xnew
    for j, o_ref in enumerate(xn_refs):
        o_ref[...] = _rms_rows(xnew, wnext_ref[j:j + 1, :]).astype(o_ref.dtype)


def _resid_norm(x, f, w_post, w_next):
    m, d = x.shape
    n_next = 0 if w_next is None else w_next.shape[0]
    br = _row_block(m, 256)
    row = pl.BlockSpec((br, d), lambda i: (i, 0))
    in_specs = [row, row, pl.BlockSpec((1, d), lambda i: (0, 0))]
    args = [x, f, w_post.reshape(1, d)]
    if n_next:
        in_specs.append(pl.BlockSpec((n_next, d), lambda i: (0, 0)))
        args.append(w_next)
    outs = pl.pallas_call(
        functools.partial(_resid_kernel, n_next),
        grid=(m // br,),
        in_specs=in_specs,
        out_specs=[row] * (1 + n_next),
        out_shape=[jax.ShapeDtypeStruct((m, d), F32)] + [jax.ShapeDtypeStruct((m, d), BF16)] * n_next,
        compiler_params=_params("parallel"),
        name="resid_norm",
    )(*args)
    return outs[0], list(outs[1:])


def _w_spec(w, layer, k, bn, col_block, w_is_nk=False):
    shape = (bn, k) if w_is_nk else (k, bn)
    pos = (lambda n: (col_block(n), 0)) if w_is_nk else (lambda n: (0, col_block(n)))
    if w.ndim == 3:
        return pl.BlockSpec((None,) + shape, lambda n, m: (layer,) + pos(n))
    return pl.BlockSpec(shape, lambda n, m: pos(n))


def _rows_with_tail(a_ref, tail_ref):
    return jnp.concatenate([a_ref[...], tail_ref[...]], axis=0)


def _mm_kernel(has_tail, w_is_nk, *refs):
    if has_tail:
        a_ref, tail_ref, w_ref, o_ref, otail_ref, wb_ref = refs
    else:
        a_ref, w_ref, o_ref, wb_ref = refs
    step = pl.program_id(1)

    @pl.when(step == 0)
    def _():
        wb_ref[...] = w_ref[...].astype(BF16)

    def mm(rows):
        if w_is_nk:
            return lax.dot_general(rows, wb_ref[...], (((1,), (1,)), ((), ())), preferred_element_type=F32)
        return jnp.dot(rows, wb_ref[...], preferred_element_type=F32)

    if not has_tail:
        o_ref[...] = mm(a_ref[...]).astype(o_ref.dtype)
        return
    last = pl.num_programs(1) - 1

    @pl.when(step < last)
    def _():
        o_ref[...] = mm(a_ref[...]).astype(o_ref.dtype)

    @pl.when(step == last)
    def _():
        bm = a_ref.shape[0]
        r = mm(_rows_with_tail(a_ref, tail_ref))
        o_ref[...] = r[:bm].astype(o_ref.dtype)
        otail_ref[...] = r[bm:].astype(otail_ref.dtype)


def _matmul(a, w, *, tail=None, layer=0, col_off=0, n_cols=None, out_dtype=F32, bn=512, bm=1024, w_is_nk=False):
    m, k = a.shape
    n_total = w.shape[-2] if w_is_nk else w.shape[-1]
    n_cols = n_total - col_off if n_cols is None else n_cols
    bn = min(bn, n_cols)
    assert n_cols % bn == 0 and col_off % bn == 0, (n_cols, col_off, bn)
    bm = _row_block(m, bm)
    off = col_off // bn
    in_specs = [pl.BlockSpec((bm, k), lambda n, i: (i, 0))]
    out_specs = [pl.BlockSpec((bm, bn), lambda n, i: (i, n))]
    out_shape = [jax.ShapeDtypeStruct((m, n_cols), out_dtype)]
    args = [a]
    if tail is not None:
        mt = tail.shape[0]
        in_specs.append(pl.BlockSpec((mt, k), lambda n, i: (0, 0)))
        out_specs.append(pl.BlockSpec((mt, bn), lambda n, i: (0, n)))
        out_shape.append(jax.ShapeDtypeStruct((mt, n_cols), out_dtype))
        args.append(tail)
    in_specs.append(_w_spec(w, layer, k, bn, lambda n: n + off, w_is_nk))
    outs = pl.pallas_call(
        functools.partial(_mm_kernel, tail is not None, w_is_nk),
        grid=(n_cols // bn, m // bm),
        in_specs=in_specs,
        out_specs=out_specs,
        out_shape=out_shape,
        scratch_shapes=[pltpu.VMEM((bn, k) if w_is_nk else (k, bn), BF16)],
        compiler_params=_params("arbitrary", "arbitrary"),
        name="matmul",
    )(*args, w)
    return outs[0] if tail is None else tuple(outs)


def _swiglu_kernel(a_ref, tail_ref, wg_ref, wu_ref, o_ref, otail_ref, wgb_ref, wub_ref):
    step = pl.program_id(1)
    last = pl.num_programs(1) - 1

    @pl.when(step == 0)
    def _():
        wgb_ref[...] = wg_ref[...].astype(BF16)
        wub_ref[...] = wu_ref[...].astype(BF16)

    def gated(rows):
        g = jnp.dot(rows, wgb_ref[...], preferred_element_type=F32)
        u = jnp.dot(rows, wub_ref[...], preferred_element_type=F32)
        return _silu(g) * u

    @pl.when(step < last)
    def _():
        o_ref[...] = gated(a_ref[...]).astype(o_ref.dtype)

    @pl.when(step == last)
    def _():
        bm = a_ref.shape[0]
        r = gated(_rows_with_tail(a_ref, tail_ref))
        o_ref[...] = r[:bm].astype(o_ref.dtype)
        otail_ref[...] = r[bm:].astype(otail_ref.dtype)


def _swiglu_matmul(a, tail, w, layer, d_ff, *, bn=256, bm=1024):
    m, k = a.shape
    mt = tail.shape[0]
    assert d_ff % bn == 0
    bm = _row_block(m, bm)
    nb = d_ff // bn
    return pl.pallas_call(
        _swiglu_kernel,
        grid=(nb, m // bm),
        in_specs=[pl.BlockSpec((bm, k), lambda n, i: (i, 0)),
                  pl.BlockSpec((mt, k), lambda n, i: (0, 0)),
                  _w_spec(w, layer, k, bn, lambda n: n),
                  _w_spec(w, layer, k, bn, lambda n: n + nb)],
        out_specs=[pl.BlockSpec((bm, bn), lambda n, i: (i, n)),
                   pl.BlockSpec((mt, bn), lambda n, i: (0, n))],
        out_shape=[jax.ShapeDtypeStruct((m, d_ff), BF16), jax.ShapeDtypeStruct((mt, d_ff), BF16)],
        scratch_shapes=[pltpu.VMEM((k, bn), BF16), pltpu.VMEM((k, bn), BF16)],
        compiler_params=_params("arbitrary", "arbitrary"),
        name="swiglu_matmul",
    )(a, tail, w, w)


def _cast_kernel(x_ref, o_ref):
    o_ref[...] = x_ref[...].astype(o_ref.dtype)


def _cast_bf16(w):
    lyr, k, n = w.shape
    budget = 12 * 1024 * 1024
    bk = max(r for r in range(16, k + 1, 16) if k % r == 0 and r * n * 4 <= budget)
    return pl.pallas_call(
        _cast_kernel,
        grid=(lyr, k // bk),
        in_specs=[pl.BlockSpec((None, bk, n), lambda i, j: (i, j, 0))],
        out_specs=pl.BlockSpec((None, bk, n), lambda i, j: (i, j, 0)),
        out_shape=jax.ShapeDtypeStruct(w.shape, BF16),
        compiler_params=_params("parallel", "parallel"),
        name="cast_bf16",
    )(w)


def _mm_bf16w_kernel(a_ref, w_ref, o_ref):
    o_ref[...] = jnp.dot(a_ref[...], w_ref[...], preferred_element_type=F32).astype(o_ref.dtype)


def _matmul_bf16w(a, w, layer, *, bn=512, bm=512):
    m, k = a.shape
    n = w.shape[-1]
    assert n % bn == 0
    bm = _row_block(m, bm)
    return pl.pallas_call(
        _mm_bf16w_kernel,
        grid=(n // bn, m // bm),
        in_specs=[pl.BlockSpec((bm, k), lambda j, i: (i, 0)),
                  pl.BlockSpec((None, k, bn), lambda j, i: (layer, 0, j))],
        out_specs=pl.BlockSpec((bm, bn), lambda j, i: (i, j)),
        out_shape=jax.ShapeDtypeStruct((m, n), F32),
        compiler_params=_params("parallel", "parallel"),
        name="matmul_bf16w",
    )(a, w)


def _mem_attn_kernel(head_dim, q_ref, k_ref, v_ref, *refs):
    o_ref = refs[-1]
    scale = head_dim ** -0.5
    for h in range(N_MEM_HEADS):
        cols = slice(h * head_dim, (h + 1) * head_dim)
        q = q_ref[:, cols]
        k = k_ref[:, cols].astype(BF16)
        v = v_ref[:, cols].astype(BF16)
        s = lax.dot_general(q, k, (((1,), (1,)), ((), ())), preferred_element_type=F32) * scale
        m = jnp.max(s, axis=-1, keepdims=True)
        p = jnp.exp(s - m)
        p = p * (1.0 / jnp.sum(p, axis=-1, keepdims=True))
        o_ref[:, cols] = jnp.dot(p.astype(BF16), v, preferred_element_type=F32).astype(o_ref.dtype)


def _mem_attn(q, q_col, width, mk, mv, layer, k_col, v_col, dest=None):
    b, l, _ = q.shape
    n_mem = mk.shape[2]
    head_dim = width // N_MEM_HEADS
    bl = _row_block(l, 512)
    in_specs = [pl.BlockSpec((None, bl, width), lambda i, j: (i, j, q_col)),
                pl.BlockSpec((None, None, n_mem, width), lambda i, j: (layer, i, 0, k_col)),
                pl.BlockSpec((None, None, n_mem, width), lambda i, j: (layer, i, 0, v_col))]
    args = [q, mk, mv]
    out_cols, out_col, aliases = width, 0, {}
    if dest is not None:
        out_cols = dest.shape[-1]
        out_col = out_cols // width - 1
        in_specs.append(pl.BlockSpec(memory_space=pl.ANY))
        args.append(dest)
        aliases = {3: 0}
    return pl.pallas_call(
        functools.partial(_mem_attn_kernel, head_dim),
        grid=(b, l // bl),
        in_specs=in_specs,
        out_specs=pl.BlockSpec((None, bl, width), lambda i, j: (i, j, out_col)),
        out_shape=jax.ShapeDtypeStruct((b, l, out_cols), BF16),
        input_output_aliases=aliases,
        compiler_params=_params("parallel", "parallel"),
        name="mem_attn",
    )(*args)


def _sink_softmax(s, sink):
    m = jnp.maximum(jnp.max(s, axis=-1, keepdims=True), sink)
    p = jnp.exp(s - m)
    return p * (1.0 / (jnp.sum(p, axis=-1, keepdims=True) + jnp.exp(sink - m)))


def _swa_prompt_kernel(slopes, sink_ref, q_ref, kp_ref, kc_ref, vp_ref, vc_ref, o_ref):
    w = WINDOW
    blk = pl.program_id(1)
    qi = lax.broadcasted_iota(jnp.int32, (w, 2 * w), 0)
    kj = lax.broadcasted_iota(jnp.int32, (w, 2 * w), 1)
    dist = w + qi - kj
    valid = (dist >= 0) & (dist < WINDOW) & ((blk - 1) * w + kj >= 0)
    distf = dist.astype(F32)
    scale = HEAD_DIM ** -0.5
    group = len(slopes) // N_KV_HEADS
    for hk in range(N_KV_HEADS):
        kcols = slice(hk * HEAD_DIM, (hk + 1) * HEAD_DIM)
        kcat = jnp.concatenate([kp_ref[:, kcols], kc_ref[:, kcols]], axis=0).astype(BF16)
        vcat = jnp.concatenate([vp_ref[:, kcols], vc_ref[:, kcols]], axis=0).astype(BF16)
        for g in range(group):
            h = hk * group + g
            cols = slice(h * HEAD_DIM, (h + 1) * HEAD_DIM)
            s = lax.dot_general(q_ref[:, cols], kcat, (((1,), (1,)), ((), ())),
                                preferred_element_type=F32) * scale
            s = s - float(slopes[h]) * distf
            s = jnp.where(valid, s, -jnp.inf)
            p = _sink_softmax(s, sink_ref[h])
            o_ref[:, cols] = jnp.dot(p.astype(BF16), vcat, preferred_element_type=F32).astype(o_ref.dtype)


def _swa_prompt(q, kv, sinks, slopes, out_cols):
    b, l, _ = q.shape
    w = WINDOW
    kvw = N_KV_HEADS * HEAD_DIM
    tw = len(slopes) * HEAD_DIM
    assert l % w == 0
    prev = lambda i, j: jnp.maximum(j - 1, 0)
    return pl.pallas_call(
        functools.partial(_swa_prompt_kernel, slopes),
        grid=(b, l // w),
        in_specs=[pl.BlockSpec(memory_space=pltpu.SMEM),
                  pl.BlockSpec((None, w, tw), lambda i, j: (i, j, 0)),
                  pl.BlockSpec((None, w, kvw), lambda i, j: (i, prev(i, j), 0)),
                  pl.BlockSpec((None, w, kvw), lambda i, j: (i, j, 0)),
                  pl.BlockSpec((None, w, kvw), lambda i, j: (i, prev(i, j), 1)),
                  pl.BlockSpec((None, w, kvw), lambda i, j: (i, j, 1))],
        out_specs=pl.BlockSpec((None, w, tw), lambda i, j: (i, j, 0)),
        out_shape=jax.ShapeDtypeStruct((b, l, out_cols), BF16),
        compiler_params=_params("parallel", "parallel"),
        name="swa_prompt",
    )(sinks, q, kv, kv, kv, kv)


def _swa_sample_kernel(slopes, sink_ref, q_ref, k_ref, v_ref, o_ref):
    n_heads = len(slopes)
    group = n_heads // N_KV_HEADS
    wb = k_ref.shape[0]
    q = q_ref[...].astype(BF16)
    head = lax.broadcasted_iota(jnp.int32, (n_heads, wb), 0)
    pos = lax.broadcasted_iota(jnp.int32, (n_heads, wb), 1)
    distf = (wb - 1 - pos).astype(F32)
    slope = jnp.zeros((n_heads, wb), F32)
    sink = jnp.zeros((n_heads, 1), F32)
    head_col = lax.broadcasted_iota(jnp.int32, (n_heads, 1), 0)
    for h in range(n_heads):
        slope = jnp.where(head == h, float(slopes[h]), slope)
        sink = jnp.where(head_col == h, sink_ref[h], sink)
    scale = HEAD_DIM ** -0.5
    s = jnp.zeros((n_heads, wb), F32)
    for hk in range(N_KV_HEADS):
        cols = slice(hk * HEAD_DIM, (hk + 1) * HEAD_DIM)
        s_hk = lax.dot_general(q, k_ref[:, cols].astype(BF16), (((1,), (1,)), ((), ())),
                               preferred_element_type=F32)
        s = jnp.where(head // group == hk, s_hk, s)
    s = s * scale - slope * distf
    p = _sink_softmax(s, sink).astype(BF16)
    o = jnp.zeros((n_heads, HEAD_DIM), F32)
    head_o = lax.broadcasted_iota(jnp.int32, (n_heads, HEAD_DIM), 0)
    for hk in range(N_KV_HEADS):
        cols = slice(hk * HEAD_DIM, (hk + 1) * HEAD_DIM)
        o_hk = jnp.dot(p, v_ref[:, cols].astype(BF16), preferred_element_type=F32)
        o = jnp.where(head_o // group == hk, o_hk, o)
    o_ref[...] = o.astype(o_ref.dtype)


def _swa_sample(q, win_k, win_v, sinks, slopes):
    b, n_heads, _ = q.shape
    wb, kvw = win_k.shape[1:]
    return pl.pallas_call(
        functools.partial(_swa_sample_kernel, slopes),
        grid=(b,),
        in_specs=[pl.BlockSpec(memory_space=pltpu.SMEM),
                  pl.BlockSpec((None, n_heads, HEAD_DIM), lambda i: (i, 0, 0)),
                  pl.BlockSpec((None, wb, kvw), lambda i: (i, 0, 0)),
                  pl.BlockSpec((None, wb, kvw), lambda i: (i, 0, 0))],
        out_specs=pl.BlockSpec((None, n_heads, HEAD_DIM), lambda i: (i, 0, 0)),
        out_shape=jax.ShapeDtypeStruct((b, n_heads, HEAD_DIM), BF16),
        compiler_params=_params("parallel"),
        name="swa_sample",
    )(sinks, q, win_k, win_v)


def _lane_pick(x, lane):
    idx = lax.broadcasted_iota(jnp.int32, x.shape, x.ndim - 1)
    return jnp.sum(jnp.where(idx == lane, x, 0.0), axis=-1, keepdims=True)


def _gates(ba, alog, dtb):
    beta = jax.nn.sigmoid(ba)
    g = -jnp.exp(alog) * _softplus(ba + dtb)
    return beta, g


def _bdot(a, b, contract_b_last, precision=None):
    dims = (((2,), (2 if contract_b_last else 1,)), ((0,), (0,)))
    return lax.dot_general(a, b, dims, preferred_element_type=F32, precision=precision)


def _delta_prompt_kernel(n_heads, hb, q_ref, k_ref, v_ref, z_ref, ba_ref, wq_ref, wk_ref, wv_ref,
                         alog_ref, dtb_ref, gn_ref, o_ref, s_ref, pad_ref, beta_ref, g_ref):
    l = q_ref.shape[0]
    c = DELTA_CHUNK
    n = l // c
    hd = HEAD_DIM
    hstep = pl.program_id(1)

    @pl.when(hstep == 0)
    def _():
        beta_all, g_all = _gates(ba_ref[...], alog_ref[...], dtb_ref[...])
        beta_ref[...] = beta_all
        g_ref[...] = g_all

    pad_ref[0:8, :] = jnp.zeros((8, hd), F32)

    def conv(x_ref, w_ref, cols):
        pad_ref[8:8 + l, :] = x_ref[:, cols]
        base = 8 - (CONV_W - 1)
        out = pad_ref[base:base + l, :] * w_ref[0:1, cols]
        for j in range(1, CONV_W):
            out = out + pad_ref[base + j:base + j + l, :] * w_ref[j:j + 1, cols]
        return _silu(out)

    def l2norm(x):
        return x * lax.rsqrt(jnp.sum(x * x, axis=-1, keepdims=True) + EPS)

    row = lax.broadcasted_iota(jnp.int32, (1, c, c), 1)
    col = lax.broadcasted_iota(jnp.int32, (1, c, c), 2)
    incl = row >= col
    strict = row > col
    eye = row == col

    def lower_left(shift):
        return (((row >> (shift + 1)) == (col >> (shift + 1)))
                & (((row >> shift) & 1) == 1) & (((col >> shift) & 1) == 0))

    def prepare(hh):
        cols = slice(hh * hd, (hh + 1) * hd)
        head = hstep * hb + hh
        q3 = (l2norm(conv(q_ref, wq_ref, cols)) * (hd ** -0.5)).reshape(n, c, hd)
        k3 = l2norm(conv(k_ref, wk_ref, cols)).reshape(n, c, hd)
        v3 = conv(v_ref, wv_ref, cols).reshape(n, c, hd)
        beta = _lane_pick(beta_ref[...], head).reshape(n, c, 1)
        g = _lane_pick(g_ref[...], n_heads + head).reshape(n, c, 1)
        gc_row = jnp.sum(jnp.where(row <= col, jnp.broadcast_to(g, (n, c, c)), 0.0), axis=1, keepdims=True)
        gc_col = jnp.sum(jnp.where(eye, jnp.broadcast_to(gc_row, (n, c, c)), 0.0), axis=2, keepdims=True)
        decay = jnp.where(incl, jnp.exp(jnp.where(incl, gc_col - gc_row, 0.0)), 0.0)

        kb = k3.astype(BF16)
        qkk = _bdot(jnp.concatenate([q3, k3], axis=1).astype(BF16), kb, True)
        qk = qkk[:, :c] * decay
        lmat = jnp.where(strict, qkk[:, c:] * decay * beta, 0.0)
        tinv = jnp.where(eye, 1.0, 0.0) - jnp.where(lower_left(0), lmat, 0.0)
        for shift in range(1, int(math.log2(c))):
            cs = jnp.where(lower_left(shift), lmat, 0.0).astype(BF16)
            tb = tinv.astype(BF16)
            tinv = tinv - _bdot(_bdot(tb, cs, False).astype(BF16), tb, False)
        egc = jnp.exp(gc_col)
        rhs = jnp.concatenate([v3 * beta, k3 * (beta * egc)], axis=2).astype(BF16)
        uw = _bdot(tinv.astype(BF16), rhs, False)
        g_last = gc_col[:, c - 1:c, :]
        return dict(
            u=uw[:, :, :hd],
            wq=jnp.concatenate([uw[:, :, hd:], q3 * egc], axis=1).astype(BF16),
            qk=qk.astype(BF16),
            kt=(k3 * jnp.exp(g_last - gc_col)).astype(BF16),
            eg_last=jnp.exp(g_last))

    heads = [prepare(hh) for hh in range(hb)]
    states = [jnp.zeros((hd, hd), F32) for _ in range(hb)]
    for i in range(n):
        for hh, p in enumerate(heads):
            cols = slice(hh * hd, (hh + 1) * hd)
            tok = slice(i * c, (i + 1) * c)
            sb = states[hh].astype(BF16)
            ws = jnp.dot(p["wq"][i], sb, preferred_element_type=F32)
            vnb = (p["u"][i] - ws[:c]).astype(BF16)
            o = ws[c:] + jnp.dot(p["qk"][i], vnb, preferred_element_type=F32)
            states[hh] = states[hh] * p["eg_last"][i] + lax.dot_general(
                p["kt"][i], vnb, (((0,), (0,)), ((), ())), preferred_element_type=F32)
            o = o * lax.rsqrt(jnp.mean(o * o, axis=-1, keepdims=True) + EPS) * gn_ref[...]
            o_ref[tok, cols] = (o * _silu(z_ref[tok, cols])).astype(o_ref.dtype)
    for hh in range(hb):
        s_ref[hh] = states[hh]


def _delta_prompt(proj, ba, w_conv, layer, alog_pad, dtb_pad, gate_norm, n_heads, out_cols):
    b, l, _ = proj.shape
    hd = HEAD_DIM
    tw = n_heads * hd
    hb = DELTA_HEADS_PER_STEP
    assert n_heads % hb == 0
    steps = n_heads // hb
    col = lambda part: pl.BlockSpec((None, l, hb * hd), lambda i, h: (i, 0, part * steps + h))
    wcol = lambda part: pl.BlockSpec((None, CONV_W, hb * hd), lambda i, h: (layer, 0, part * steps + h))
    vec = pl.BlockSpec((1, LANES), lambda i, h: (0, 0))
    return pl.pallas_call(
        functools.partial(_delta_prompt_kernel, n_heads, hb),
        grid=(b, steps),
        in_specs=[col(0), col(1), col(2), col(3),
                  pl.BlockSpec((None, l, LANES), lambda i, h: (i, 0, 0)),
                  wcol(0), wcol(1), wcol(2), vec, vec, vec],
        out_specs=[pl.BlockSpec((None, l, hb * hd), lambda i, h: (i, 0, h)),
                   pl.BlockSpec((None, hb, hd, hd), lambda i, h: (i, h, 0, 0))],
        out_shape=[jax.ShapeDtypeStruct((b, l, out_cols), BF16),
                   jax.ShapeDtypeStruct((b, n_heads, hd, hd), F32)],
        scratch_shapes=[pltpu.VMEM((l + 8, hd), F32), pltpu.VMEM((l, LANES), F32), pltpu.VMEM((l, LANES), F32)],
        compiler_params=_params("parallel", "arbitrary"),
        name="delta_prompt",
    )(proj, proj, proj, proj, ba, w_conv, w_conv, w_conv, alog_pad, dtb_pad, gate_norm)


def _delta_sample_kernel(n_heads, x_ref, z_ref, ba_ref, hist_ref, wc_ref, s_ref, alog_ref, dtb_ref, gn_ref,
                         o_ref, hist_o_ref, s_o_ref):
    hd = HEAD_DIM
    tw = n_heads * hd
    x = x_ref[...]
    acc = hist_ref[0:1, :] * wc_ref[0:1, :]
    for j in range(1, CONV_W - 1):
        acc = acc + hist_ref[j:j + 1, :] * wc_ref[j:j + 1, :]
    acc = _silu(acc + x * wc_ref[CONV_W - 1:CONV_W, :])
    hist_o_ref[0:CONV_W - 2, :] = hist_ref[1:CONV_W - 1, :]
    hist_o_ref[CONV_W - 2:CONV_W - 1, :] = x

    beta_all, g_all = _gates(ba_ref[...], alog_ref[...], dtb_ref[...])
    row = lax.broadcasted_iota(jnp.int32, (hd, hd), 0)
    col = lax.broadcasted_iota(jnp.int32, (hd, hd), 1)
    eye = row == col

    def to_col(r):
        return jnp.sum(jnp.where(eye, jnp.broadcast_to(r, (hd, hd)), 0.0), axis=1, keepdims=True)

    def l2norm(r):
        return r * lax.rsqrt(jnp.sum(r * r, axis=-1, keepdims=True) + EPS)

    for h in range(n_heads):
        cols = slice(h * hd, (h + 1) * hd)
        q = l2norm(acc[:, cols]) * (hd ** -0.5)
        k = l2norm(acc[:, tw + h * hd: tw + (h + 1) * hd])
        v = acc[:, 2 * tw + h * hd: 2 * tw + (h + 1) * hd]
        beta = _lane_pick(beta_all, h)
        g = _lane_pick(g_all, n_heads + h)
        k_col = to_col(k)
        s = s_ref[h] * jnp.exp(g)
        vn = beta * (v - jnp.sum(s * k_col, axis=0, keepdims=True))
        s = s + k_col * vn
        s_o_ref[h] = s
        o = jnp.sum(s * to_col(q), axis=0, keepdims=True)
        o = o * lax.rsqrt(jnp.mean(o * o, axis=-1, keepdims=True) + EPS) * gn_ref[...]
        o_ref[:, cols] = (o * _silu(z_ref[:, cols])).astype(o_ref.dtype)


def _delta_sample(proj, ba, state_conv, w_conv, state_delta, layer, alog_pad, dtb_pad, gate_norm, n_heads):
    b = proj.shape[0]
    hd = HEAD_DIM
    tw = n_heads * hd
    hist = CONV_W - 1
    vec = pl.BlockSpec((1, LANES), lambda i: (0, 0))
    return pl.pallas_call(
        functools.partial(_delta_sample_kernel, n_heads),
        grid=(b,),
        in_specs=[pl.BlockSpec((None, 1, 3 * tw), lambda i: (i, 0, 0)),
                  pl.BlockSpec((None, 1, tw), lambda i: (i, 0, 3)),
                  pl.BlockSpec((None, 1, LANES), lambda i: (i, 0, 0)),
                  pl.BlockSpec((None, None, hist, 3 * tw), lambda i: (layer, i, 0, 0)),
                  pl.BlockSpec((None, CONV_W, 3 * tw), lambda i: (layer, 0, 0)),
                  pl.BlockSpec((None, None, n_heads, hd, hd), lambda i: (layer, i, 0, 0, 0)),
                  vec, vec, vec],
        out_specs=[pl.BlockSpec((None, 1, tw), lambda i: (i, 0, 0)),
                   pl.BlockSpec((None, hist, 3 * tw), lambda i: (i, 0, 0)),
                   pl.BlockSpec((None, n_heads, hd, hd), lambda i: (i, 0, 0, 0))],
        out_shape=[jax.ShapeDtypeStruct((b, 1, tw), BF16),
                   jax.ShapeDtypeStruct((b, hist, 3 * tw), F32),
                   jax.ShapeDtypeStruct((b, n_heads, hd, hd), F32)],
        compiler_params=_params("parallel"),
        name="delta_sample",
    )(proj, proj, ba, state_conv, w_conv, state_delta, alog_pad, dtb_pad, gate_norm)


def _trunk(x_p, x_s, mem_kv_p, cache_mk, cache_mv, conv_state, delta_state, buf_k, buf_v, wts):
    bp, lp, d = x_p.shape
    bs, ls, _ = x_s.shape
    assert ls == 1
    depth = wts["w_out"].shape[0]
    n_a = wts["w_in_a_t"].shape[0]
    d_ff = wts["w_down"].shape[1]
    tw = wts["w_conv"].shape[-1] // 3
    n_heads = tw // HEAD_DIM
    mem_w = d - tw
    kvw = N_KV_HEADS * HEAD_DIM
    slopes = _alibi_slopes(n_heads)
    off_b = 4 * tw
    off_qm = off_b + 2 * n_heads

    def lane_pad(vals):
        return jnp.zeros((1, LANES), F32).at[0, n_heads:2 * n_heads].set(vals.astype(F32))

    def resid(xp, fp, xs, fs, w_post, w_next):
        xp, np_ = _resid_norm(xp, fp, w_post, w_next)
        xs, ns_ = _resid_norm(xs, fs, w_post, w_next)
        return xp, xs, np_, ns_

    xp2 = x_p.reshape(bp * lp, d)
    xs2 = x_s.reshape(bs, d)
    (xn_p,) = _norm_cast(xp2, wts["norm_mix_pre"][0:1])
    (xn_s,) = _norm_cast(xs2, wts["norm_mix_pre"][0:1])
    conv_p, conv_s, delta_p, delta_s = [], [], [], []
    kv_p = win_k = win_v = None
    for layer in range(depth):
        if layer < n_a:
            wt = wts["w_in_a_t"]
            proj_p, proj_s = _matmul(xn_p, wt, tail=xn_s, layer=layer, n_cols=off_b, w_is_nk=True)
            w_ba = jnp.pad(wt[layer, off_b:off_qm, :], ((0, LANES - 2 * n_heads), (0, 0)))
            ba_p, ba_s = _matmul(xn_p, w_ba, tail=xn_s, w_is_nk=True)
            qm_p, qm_s = _matmul(xn_p, wt[layer, off_qm:, :], tail=xn_s, out_dtype=BF16, w_is_nk=True)
            alog_pad = lane_pad(wts["a_log"][layer])
            dtb_pad = lane_pad(wts["dt_bias"][layer])
            gate_norm = wts["w_gate_norm"][layer].reshape(1, HEAD_DIM)
            proj_p = proj_p.reshape(bp, lp, off_b)
            tok_p, s_p = _delta_prompt(proj_p, ba_p.reshape(bp, lp, LANES), wts["w_conv"], layer,
                                       alog_pad, dtb_pad, gate_norm, n_heads, d)
            conv_p.append(proj_p[:, lp - (CONV_W - 1):, :3 * tw])
            tok_s, hist_s, s_s = _delta_sample(proj_s.reshape(bs, 1, off_b), ba_s.reshape(bs, 1, LANES), conv_state,
                                               wts["w_conv"], delta_state, layer, alog_pad, dtb_pad, gate_norm,
                                               n_heads)
            conv_s.append(hist_s)
            delta_p.append(s_p)
            delta_s.append(s_s)
            mixed_p = _mem_attn(qm_p.reshape(bp, lp, mem_w), 0, mem_w, mem_kv_p, mem_kv_p, layer, 0, 1, dest=tok_p)
            mo_s = _mem_attn(qm_s.reshape(bs, 1, mem_w), 0, mem_w, cache_mk, cache_mv, layer, 0, 0)
        else:
            lb = layer - n_a
            sinks = wts["sinks"][lb].astype(F32)
            proj_p, proj_s = _matmul(xn_p, wts["w_in_b"], tail=xn_s, layer=lb, out_dtype=BF16)
            proj_p = proj_p.reshape(bp, lp, d)
            tok_p = _swa_prompt(proj_p, kv_p, sinks, slopes, d)
            mixed_p = _mem_attn(proj_p, tw // mem_w, mem_w, mem_kv_p, mem_kv_p, layer, 0, 1, dest=tok_p)
            q_s = proj_s[:, :tw].reshape(bs, n_heads, HEAD_DIM)
            tok_s = _swa_sample(q_s, win_k, win_v, sinks, slopes).reshape(bs, 1, tw)
            mo_s = _mem_attn(proj_s[:, tw:].reshape(bs, 1, mem_w), 0, mem_w, cache_mk, cache_mv, layer, 0, 0)
        mixed_p = mixed_p.reshape(bp * lp, d)
        mixed_s = jnp.concatenate([tok_s, mo_s], axis=-1).reshape(bs, d)
        mix_p, mix_s = _matmul(mixed_p, wts["w_out"], tail=mixed_s, layer=layer)
        xp2, xs2, (hn_p,), (hn_s,) = resid(xp2, mix_p, xs2, mix_s, wts["norm_mix_post"][layer],
                                           wts["norm_ffn_pre"][layer:layer + 1])
        hid_p, hid_s = _swiglu_matmul(hn_p, hn_s, wts["w_gate_up"], layer, d_ff)
        f_p = _matmul_bf16w(hid_p, wts["w_down_bf16"], layer)
        f_s = _matmul_bf16w(hid_s, wts["w_down_bf16"], layer)
        w_post = wts["norm_ffn_post"][layer]
        if layer + 1 == depth:
            xp2, xs2, _, _ = resid(xp2, f_p, xs2, f_s, w_post, None)
        elif layer + 1 == n_a:
            w_next = jnp.stack([wts["norm_mix_pre"][layer + 1], wts["norm_kv"]])
            xp2, xs2, (xn_p, xkv_p), (xn_s, xkv_s) = resid(xp2, f_p, xs2, f_s, w_post, w_next)
            kv_p, kv_s = _matmul(xkv_p, wts["w_kv"], tail=xkv_s)
            kv_p = kv_p.reshape(bp, lp, 2 * kvw)
            wb = buf_k.shape[1]
            win_k = jnp.concatenate([buf_k.reshape(bs, wb, kvw), kv_s[:, None, :kvw]], axis=1)[:, 1:]
            win_v = jnp.concatenate([buf_v.reshape(bs, wb, kvw), kv_s[:, None, kvw:]], axis=1)[:, 1:]
        else:
            xp2, xs2, (xn_p,), (xn_s,) = resid(xp2, f_p, xs2, f_s, w_post,
                                               wts["norm_mix_pre"][layer + 1:layer + 2])
    wp = min(WINDOW, lp)
    heads = lambda t: t.reshape(t.shape[0], -1, N_KV_HEADS, HEAD_DIM)
    return (xp2.reshape(bp, lp, d), xs2.reshape(bs, 1, d),
            heads(kv_p[:, lp - wp:, :kvw]), heads(kv_p[:, lp - wp:, kvw:]), jnp.stack(conv_p), jnp.stack(delta_p),
            heads(win_k), heads(win_v), jnp.stack(conv_s), jnp.stack(delta_s))


def kernel(x_prompt, x_sample, cache_mem_k, cache_mem_v, cache_swa_k, cache_swa_v, state_conv, state_delta, mem_prompt, w_in_a, w_conv, a_log, dt_bias, w_gate_norm, w_in_b, sinks, norm_kv, w_kv, norm_mem, w_mem_kv, w_out, norm_mix_pre, norm_mix_post, norm_ffn_pre, norm_ffn_post, w_gate_up, w_down):
    wts = dict(w_in_a_t=jnp.swapaxes(w_in_a, 1, 2), w_conv=w_conv, a_log=a_log, dt_bias=dt_bias,
               w_gate_norm=w_gate_norm, w_in_b=w_in_b, sinks=sinks, norm_kv=norm_kv, w_kv=w_kv, w_out=w_out,
               norm_mix_pre=norm_mix_pre, norm_mix_post=norm_mix_post, norm_ffn_pre=norm_ffn_pre,
               norm_ffn_post=norm_ffn_post, w_gate_up=w_gate_up, w_down=w_down,
               w_down_bf16=_cast_bf16(w_down))
    depth = w_out.shape[0]
    bp, n_mem, d = mem_prompt.shape
    bs = x_sample.shape[0]
    mem_w = w_mem_kv.shape[-1] // 2
    mem_shape = (depth, bp, n_mem, N_MEM_HEADS, mem_w // N_MEM_HEADS)

    memn = _norm_cast(mem_prompt.reshape(bp * n_mem, d), norm_mem)
    mem_kv = jnp.stack([_matmul(memn[i], w_mem_kv, layer=i) for i in range(depth)])
    mem_kv = mem_kv.reshape(depth, bp, n_mem, 2 * mem_w)
    mem_k_p = mem_kv[..., :mem_w].reshape(mem_shape)
    mem_v_p = mem_kv[..., mem_w:].reshape(mem_shape)
    cmk = cache_mem_k.reshape(depth, bs, n_mem, mem_w)
    cmv = cache_mem_v.reshape(depth, bs, n_mem, mem_w)
    y_p, y_s, swk_p, swv_p, conv_p, delta_p, swk_s, swv_s, conv_s, delta_s = _trunk(
        x_prompt, x_sample, mem_kv, cmk, cmv, state_conv, state_delta, cache_swa_k, cache_swa_v, wts)
    return (y_p, y_s, mem_k_p, mem_v_p, swk_p, swv_p, conv_p, delta_p, swk_s, swv_s, conv_s, delta_s)
```

```python
import functools
import math

import numpy as np
import jax
import jax.numpy as jnp
from jax import lax
from jax.experimental import pallas as pl
from jax.experimental.pallas import tpu as pltpu

F32 = jnp.float32
BF16 = jnp.bfloat16
EPS = 1e-6

HEAD_DIM = 128
N_MEM_HEADS = 4
N_KV_HEADS = 8
WINDOW = 128
CONV_W = 4
DELTA_CHUNK = 128
DELTA_HEADS_PER_STEP = 2
LANES = 128
VMEM_LIMIT_BYTES = 56 * 1024 * 1024


def _params(*sem):
    return pltpu.CompilerParams(dimension_semantics=sem, vmem_limit_bytes=VMEM_LIMIT_BYTES)


def _alibi_slopes(n):
    def pow2_slopes(m):
        start = 2.0 ** (-8.0 / m)
        return [start ** (i + 1) for i in range(m)]
    c = 2 ** int(math.floor(math.log2(n)))
    s = pow2_slopes(c)
    if c < n:
        s = s + pow2_slopes(2 * c)[0::2][: n - c]
    return np.asarray(s, np.float32)


def _rms_rows(x, w):
    return x * lax.rsqrt(jnp.mean(x * x, axis=-1, keepdims=True) + EPS) * w


def _silu(x):
    return x * jax.nn.sigmoid(x)


def _softplus(x):
    return jnp.maximum(x, 0.0) + jnp.log1p(jnp.exp(-jnp.abs(x)))


def _row_block(m, target):
    b = min(m, target)
    assert m % b == 0, (m, b)
    return b


def _norm_kernel(x_ref, w_ref, *o_refs):
    x = x_ref[...]
    for j, o_ref in enumerate(o_refs):
        o_ref[...] = _rms_rows(x, w_ref[j:j + 1, :]).astype(o_ref.dtype)


def _norm_cast(x, ws):
    m, d = x.shape
    n = ws.shape[0]
    br = _row_block(m, 256)
    return pl.pallas_call(
        _norm_kernel,
        grid=(m // br,),
        in_specs=[pl.BlockSpec((br, d), lambda i: (i, 0)),
                  pl.BlockSpec((n, d), lambda i: (0, 0))],
        out_specs=[pl.BlockSpec((br, d), lambda i: (i, 0))] * n,
        out_shape=[jax.ShapeDtypeStruct((m, d), BF16)] * n,
        compiler_params=_params("parallel"),
        name="norm_cast",
    )(x, ws)


def _resid_kernel(n_next, x_ref, f_ref, wpost_ref, *refs):
    if n_next:
        wnext_ref, xo_ref = refs[0], refs[1]
        xn_refs = refs[2:]
    else:
        xo_ref = refs[0]
        xn_refs = ()
    xnew = x_ref[...] + _rms_rows(f_ref[...], wpost_ref[...])
    xo_ref[...] = xnew
    for j, o_ref in enumerate(xn_refs):
        o_ref[...] = _rms_rows(xnew, wnext_ref[j:j + 1, :]).astype(o_ref.dtype)


def _resid_norm(x, f, w_post, w_next):
    m, d = x.shape
    n_next = 0 if w_next is None else w_next.shape[0]
    br = _row_block(m, 256)
    row = pl.BlockSpec((br, d), lambda i: (i, 0))
    in_specs = [row, row, pl.BlockSpec((1, d), lambda i: (0, 0))]
    args = [x, f, w_post.reshape(1, d)]
    if n_next:
        in_specs.append(pl.BlockSpec((n_next, d), lambda i: (0, 0)))
        args.append(w_next)
    outs = pl.pallas_call(
        functools.partial(_resid_kernel, n_next),
        grid=(m // br,),
        in_specs=in_specs,
        out_specs=[row] * (1 + n_next),
        out_shape=[jax.ShapeDtypeStruct((m, d), F32)] + [jax.ShapeDtypeStruct((m, d), BF16)] * n_next,
        compiler_params=_params("parallel"),
        name="resid_norm",
    )(*args)
    return outs[0], list(outs[1:])


def _w_spec(w, layer, k, bn, col_block, w_is_nk=False):
    shape = (bn, k) if w_is_nk else (k, bn)
    pos = (lambda n: (col_block(n), 0)) if w_is_nk else (lambda n: (0, col_block(n)))
    if w.ndim == 3:
        return pl.BlockSpec((None,) + shape, lambda n, m: (layer,) + pos(n))
    return pl.BlockSpec(shape, lambda n, m: pos(n))


def _rows_with_tail(a_ref, tail_ref):
    return jnp.concatenate([a_ref[...], tail_ref[...]], axis=0)


def _mm_kernel(has_tail, w_is_nk, *refs):
    if has_tail:
        a_ref, tail_ref, w_ref, o_ref, otail_ref, wb_ref = refs
    else:
        a_ref, w_ref, o_ref, wb_ref = refs
    step = pl.program_id(1)

    @pl.when(step == 0)
    def _():
        wb_ref[...] = w_ref[...].astype(BF16)

    def mm(rows):
        if w_is_nk:
            return lax.dot_general(rows, wb_ref[...], (((1,), (1,)), ((), ())), preferred_element_type=F32)
        return jnp.dot(rows, wb_ref[...], preferred_element_type=F32)

    if not has_tail:
        o_ref[...] = mm(a_ref[...]).astype(o_ref.dtype)
        return
    last = pl.num_programs(1) - 1

    @pl.when(step < last)
    def _():
        o_ref[...] = mm(a_ref[...]).astype(o_ref.dtype)

    @pl.when(step == last)
    def _():
        bm = a_ref.shape[0]
        r = mm(_rows_with_tail(a_ref, tail_ref))
        o_ref[...] = r[:bm].astype(o_ref.dtype)
        otail_ref[...] = r[bm:].astype(otail_ref.dtype)


def _matmul(a, w, *, tail=None, layer=0, col_off=0, n_cols=None, out_dtype=F32, bn=512, bm=1024, w_is_nk=False):
    m, k = a.shape
    n_total = w.shape[-2] if w_is_nk else w.shape[-1]
    n_cols = n_total - col_off if n_cols is None else n_cols
    bn = min(bn, n_cols)
    assert n_cols % bn == 0 and col_off % bn == 0, (n_cols, col_off, bn)
    bm = _row_block(m, bm)
    off = col_off // bn
    in_specs = [pl.BlockSpec((bm, k), lambda n, i: (i, 0))]
    out_specs = [pl.BlockSpec((bm, bn), lambda n, i: (i, n))]
    out_shape = [jax.ShapeDtypeStruct((m, n_cols), out_dtype)]
    args = [a]
    if tail is not None:
        mt = tail.shape[0]
        in_specs.append(pl.BlockSpec((mt, k), lambda n, i: (0, 0)))
        out_specs.append(pl.BlockSpec((mt, bn), lambda n, i: (0, n)))
        out_shape.append(jax.ShapeDtypeStruct((mt, n_cols), out_dtype))
        args.append(tail)
    in_specs.append(_w_spec(w, layer, k, bn, lambda n: n + off, w_is_nk))
    outs = pl.pallas_call(
        functools.partial(_mm_kernel, tail is not None, w_is_nk),
        grid=(n_cols // bn, m // bm),
        in_specs=in_specs,
        out_specs=out_specs,
        out_shape=out_shape,
        scratch_shapes=[pltpu.VMEM((bn, k) if w_is_nk else (k, bn), BF16)],
        compiler_params=_params("arbitrary", "arbitrary"),
        name="matmul",
    )(*args, w)
    return outs[0] if tail is None else tuple(outs)


def _swiglu_kernel(a_ref, tail_ref, wg_ref, wu_ref, wd_ref, o_ref, otail_ref, wdb_ref, wgb_ref, wub_ref):
    step = pl.program_id(1)
    last = pl.num_programs(1) - 1
    wdb_ref[...] = wd_ref[...].astype(BF16)

    @pl.when(step == 0)
    def _():
        wgb_ref[...] = wg_ref[...].astype(BF16)
        wub_ref[...] = wu_ref[...].astype(BF16)

    def gated(rows):
        g = jnp.dot(rows, wgb_ref[...], preferred_element_type=F32)
        u = jnp.dot(rows, wub_ref[...], preferred_element_type=F32)
        return _silu(g) * u

    @pl.when(step < last)
    def _():
        o_ref[...] = gated(a_ref[...]).astype(o_ref.dtype)

    @pl.when(step == last)
    def _():
        bm = a_ref.shape[0]
        r = gated(_rows_with_tail(a_ref, tail_ref))
        o_ref[...] = r[:bm].astype(o_ref.dtype)
        otail_ref[...] = r[bm:].astype(otail_ref.dtype)


def _swiglu_matmul(a, tail, w, w_down, layer, *, bn=256, bm=1024):
    m, k = a.shape
    mt = tail.shape[0]
    d_ff, n_down = w_down.shape[1:]
    assert d_ff % bn == 0
    bm = _row_block(m, bm)
    nb = d_ff // bn
    m_steps = m // bm
    slab = d_ff // (nb * m_steps)
    assert slab * nb * m_steps == d_ff and slab % 16 == 0, (d_ff, nb, m_steps)
    return pl.pallas_call(
        _swiglu_kernel,
        grid=(nb, m_steps),
        in_specs=[pl.BlockSpec((bm, k), lambda n, i: (i, 0)),
                  pl.BlockSpec((mt, k), lambda n, i: (0, 0)),
                  _w_spec(w, layer, k, bn, lambda n: n),
                  _w_spec(w, layer, k, bn, lambda n: n + nb),
                  pl.BlockSpec((None, slab, n_down), lambda n, i: (layer, n * m_steps + i, 0))],
        out_specs=[pl.BlockSpec((bm, bn), lambda n, i: (i, n)),
                   pl.BlockSpec((mt, bn), lambda n, i: (0, n)),
                   pl.BlockSpec((slab, n_down), lambda n, i: (n * m_steps + i, 0))],
        out_shape=[jax.ShapeDtypeStruct((m, d_ff), BF16), jax.ShapeDtypeStruct((mt, d_ff), BF16),
                   jax.ShapeDtypeStruct((d_ff, n_down), BF16)],
        scratch_shapes=[pltpu.VMEM((k, bn), BF16), pltpu.VMEM((k, bn), BF16)],
        compiler_params=_params("arbitrary", "arbitrary"),
        name="swiglu_matmul",
    )(a, tail, w, w, w_down)


def _mm_bf16w_kernel(a_ref, w_ref, o_ref):
    o_ref[...] = jnp.dot(a_ref[...], w_ref[...], preferred_element_type=F32).astype(o_ref.dtype)


def _matmul_bf16w(a, w, *, bn=512, bm=512):
    m, k = a.shape
    n = w.shape[-1]
    assert n % bn == 0
    bm = _row_block(m, bm)
    return pl.pallas_call(
        _mm_bf16w_kernel,
        grid=(n // bn, m // bm),
        in_specs=[pl.BlockSpec((bm, k), lambda j, i: (i, 0)),
                  pl.BlockSpec((k, bn), lambda j, i: (0, j))],
        out_specs=pl.BlockSpec((bm, bn), lambda j, i: (i, j)),
        out_shape=jax.ShapeDtypeStruct((m, n), F32),
        compiler_params=_params("parallel", "parallel"),
        name="matmul_bf16w",
    )(a, w)


def _mem_attn_kernel(head_dim, q_ref, k_ref, v_ref, *refs):
    o_ref = refs[-1]
    scale = head_dim ** -0.5
    for h in range(N_MEM_HEADS):
        cols = slice(h * head_dim, (h + 1) * head_dim)
        q = q_ref[:, cols]
        k = k_ref[:, cols].astype(BF16)
        v = v_ref[:, cols].astype(BF16)
        s = lax.dot_general(q, k, (((1,), (1,)), ((), ())), preferred_element_type=F32) * scale
        m = jnp.max(s, axis=-1, keepdims=True)
        p = jnp.exp(s - m)
        p = p * (1.0 / jnp.sum(p, axis=-1, keepdims=True))
        o_ref[:, cols] = jnp.dot(p.astype(BF16), v, preferred_element_type=F32).astype(o_ref.dtype)


def _mem_attn(q, q_col, width, mk, mv, layer, k_col, v_col, dest=None):
    b, l, _ = q.shape
    n_mem = mk.shape[2]
    head_dim = width // N_MEM_HEADS
    bl = _row_block(l, 512)
    in_specs = [pl.BlockSpec((None, bl, width), lambda i, j: (i, j, q_col)),
                pl.BlockSpec((None, None, n_mem, width), lambda i, j: (layer, i, 0, k_col)),
                pl.BlockSpec((None, None, n_mem, width), lambda i, j: (layer, i, 0, v_col))]
    args = [q, mk, mv]
    out_cols, out_col, aliases = width, 0, {}
    if dest is not None:
        out_cols = dest.shape[-1]
        out_col = out_cols // width - 1
        in_specs.append(pl.BlockSpec(memory_space=pl.ANY))
        args.append(dest)
        aliases = {3: 0}
    return pl.pallas_call(
        functools.partial(_mem_attn_kernel, head_dim),
        grid=(b, l // bl),
        in_specs=in_specs,
        out_specs=pl.BlockSpec((None, bl, width), lambda i, j: (i, j, out_col)),
        out_shape=jax.ShapeDtypeStruct((b, l, out_cols), BF16),
        input_output_aliases=aliases,
        compiler_params=_params("parallel", "parallel"),
        name="mem_attn",
    )(*args)


def _tile_rows(cache):
    lyr, b, n_mem, h, hd = cache.shape
    t = hd // LANES
    v = cache.reshape(lyr, b, n_mem, h, t, LANES)
    return jnp.swapaxes(v, 3, 4).reshape(lyr, b, n_mem * t * h, LANES)


def _mem_attn_decode_kernel(n_heads, q_ref, k_ref, v_ref, o_ref):
    tiles = q_ref.shape[-1] // (n_heads * LANES)
    head_dim = tiles * LANES
    group = tiles * n_heads
    n_mem = k_ref.shape[0] // group
    scale = head_dim ** -0.5

    def head_tile(ref, h, t):
        return ref[pl.ds(t * n_heads + h, n_mem, stride=group), :].astype(BF16)

    for h in range(n_heads):
        s = jnp.zeros((q_ref.shape[0], n_mem), F32)
        for t in range(tiles):
            q = q_ref[:, h * head_dim + t * LANES:h * head_dim + (t + 1) * LANES]
            s = s + lax.dot_general(q, head_tile(k_ref, h, t), (((1,), (1,)), ((), ())),
                                    preferred_element_type=F32)
        s = s * scale
        m = jnp.max(s, axis=-1, keepdims=True)
        p = jnp.exp(s - m)
        p = (p * (1.0 / jnp.sum(p, axis=-1, keepdims=True))).astype(BF16)
        for t in range(tiles):
            o = jnp.dot(p, head_tile(v_ref, h, t), preferred_element_type=F32)
            o_ref[:, h * head_dim + t * LANES:h * head_dim + (t + 1) * LANES] = o.astype(o_ref.dtype)


def _mem_attn_decode(q, mk_rows, mv_rows, layer):
    b, l, width = q.shape
    rows = mk_rows.shape[2]
    kv_spec = pl.BlockSpec((None, None, rows, LANES), lambda i: (layer, i, 0, 0))
    return pl.pallas_call(
        functools.partial(_mem_attn_decode_kernel, N_MEM_HEADS),
        grid=(b,),
        in_specs=[pl.BlockSpec((None, l, width), lambda i: (i, 0, 0)), kv_spec, kv_spec],
        out_specs=pl.BlockSpec((None, l, width), lambda i: (i, 0, 0)),
        out_shape=jax.ShapeDtypeStruct((b, l, width), BF16),
        compiler_params=_params("parallel"),
        name="mem_attn_decode",
    )(q, mk_rows, mv_rows)


def _sink_softmax(s, sink):
    m = jnp.maximum(jnp.max(s, axis=-1, keepdims=True), sink)
    p = jnp.exp(s - m)
    return p * (1.0 / (jnp.sum(p, axis=-1, keepdims=True) + jnp.exp(sink - m)))


def _swa_prompt_kernel(slopes, sink_ref, q_ref, kp_ref, kc_ref, vp_ref, vc_ref, o_ref):
    w = WINDOW
    blk = pl.program_id(1)
    qi = lax.broadcasted_iota(jnp.int32, (w, 2 * w), 0)
    kj = lax.broadcasted_iota(jnp.int32, (w, 2 * w), 1)
    dist = w + qi - kj
    valid = (dist >= 0) & (dist < WINDOW) & ((blk - 1) * w + kj >= 0)
    distf = dist.astype(F32)
    scale = HEAD_DIM ** -0.5
    group = len(slopes) // N_KV_HEADS
    for hk in range(N_KV_HEADS):
        kcols = slice(hk * HEAD_DIM, (hk + 1) * HEAD_DIM)
        kcat = jnp.concatenate([kp_ref[:, kcols], kc_ref[:, kcols]], axis=0).astype(BF16)
        vcat = jnp.concatenate([vp_ref[:, kcols], vc_ref[:, kcols]], axis=0).astype(BF16)
        for g in range(group):
            h = hk * group + g
            cols = slice(h * HEAD_DIM, (h + 1) * HEAD_DIM)
            s = lax.dot_general(q_ref[:, cols], kcat, (((1,), (1,)), ((), ())),
                                preferred_element_type=F32) * scale
            s = s - float(slopes[h]) * distf
            s = jnp.where(valid, s, -jnp.inf)
            p = _sink_softmax(s, sink_ref[h])
            o_ref[:, cols] = jnp.dot(p.astype(BF16), vcat, preferred_element_type=F32).astype(o_ref.dtype)


def _swa_prompt(q, kv, sinks, slopes, out_cols):
    b, l, _ = q.shape
    w = WINDOW
    kvw = N_KV_HEADS * HEAD_DIM
    tw = len(slopes) * HEAD_DIM
    assert l % w == 0
    prev = lambda i, j: jnp.maximum(j - 1, 0)
    return pl.pallas_call(
        functools.partial(_swa_prompt_kernel, slopes),
        grid=(b, l // w),
        in_specs=[pl.BlockSpec(memory_space=pltpu.SMEM),
                  pl.BlockSpec((None, w, tw), lambda i, j: (i, j, 0)),
                  pl.BlockSpec((None, w, kvw), lambda i, j: (i, prev(i, j), 0)),
                  pl.BlockSpec((None, w, kvw), lambda i, j: (i, j, 0)),
                  pl.BlockSpec((None, w, kvw), lambda i, j: (i, prev(i, j), 1)),
                  pl.BlockSpec((None, w, kvw), lambda i, j: (i, j, 1))],
        out_specs=pl.BlockSpec((None, w, tw), lambda i, j: (i, j, 0)),
        out_shape=jax.ShapeDtypeStruct((b, l, out_cols), BF16),
        compiler_params=_params("parallel", "parallel"),
        name="swa_prompt",
    )(sinks, q, kv, kv, kv, kv)


def _swa_sample_kernel(slopes, sink_ref, q_ref, k_ref, v_ref, o_ref):
    n_heads = len(slopes)
    group = n_heads // N_KV_HEADS
    wb = k_ref.shape[0] // N_KV_HEADS
    q = q_ref[...].astype(BF16)

    def kv_head(ref, hk):
        return ref[pl.ds(hk, wb, stride=N_KV_HEADS), :].astype(BF16)

    head = lax.broadcasted_iota(jnp.int32, (n_heads, wb), 0)
    pos = lax.broadcasted_iota(jnp.int32, (n_heads, wb), 1)
    distf = (wb - 1 - pos).astype(F32)
    slope = jnp.zeros((n_heads, wb), F32)
    sink = jnp.zeros((n_heads, 1), F32)
    head_col = lax.broadcasted_iota(jnp.int32, (n_heads, 1), 0)
    for h in range(n_heads):
        slope = jnp.where(head == h, float(slopes[h]), slope)
        sink = jnp.where(head_col == h, sink_ref[h], sink)
    scale = HEAD_DIM ** -0.5
    s = jnp.zeros((n_heads, wb), F32)
    for hk in range(N_KV_HEADS):
        s_hk = lax.dot_general(q, kv_head(k_ref, hk), (((1,), (1,)), ((), ())), preferred_element_type=F32)
        s = jnp.where(head // group == hk, s_hk, s)
    s = s * scale - slope * distf
    p = _sink_softmax(s, sink).astype(BF16)
    o = jnp.zeros((n_heads, HEAD_DIM), F32)
    head_o = lax.broadcasted_iota(jnp.int32, (n_heads, HEAD_DIM), 0)
    for hk in range(N_KV_HEADS):
        o_hk = jnp.dot(p, kv_head(v_ref, hk), preferred_element_type=F32)
        o = jnp.where(head_o // group == hk, o_hk, o)
    o_ref[...] = o.astype(o_ref.dtype)


def _swa_sample(q, win_k, win_v, sinks, slopes):
    b, n_heads, _ = q.shape
    rows = win_k.shape[1] * N_KV_HEADS
    win_k = win_k.reshape(b, rows, HEAD_DIM)
    win_v = win_v.reshape(b, rows, HEAD_DIM)
    return pl.pallas_call(
        functools.partial(_swa_sample_kernel, slopes),
        grid=(b,),
        in_specs=[pl.BlockSpec(memory_space=pltpu.SMEM),
                  pl.BlockSpec((None, n_heads, HEAD_DIM), lambda i: (i, 0, 0)),
                  pl.BlockSpec((None, rows, HEAD_DIM), lambda i: (i, 0, 0)),
                  pl.BlockSpec((None, rows, HEAD_DIM), lambda i: (i, 0, 0))],
        out_specs=pl.BlockSpec((None, n_heads, HEAD_DIM), lambda i: (i, 0, 0)),
        out_shape=jax.ShapeDtypeStruct((b, n_heads, HEAD_DIM), BF16),
        compiler_params=_params("parallel"),
        name="swa_sample",
    )(sinks, q, win_k, win_v)


def _lane_pick(x, lane):
    idx = lax.broadcasted_iota(jnp.int32, x.shape, x.ndim - 1)
    return jnp.sum(jnp.where(idx == lane, x, 0.0), axis=-1, keepdims=True)


def _gates(ba, alog, dtb):
    beta = jax.nn.sigmoid(ba)
    g = -jnp.exp(alog) * _softplus(ba + dtb)
    return beta, g


def _bdot(a, b, contract_b_last, precision=None):
    dims = (((2,), (2 if contract_b_last else 1,)), ((0,), (0,)))
    return lax.dot_general(a, b, dims, preferred_element_type=F32, precision=precision)


def _delta_prompt_kernel(n_heads, hb, q_ref, k_ref, v_ref, z_ref, ba_ref, wq_ref, wk_ref, wv_ref,
                         alog_ref, dtb_ref, gn_ref, o_ref, s_ref, pad_ref, beta_ref, g_ref):
    l = q_ref.shape[0]
    c = DELTA_CHUNK
    n = l // c
    hd = HEAD_DIM
    hstep = pl.program_id(1)

    @pl.when(hstep == 0)
    def _():
        beta_all, g_all = _gates(ba_ref[...], alog_ref[...], dtb_ref[...])
        beta_ref[...] = beta_all
        g_ref[...] = g_all

    pad_ref[0:8, :] = jnp.zeros((8, hd), F32)

    def conv(x_ref, w_ref, cols):
        pad_ref[8:8 + l, :] = x_ref[:, cols]
        base = 8 - (CONV_W - 1)
        out = pad_ref[base:base + l, :] * w_ref[0:1, cols]
        for j in range(1, CONV_W):
            out = out + pad_ref[base + j:base + j + l, :] * w_ref[j:j + 1, cols]
        return _silu(out)

    def l2norm(x):
        return x * lax.rsqrt(jnp.sum(x * x, axis=-1, keepdims=True) + EPS)

    row = lax.broadcasted_iota(jnp.int32, (1, c, c), 1)
    col = lax.broadcasted_iota(jnp.int32, (1, c, c), 2)
    incl = row >= col
    strict = row > col
    eye = row == col

    def lower_left(shift):
        return (((row >> (shift + 1)) == (col >> (shift + 1)))
                & (((row >> shift) & 1) == 1) & (((col >> shift) & 1) == 0))

    def prepare(hh):
        cols = slice(hh * hd, (hh + 1) * hd)
        head = hstep * hb + hh
        q3 = (l2norm(conv(q_ref, wq_ref, cols)) * (hd ** -0.5)).reshape(n, c, hd)
        k3 = l2norm(conv(k_ref, wk_ref, cols)).reshape(n, c, hd)
        v3 = conv(v_ref, wv_ref, cols).reshape(n, c, hd)
        beta = _lane_pick(beta_ref[...], head).reshape(n, c, 1)
        g = _lane_pick(g_ref[...], n_heads + head).reshape(n, c, 1)
        gc_row = jnp.sum(jnp.where(row <= col, jnp.broadcast_to(g, (n, c, c)), 0.0), axis=1, keepdims=True)
        gc_col = jnp.sum(jnp.where(eye, jnp.broadcast_to(gc_row, (n, c, c)), 0.0), axis=2, keepdims=True)
        decay = jnp.where(incl, jnp.exp(jnp.where(incl, gc_col - gc_row, 0.0)), 0.0)

        kb = k3.astype(BF16)
        qkk = _bdot(jnp.concatenate([q3, k3], axis=1).astype(BF16), kb, True)
        qk = qkk[:, :c] * decay
        lmat = jnp.where(strict, qkk[:, c:] * decay * beta, 0.0)
        tinv = jnp.where(eye, 1.0, 0.0) - jnp.where(lower_left(0), lmat, 0.0)
        for shift in range(1, int(math.log2(c))):
            cs = jnp.where(lower_left(shift), lmat, 0.0).astype(BF16)
            tb = tinv.astype(BF16)
            tinv = tinv - _bdot(_bdot(tb, cs, False).astype(BF16), tb, False)
        egc = jnp.exp(gc_col)
        rhs = jnp.concatenate([v3 * beta, k3 * (beta * egc)], axis=2).astype(BF16)
        uw = _bdot(tinv.astype(BF16), rhs, False)
        g_last = gc_col[:, c - 1:c, :]
        return dict(
            u=uw[:, :, :hd],
            wq=jnp.concatenate([uw[:, :, hd:], q3 * egc], axis=1).astype(BF16),
            qk=qk.astype(BF16),
            kt=(k3 * jnp.exp(g_last - gc_col)).astype(BF16),
            eg_last=jnp.exp(g_last))

    heads = [prepare(hh) for hh in range(hb)]
    states = [jnp.zeros((hd, hd), F32) for _ in range(hb)]
    for i in range(n):
        for hh, p in enumerate(heads):
            cols = slice(hh * hd, (hh + 1) * hd)
            tok = slice(i * c, (i + 1) * c)
            sb = states[hh].astype(BF16)
            ws = jnp.dot(p["wq"][i], sb, preferred_element_type=F32)
            vnb = (p["u"][i] - ws[:c]).astype(BF16)
            o = ws[c:] + jnp.dot(p["qk"][i], vnb, preferred_element_type=F32)
            states[hh] = states[hh] * p["eg_last"][i] + lax.dot_general(
                p["kt"][i], vnb, (((0,), (0,)), ((), ())), preferred_element_type=F32)
            o = o * lax.rsqrt(jnp.mean(o * o, axis=-1, keepdims=True) + EPS) * gn_ref[...]
            o_ref[tok, cols] = (o * _silu(z_ref[tok, cols])).astype(o_ref.dtype)
    for hh in range(hb):
        s_ref[hh] = states[hh]


def _delta_prompt(proj, ba, w_conv, layer, alog_pad, dtb_pad, gate_norm, n_heads, out_cols):
    b, l, _ = proj.shape
    hd = HEAD_DIM
    tw = n_heads * hd
    hb = DELTA_HEADS_PER_STEP
    assert n_heads % hb == 0
    steps = n_heads // hb
    col = lambda part: pl.BlockSpec((None, l, hb * hd), lambda i, h: (i, 0, part * steps + h))
    wcol = lambda part: pl.BlockSpec((None, CONV_W, hb * hd), lambda i, h: (layer, 0, part * steps + h))
    vec = pl.BlockSpec((1, LANES), lambda i, h: (0, 0))
    return pl.pallas_call(
        functools.partial(_delta_prompt_kernel, n_heads, hb),
        grid=(b, steps),
        in_specs=[col(0), col(1), col(2), col(3),
                  pl.BlockSpec((None, l, LANES), lambda i, h: (i, 0, 0)),
                  wcol(0), wcol(1), wcol(2), vec, vec, vec],
        out_specs=[pl.BlockSpec((None, l, hb * hd), lambda i, h: (i, 0, h)),
                   pl.BlockSpec((None, hb, hd, hd), lambda i, h: (i, h, 0, 0))],
        out_shape=[jax.ShapeDtypeStruct((b, l, out_cols), BF16),
                   jax.ShapeDtypeStruct((b, n_heads, hd, hd), F32)],
        scratch_shapes=[pltpu.VMEM((l + 8, hd), F32), pltpu.VMEM((l, LANES), F32), pltpu.VMEM((l, LANES), F32)],
        compiler_params=_params("parallel", "arbitrary"),
        name="delta_prompt",
    )(proj, proj, proj, proj, ba, w_conv, w_conv, w_conv, alog_pad, dtb_pad, gate_norm)


def _delta_sample_kernel(n_heads, x_ref, z_ref, ba_ref, hist_ref, wc_ref, s_ref, alog_ref, dtb_ref, gn_ref,
                         o_ref, hist_o_ref, s_o_ref):
    hd = HEAD_DIM
    tw = n_heads * hd
    x = x_ref[...]
    acc = hist_ref[0:1, :] * wc_ref[0:1, :]
    for j in range(1, CONV_W - 1):
        acc = acc + hist_ref[j:j + 1, :] * wc_ref[j:j + 1, :]
    acc = _silu(acc + x * wc_ref[CONV_W - 1:CONV_W, :])
    hist_o_ref[0:CONV_W - 2, :] = hist_ref[1:CONV_W - 1, :]
    hist_o_ref[CONV_W - 2:CONV_W - 1, :] = x

    beta_all, g_all = _gates(ba_ref[...], alog_ref[...], dtb_ref[...])
    row = lax.broadcasted_iota(jnp.int32, (hd, hd), 0)
    col = lax.broadcasted_iota(jnp.int32, (hd, hd), 1)
    eye = row == col

    def to_col(r):
        return jnp.sum(jnp.where(eye, jnp.broadcast_to(r, (hd, hd)), 0.0), axis=1, keepdims=True)

    def l2norm(r):
        return r * lax.rsqrt(jnp.sum(r * r, axis=-1, keepdims=True) + EPS)

    for h in range(n_heads):
        cols = slice(h * hd, (h + 1) * hd)
        q = l2norm(acc[:, cols]) * (hd ** -0.5)
        k = l2norm(acc[:, tw + h * hd: tw + (h + 1) * hd])
        v = acc[:, 2 * tw + h * hd: 2 * tw + (h + 1) * hd]
        beta = _lane_pick(beta_all, h)
        g = _lane_pick(g_all, n_heads + h)
        k_col = to_col(k)
        s = s_ref[h] * jnp.exp(g)
        vn = beta * (v - jnp.sum(s * k_col, axis=0, keepdims=True))
        s = s + k_col * vn
        s_o_ref[h] = s
        o = jnp.sum(s * to_col(q), axis=0, keepdims=True)
        o = o * lax.rsqrt(jnp.mean(o * o, axis=-1, keepdims=True) + EPS) * gn_ref[...]
        o_ref[:, cols] = (o * _silu(z_ref[:, cols])).astype(o_ref.dtype)


def _delta_sample(proj, ba, state_conv, w_conv, state_delta, layer, alog_pad, dtb_pad, gate_norm, n_heads):
    b = proj.shape[0]
    hd = HEAD_DIM
    tw = n_heads * hd
    hist = CONV_W - 1
    vec = pl.BlockSpec((1, LANES), lambda i: (0, 0))
    return pl.pallas_call(
        functools.partial(_delta_sample_kernel, n_heads),
        grid=(b,),
        in_specs=[pl.BlockSpec((None, 1, 3 * tw), lambda i: (i, 0, 0)),
                  pl.BlockSpec((None, 1, tw), lambda i: (i, 0, 3)),
                  pl.BlockSpec((None, 1, LANES), lambda i: (i, 0, 0)),
                  pl.BlockSpec((None, None, hist, 3 * tw), lambda i: (layer, i, 0, 0)),
                  pl.BlockSpec((None, CONV_W, 3 * tw), lambda i: (layer, 0, 0)),
                  pl.BlockSpec((None, None, n_heads, hd, hd), lambda i: (layer, i, 0, 0, 0)),
                  vec, vec, vec],
        out_specs=[pl.BlockSpec((None, 1, tw), lambda i: (i, 0, 0)),
                   pl.BlockSpec((None, hist, 3 * tw), lambda i: (i, 0, 0)),
                   pl.BlockSpec((None, n_heads, hd, hd), lambda i: (i, 0, 0, 0))],
        out_shape=[jax.ShapeDtypeStruct((b, 1, tw), BF16),
                   jax.ShapeDtypeStruct((b, hist, 3 * tw), F32),
                   jax.ShapeDtypeStruct((b, n_heads, hd, hd), F32)],
        compiler_params=_params("parallel"),
        name="delta_sample",
    )(proj, proj, ba, state_conv, w_conv, state_delta, alog_pad, dtb_pad, gate_norm)


def _trunk(x_p, x_s, mem_kv_p, cache_mk, cache_mv, conv_state, delta_state, buf_k, buf_v, wts):
    bp, lp, d = x_p.shape
    bs, ls, _ = x_s.shape
    assert ls == 1
    depth = wts["w_out"].shape[0]
    n_a = wts["w_in_a_t"].shape[0]
    d_ff = wts["w_down"].shape[1]
    tw = wts["w_conv"].shape[-1] // 3
    n_heads = tw // HEAD_DIM
    mem_w = d - tw
    kvw = N_KV_HEADS * HEAD_DIM
    slopes = _alibi_slopes(n_heads)
    off_b = 4 * tw
    off_qm = off_b + 2 * n_heads

    def lane_pad(vals):
        return jnp.zeros((1, LANES), F32).at[0, n_heads:2 * n_heads].set(vals.astype(F32))

    def resid(xp, fp, xs, fs, w_post, w_next):
        xp, np_ = _resid_norm(xp, fp, w_post, w_next)
        xs, ns_ = _resid_norm(xs, fs, w_post, w_next)
        return xp, xs, np_, ns_

    xp2 = x_p.reshape(bp * lp, d)
    xs2 = x_s.reshape(bs, d)
    (xn_p,) = _norm_cast(xp2, wts["norm_mix_pre"][0:1])
    (xn_s,) = _norm_cast(xs2, wts["norm_mix_pre"][0:1])
    conv_p, conv_s, delta_p, delta_s = [], [], [], []
    kv_p = win_k = win_v = None
    for layer in range(depth):
        if layer < n_a:
            wt = wts["w_in_a_t"]
            proj_p, proj_s = _matmul(xn_p, wt, tail=xn_s, layer=layer, n_cols=off_b, w_is_nk=True)
            w_ba = jnp.pad(wt[layer, off_b:off_qm, :], ((0, LANES - 2 * n_heads), (0, 0)))
            ba_p, ba_s = _matmul(xn_p, w_ba, tail=xn_s, w_is_nk=True)
            qm_p, qm_s = _matmul(xn_p, wt[layer, off_qm:, :], tail=xn_s, out_dtype=BF16, w_is_nk=True)
            alog_pad = lane_pad(wts["a_log"][layer])
            dtb_pad = lane_pad(wts["dt_bias"][layer])
            gate_norm = wts["w_gate_norm"][layer].reshape(1, HEAD_DIM)
            proj_p = proj_p.reshape(bp, lp, off_b)
            tok_p, s_p = _delta_prompt(proj_p, ba_p.reshape(bp, lp, LANES), wts["w_conv"], layer,
                                       alog_pad, dtb_pad, gate_norm, n_heads, d)
            conv_p.append(proj_p[:, lp - (CONV_W - 1):, :3 * tw])
            tok_s, hist_s, s_s = _delta_sample(proj_s.reshape(bs, 1, off_b), ba_s.reshape(bs, 1, LANES), conv_state,
                                               wts["w_conv"], delta_state, layer, alog_pad, dtb_pad, gate_norm,
                                               n_heads)
            conv_s.append(hist_s)
            delta_p.append(s_p)
            delta_s.append(s_s)
            mixed_p = _mem_attn(qm_p.reshape(bp, lp, mem_w), 0, mem_w, mem_kv_p, mem_kv_p, layer, 0, 1, dest=tok_p)
            mo_s = _mem_attn_decode(qm_s.reshape(bs, 1, mem_w), cache_mk, cache_mv, layer)
        else:
            lb = layer - n_a
            sinks = wts["sinks"][lb].astype(F32)
            proj_p, proj_s = _matmul(xn_p, wts["w_in_b"], tail=xn_s, layer=lb, out_dtype=BF16)
            proj_p = proj_p.reshape(bp, lp, d)
            tok_p = _swa_prompt(proj_p, kv_p, sinks, slopes, d)
            mixed_p = _mem_attn(proj_p, tw // mem_w, mem_w, mem_kv_p, mem_kv_p, layer, 0, 1, dest=tok_p)
            q_s = proj_s[:, :tw].reshape(bs, n_heads, HEAD_DIM)
            tok_s = _swa_sample(q_s, win_k, win_v, sinks, slopes).reshape(bs, 1, tw)
            mo_s = _mem_attn_decode(proj_s[:, tw:].reshape(bs, 1, mem_w), cache_mk, cache_mv, layer)
        mixed_p = mixed_p.reshape(bp * lp, d)
        mixed_s = jnp.concatenate([tok_s, mo_s], axis=-1).reshape(bs, d)
        mix_p, mix_s = _matmul(mixed_p, wts["w_out"], tail=mixed_s, layer=layer)
        xp2, xs2, (hn_p,), (hn_s,) = resid(xp2, mix_p, xs2, mix_s, wts["norm_mix_post"][layer],
                                           wts["norm_ffn_pre"][layer:layer + 1])
        hid_p, hid_s, w_down_bf16 = _swiglu_matmul(hn_p, hn_s, wts["w_gate_up"], wts["w_down"], layer)
        f_p = _matmul_bf16w(hid_p, w_down_bf16)
        f_s = _matmul_bf16w(hid_s, w_down_bf16)
        w_post = wts["norm_ffn_post"][layer]
        if layer + 1 == depth:
            xp2, xs2, _, _ = resid(xp2, f_p, xs2, f_s, w_post, None)
        elif layer + 1 == n_a:
            w_next = jnp.stack([wts["norm_mix_pre"][layer + 1], wts["norm_kv"]])
            xp2, xs2, (xn_p, xkv_p), (xn_s, xkv_s) = resid(xp2, f_p, xs2, f_s, w_post, w_next)
            kv_p, kv_s = _matmul(xkv_p, wts["w_kv"], tail=xkv_s)
            kv_p = kv_p.reshape(bp, lp, 2 * kvw)
            new_row = lambda t: t.reshape(bs, 1, N_KV_HEADS, HEAD_DIM)
            win_k = jnp.concatenate([buf_k[:, 1:], new_row(kv_s[:, :kvw])], axis=1)
            win_v = jnp.concatenate([buf_v[:, 1:], new_row(kv_s[:, kvw:])], axis=1)
        else:
            xp2, xs2, (xn_p,), (xn_s,) = resid(xp2, f_p, xs2, f_s, w_post,
                                               wts["norm_mix_pre"][layer + 1:layer + 2])
    wp = min(WINDOW, lp)
    heads = lambda t: t.reshape(t.shape[0], -1, N_KV_HEADS, HEAD_DIM)
    return (xp2.reshape(bp, lp, d), xs2.reshape(bs, 1, d),
            heads(kv_p[:, lp - wp:, :kvw]), heads(kv_p[:, lp - wp:, kvw:]), jnp.stack(conv_p), jnp.stack(delta_p),
            heads(win_k), heads(win_v), jnp.stack(conv_s), jnp.stack(delta_s))


def kernel(x_prompt, x_sample, cache_mem_k, cache_mem_v, cache_swa_k, cache_swa_v, state_conv, state_delta, mem_prompt, w_in_a, w_conv, a_log, dt_bias, w_gate_norm, w_in_b, sinks, norm_kv, w_kv, norm_mem, w_mem_kv, w_out, norm_mix_pre, norm_mix_post, norm_ffn_pre, norm_ffn_post, w_gate_up, w_down):
    wts = dict(w_in_a_t=jnp.swapaxes(w_in_a, 1, 2), w_conv=w_conv, a_log=a_log, dt_bias=dt_bias,
               w_gate_norm=w_gate_norm, w_in_b=w_in_b, sinks=sinks, norm_kv=norm_kv, w_kv=w_kv, w_out=w_out,
               norm_mix_pre=norm_mix_pre, norm_mix_post=norm_mix_post, norm_ffn_pre=norm_ffn_pre,
               norm_ffn_post=norm_ffn_post, w_gate_up=w_gate_up, w_down=w_down)
    depth = w_out.shape[0]
    bp, n_mem, d = mem_prompt.shape
    bs = x_sample.shape[0]
    mem_w = w_mem_kv.shape[-1] // 2
    mem_shape = (depth, bp, n_mem, N_MEM_HEADS, mem_w // N_MEM_HEADS)

    memn = _norm_cast(mem_prompt.reshape(bp * n_mem, d), norm_mem)
    mem_kv = jnp.stack([_matmul(memn[i], w_mem_kv, layer=i) for i in range(depth)])
    mem_kv = mem_kv.reshape(depth, bp, n_mem, 2 * mem_w)
    mem_k_p = mem_kv[..., :mem_w].reshape(mem_shape)
    mem_v_p = mem_kv[..., mem_w:].reshape(mem_shape)
    cmk = _tile_rows(cache_mem_k)
    cmv = _tile_rows(cache_mem_v)
    y_p, y_s, swk_p, swv_p, conv_p, delta_p, swk_s, swv_s, conv_s, delta_s = _trunk(
        x_prompt, x_sample, mem_kv, cmk, cmv, state_conv, state_delta, cache_swa_k, cache_swa_v, wts)
    return (y_p, y_s, mem_k_p, mem_v_p, swk_p, swv_p, conv_p, delta_p, swk_s, swv_s, conv_s, delta_s)
```

```python
import functools
import math

import numpy as np
import jax
import jax.numpy as jnp
from jax import lax
from jax.experimental import pallas as pl
from jax.experimental.pallas import tpu as pltpu

F32 = jnp.float32
BF16 = jnp.bfloat16
EPS = 1e-6

HEAD_DIM = 128
N_MEM_HEADS = 4
N_KV_HEADS = 8
WINDOW = 128
CONV_W = 4
DELTA_CHUNK = 128
DELTA_HEADS_PER_STEP = 2
LANES = 128
VMEM_LIMIT_BYTES = 56 * 1024 * 1024


def _params(*sem):
    return pltpu.CompilerParams(dimension_semantics=sem, vmem_limit_bytes=VMEM_LIMIT_BYTES)


def _alibi_slopes(n):
    def pow2_slopes(m):
        start = 2.0 ** (-8.0 / m)
        return [start ** (i + 1) for i in range(m)]
    c = 2 ** int(math.floor(math.log2(n)))
    s = pow2_slopes(c)
    if c < n:
        s = s + pow2_slopes(2 * c)[0::2][: n - c]
    return np.asarray(s, np.float32)


def _rms_rows(x, w):
    return x * lax.rsqrt(jnp.mean(x * x, axis=-1, keepdims=True) + EPS) * w


def _silu(x):
    return x * jax.nn.sigmoid(x)


def _softplus(x):
    return jnp.maximum(x, 0.0) + jnp.log1p(jnp.exp(-jnp.abs(x)))


def _row_block(m, target):
    b = min(m, target)
    assert m % b == 0, (m, b)
    return b


def _norm_kernel(x_ref, w_ref, *o_refs):
    x = x_ref[...]
    for j, o_ref in enumerate(o_refs):
        o_ref[...] = _rms_rows(x, w_ref[j:j + 1, :]).astype(o_ref.dtype)


def _norm_cast(x, ws):
    m, d = x.shape
    n = ws.shape[0]
    br = _row_block(m, 256)
    return pl.pallas_call(
        _norm_kernel,
        grid=(m // br,),
        in_specs=[pl.BlockSpec((br, d), lambda i: (i, 0)),
                  pl.BlockSpec((n, d), lambda i: (0, 0))],
        out_specs=[pl.BlockSpec((br, d), lambda i: (i, 0))] * n,
        out_shape=[jax.ShapeDtypeStruct((m, d), BF16)] * n,
        compiler_params=_params("parallel"),
        name="norm_cast",
    )(x, ws)


def _resid_kernel(n_next, x_ref, f_ref, wpost_ref, *refs):
    if n_next:
        wnext_ref, xo_ref = refs[0], refs[1]
        xn_refs = refs[2:]
    else:
        xo_ref = refs[0]
        xn_refs = ()
    xnew = x_ref[...] + _rms_rows(f_ref[...], wpost_ref[...])
    xo_ref[...] = xnew
    for j, o_ref in enumerate(xn_refs):
        o_ref[...] = _rms_rows(xnew, wnext_ref[j:j + 1, :]).astype(o_ref.dtype)


def _resid_norm(x, f, w_post, w_next):
    m, d = x.shape
    n_next = 0 if w_next is None else w_next.shape[0]
    br = _row_block(m, 256)
    row = pl.BlockSpec((br, d), lambda i: (i, 0))
    in_specs = [row, row, pl.BlockSpec((1, d), lambda i: (0, 0))]
    args = [x, f, w_post.reshape(1, d)]
    if n_next:
        in_specs.append(pl.BlockSpec((n_next, d), lambda i: (0, 0)))
        args.append(w_next)
    outs = pl.pallas_call(
        functools.partial(_resid_kernel, n_next),
        grid=(m // br,),
        in_specs=in_specs,
        out_specs=[row] * (1 + n_next),
        out_shape=[jax.ShapeDtypeStruct((m, d), F32)] + [jax.ShapeDtypeStruct((m, d), BF16)] * n_next,
        compiler_params=_params("parallel"),
        name="resid_norm",
    )(*args)
    return outs[0], list(outs[1:])


def _w_spec(w, layer, k, bn, col_block, w_is_nk=False):
    shape = (bn, k) if w_is_nk else (k, bn)
    pos = (lambda n: (col_block(n), 0)) if w_is_nk else (lambda n: (0, col_block(n)))
    if w.ndim == 3:
        return pl.BlockSpec((None,) + shape, lambda n, m: (layer,) + pos(n))
    return pl.BlockSpec(shape, lambda n, m: pos(n))


def _rows_with_tail(a_ref, tail_ref):
    return jnp.concatenate([a_ref[...], tail_ref[...]], axis=0)


def _mm_kernel(has_tail, w_is_nk, *refs):
    if has_tail:
        a_ref, tail_ref, w_ref, o_ref, otail_ref, wb_ref = refs
    else:
        a_ref, w_ref, o_ref, wb_ref = refs
    step = pl.program_id(1)

    @pl.when(step == 0)
    def _():
        wb_ref[...] = w_ref[...].astype(BF16)

    def mm(rows):
        if w_is_nk:
            return lax.dot_general(rows, wb_ref[...], (((1,), (1,)), ((), ())), preferred_element_type=F32)
        return jnp.dot(rows, wb_ref[...], preferred_element_type=F32)

    if not has_tail:
        o_ref[...] = mm(a_ref[...]).astype(o_ref.dtype)
        return
    last = pl.num_programs(1) - 1

    @pl.when(step < last)
    def _():
        o_ref[...] = mm(a_ref[...]).astype(o_ref.dtype)

    @pl.when(step == last)
    def _():
        bm = a_ref.shape[0]
        r = mm(_rows_with_tail(a_ref, tail_ref))
        o_ref[...] = r[:bm].astype(o_ref.dtype)
        otail_ref[...] = r[bm:].astype(otail_ref.dtype)


def _matmul(a, w, *, tail=None, layer=0, col_off=0, n_cols=None, out_dtype=F32, bn=512, bm=1024, w_is_nk=False):
    m, k = a.shape
    n_total = w.shape[-2] if w_is_nk else w.shape[-1]
    n_cols = n_total - col_off if n_cols is None else n_cols
    bn = min(bn, n_cols)
    assert n_cols % bn == 0 and col_off % bn == 0, (n_cols, col_off, bn)
    bm = _row_block(m, bm)
    off = col_off // bn
    in_specs = [pl.BlockSpec((bm, k), lambda n, i: (i, 0))]
    out_specs = [pl.BlockSpec((bm, bn), lambda n, i: (i, n))]
    out_shape = [jax.ShapeDtypeStruct((m, n_cols), out_dtype)]
    args = [a]
    if tail is not None:
        mt = tail.shape[0]
        in_specs.append(pl.BlockSpec((mt, k), lambda n, i: (0, 0)))
        out_specs.append(pl.BlockSpec((mt, bn), lambda n, i: (0, n)))
        out_shape.append(jax.ShapeDtypeStruct((mt, n_cols), out_dtype))
        args.append(tail)
    in_specs.append(_w_spec(w, layer, k, bn, lambda n: n + off, w_is_nk))
    outs = pl.pallas_call(
        functools.partial(_mm_kernel, tail is not None, w_is_nk),
        grid=(n_cols // bn, m // bm),
        in_specs=in_specs,
        out_specs=out_specs,
        out_shape=out_shape,
        scratch_shapes=[pltpu.VMEM((bn, k) if w_is_nk else (k, bn), BF16)],
        compiler_params=_params("arbitrary", "arbitrary"),
        name="matmul",
    )(*args, w)
    return outs[0] if tail is None else tuple(outs)


def _swiglu_kernel(a_ref, tail_ref, wg_ref, wu_ref, wd_ref, o_ref, otail_ref, wdb_ref, wgb_ref, wub_ref):
    step = pl.program_id(1)
    last = pl.num_programs(1) - 1
    wdb_ref[...] = wd_ref[...].astype(BF16)

    @pl.when(step == 0)
    def _():
        wgb_ref[...] = wg_ref[...].astype(BF16)
        wub_ref[...] = wu_ref[...].astype(BF16)

    def gated(rows):
        g = jnp.dot(rows, wgb_ref[...], preferred_element_type=F32)
        u = jnp.dot(rows, wub_ref[...], preferred_element_type=F32)
        return _silu(g) * u

    @pl.when(step < last)
    def _():
        o_ref[...] = gated(a_ref[...]).astype(o_ref.dtype)

    @pl.when(step == last)
    def _():
        bm = a_ref.shape[0]
        r = gated(_rows_with_tail(a_ref, tail_ref))
        o_ref[...] = r[:bm].astype(o_ref.dtype)
        otail_ref[...] = r[bm:].astype(otail_ref.dtype)


def _swiglu_matmul(a, tail, w, w_down, layer, *, bn=256, bm=1024):
    m, k = a.shape
    mt = tail.shape[0]
    d_ff, n_down = w_down.shape[1:]
    assert d_ff % bn == 0
    bm = _row_block(m, bm)
    nb = d_ff // bn
    m_steps = m // bm
    slab = d_ff // (nb * m_steps)
    assert slab * nb * m_steps == d_ff and slab % 16 == 0, (d_ff, nb, m_steps)
    return pl.pallas_call(
        _swiglu_kernel,
        grid=(nb, m_steps),
        in_specs=[pl.BlockSpec((bm, k), lambda n, i: (i, 0)),
                  pl.BlockSpec((mt, k), lambda n, i: (0, 0)),
                  _w_spec(w, layer, k, bn, lambda n: n),
                  _w_spec(w, layer, k, bn, lambda n: n + nb),
                  pl.BlockSpec((None, slab, n_down), lambda n, i: (layer, n * m_steps + i, 0))],
        out_specs=[pl.BlockSpec((bm, bn), lambda n, i: (i, n)),
                   pl.BlockSpec((mt, bn), lambda n, i: (0, n)),
                   pl.BlockSpec((slab, n_down), lambda n, i: (n * m_steps + i, 0))],
        out_shape=[jax.ShapeDtypeStruct((m, d_ff), BF16), jax.ShapeDtypeStruct((mt, d_ff), BF16),
                   jax.ShapeDtypeStruct((d_ff, n_down), BF16)],
        scratch_shapes=[pltpu.VMEM((k, bn), BF16), pltpu.VMEM((k, bn), BF16)],
        compiler_params=_params("arbitrary", "arbitrary"),
        name="swiglu_matmul",
    )(a, tail, w, w, w_down)


def _mm_bf16w_kernel(a_ref, w_ref, o_ref):
    o_ref[...] = jnp.dot(a_ref[...], w_ref[...], preferred_element_type=F32).astype(o_ref.dtype)


def _matmul_bf16w(a, w, *, bn=512, bm=512):
    m, k = a.shape
    n = w.shape[-1]
    assert n % bn == 0
    bm = _row_block(m, bm)
    return pl.pallas_call(
        _mm_bf16w_kernel,
        grid=(n // bn, m // bm),
        in_specs=[pl.BlockSpec((bm, k), lambda j, i: (i, 0)),
                  pl.BlockSpec((k, bn), lambda j, i: (0, j))],
        out_specs=pl.BlockSpec((bm, bn), lambda j, i: (i, j)),
        out_shape=jax.ShapeDtypeStruct((m, n), F32),
        compiler_params=_params("parallel", "parallel"),
        name="matmul_bf16w",
    )(a, w)


def _mem_attn_kernel(head_dim, q_ref, k_ref, v_ref, *refs):
    o_ref = refs[-1]
    scale = head_dim ** -0.5
    for h in range(N_MEM_HEADS):
        cols = slice(h * head_dim, (h + 1) * head_dim)
        q = q_ref[:, cols]
        k = k_ref[:, cols].astype(BF16)
        v = v_ref[:, cols].astype(BF16)
        s = lax.dot_general(q, k, (((1,), (1,)), ((), ())), preferred_element_type=F32) * scale
        m = jnp.max(s, axis=-1, keepdims=True)
        p = jnp.exp(s - m)
        p = p * (1.0 / jnp.sum(p, axis=-1, keepdims=True))
        o_ref[:, cols] = jnp.dot(p.astype(BF16), v, preferred_element_type=F32).astype(o_ref.dtype)


def _mem_attn(q, q_col, width, mk, mv, layer, k_col, v_col, dest=None):
    b, l, _ = q.shape
    n_mem = mk.shape[2]
    head_dim = width // N_MEM_HEADS
    bl = _row_block(l, 512)
    in_specs = [pl.BlockSpec((None, bl, width), lambda i, j: (i, j, q_col)),
                pl.BlockSpec((None, None, n_mem, width), lambda i, j: (layer, i, 0, k_col)),
                pl.BlockSpec((None, None, n_mem, width), lambda i, j: (layer, i, 0, v_col))]
    args = [q, mk, mv]
    out_cols, out_col, aliases = width, 0, {}
    if dest is not None:
        out_cols = dest.shape[-1]
        out_col = out_cols // width - 1
        in_specs.append(pl.BlockSpec(memory_space=pl.ANY))
        args.append(dest)
        aliases = {3: 0}
    return pl.pallas_call(
        functools.partial(_mem_attn_kernel, head_dim),
        grid=(b, l // bl),
        in_specs=in_specs,
        out_specs=pl.BlockSpec((None, bl, width), lambda i, j: (i, j, out_col)),
        out_shape=jax.ShapeDtypeStruct((b, l, out_cols), BF16),
        input_output_aliases=aliases,
        compiler_params=_params("parallel", "parallel"),
        name="mem_attn",
    )(*args)


def _tile_rows(cache):
    lyr, b, n_mem, h, hd = cache.shape
    t = hd // LANES
    v = cache.reshape(lyr, b, n_mem, h, t, LANES)
    return jnp.swapaxes(v, 3, 4).reshape(lyr, b, n_mem * t * h, LANES)


def _mem_attn_decode_kernel(n_heads, q_ref, k_ref, v_ref, o_ref):
    tiles = q_ref.shape[-1] // (n_heads * LANES)
    head_dim = tiles * LANES
    group = tiles * n_heads
    n_mem = k_ref.shape[0] // group
    scale = head_dim ** -0.5

    def head_tile(ref, h, t):
        return ref[pl.ds(t * n_heads + h, n_mem, stride=group), :].astype(BF16)

    for h in range(n_heads):
        s = jnp.zeros((q_ref.shape[0], n_mem), F32)
        for t in range(tiles):
            q = q_ref[:, h * head_dim + t * LANES:h * head_dim + (t + 1) * LANES]
            s = s + lax.dot_general(q, head_tile(k_ref, h, t), (((1,), (1,)), ((), ())),
                                    preferred_element_type=F32)
        s = s * scale
        m = jnp.max(s, axis=-1, keepdims=True)
        p = jnp.exp(s - m)
        p = (p * (1.0 / jnp.sum(p, axis=-1, keepdims=True))).astype(BF16)
        for t in range(tiles):
            o = jnp.dot(p, head_tile(v_ref, h, t), preferred_element_type=F32)
            o_ref[:, h * head_dim + t * LANES:h * head_dim + (t + 1) * LANES] = o.astype(o_ref.dtype)


def _mem_attn_decode(q, mk_rows, mv_rows, layer):
    b, l, width = q.shape
    rows = mk_rows.shape[2]
    kv_spec = pl.BlockSpec((None, None, rows, LANES), lambda i: (layer, i, 0, 0))
    return pl.pallas_call(
        functools.partial(_mem_attn_decode_kernel, N_MEM_HEADS),
        grid=(b,),
        in_specs=[pl.BlockSpec((None, l, width), lambda i: (i, 0, 0)), kv_spec, kv_spec],
        out_specs=pl.BlockSpec((None, l, width), lambda i: (i, 0, 0)),
        out_shape=jax.ShapeDtypeStruct((b, l, width), BF16),
        compiler_params=_params("parallel"),
        name="mem_attn_decode",
    )(q, mk_rows, mv_rows)


def _sink_softmax(s, sink):
    m = jnp.maximum(jnp.max(s, axis=-1, keepdims=True), sink)
    p = jnp.exp(s - m)
    return p * (1.0 / (jnp.sum(p, axis=-1, keepdims=True) + jnp.exp(sink - m)))


def _swa_prompt_kernel(slopes, sink_ref, q_ref, kp_ref, kc_ref, vp_ref, vc_ref, o_ref):
    w = WINDOW
    blk = pl.program_id(1)
    qi = lax.broadcasted_iota(jnp.int32, (w, 2 * w), 0)
    kj = lax.broadcasted_iota(jnp.int32, (w, 2 * w), 1)
    dist = w + qi - kj
    valid = (dist >= 0) & (dist < WINDOW) & ((blk - 1) * w + kj >= 0)
    distf = dist.astype(F32)
    scale = HEAD_DIM ** -0.5
    group = len(slopes) // N_KV_HEADS
    for hk in range(N_KV_HEADS):
        kcols = slice(hk * HEAD_DIM, (hk + 1) * HEAD_DIM)
        kcat = jnp.concatenate([kp_ref[:, kcols], kc_ref[:, kcols]], axis=0).astype(BF16)
        vcat = jnp.concatenate([vp_ref[:, kcols], vc_ref[:, kcols]], axis=0).astype(BF16)
        for g in range(group):
            h = hk * group + g
            cols = slice(h * HEAD_DIM, (h + 1) * HEAD_DIM)
            s = lax.dot_general(q_ref[:, cols], kcat, (((1,), (1,)), ((), ())),
                                preferred_element_type=F32) * scale
            s = s - float(slopes[h]) * distf
            s = jnp.where(valid, s, -jnp.inf)
            p = _sink_softmax(s, sink_ref[h])
            o_ref[:, cols] = jnp.dot(p.astype(BF16), vcat, preferred_element_type=F32).astype(o_ref.dtype)


def _swa_prompt(q, kv, sinks, slopes, out_cols):
    b, l, _ = q.shape
    w = WINDOW
    kvw = N_KV_HEADS * HEAD_DIM
    tw = len(slopes) * HEAD_DIM
    assert l % w == 0
    prev = lambda i, j: jnp.maximum(j - 1, 0)
    return pl.pallas_call(
        functools.partial(_swa_prompt_kernel, slopes),
        grid=(b, l // w),
        in_specs=[pl.BlockSpec(memory_space=pltpu.SMEM),
                  pl.BlockSpec((None, w, tw), lambda i, j: (i, j, 0)),
                  pl.BlockSpec((None, w, kvw), lambda i, j: (i, prev(i, j), 0)),
                  pl.BlockSpec((None, w, kvw), lambda i, j: (i, j, 0)),
                  pl.BlockSpec((None, w, kvw), lambda i, j: (i, prev(i, j), 1)),
                  pl.BlockSpec((None, w, kvw), lambda i, j: (i, j, 1))],
        out_specs=pl.BlockSpec((None, w, tw), lambda i, j: (i, j, 0)),
        out_shape=jax.ShapeDtypeStruct((b, l, out_cols), BF16),
        compiler_params=_params("parallel", "parallel"),
        name="swa_prompt",
    )(sinks, q, kv, kv, kv, kv)


def _swa_sample_kernel(slopes, sink_ref, q_ref, k_ref, v_ref, o_ref):
    n_heads = len(slopes)
    group = n_heads // N_KV_HEADS
    wb = k_ref.shape[0] // N_KV_HEADS
    q = q_ref[...].astype(BF16)

    def kv_head(ref, hk):
        return ref[pl.ds(hk, wb, stride=N_KV_HEADS), :].astype(BF16)

    head = lax.broadcasted_iota(jnp.int32, (n_heads, wb), 0)
    pos = lax.broadcasted_iota(jnp.int32, (n_heads, wb), 1)
    distf = (wb - 1 - pos).astype(F32)
    slope = jnp.zeros((n_heads, wb), F32)
    sink = jnp.zeros((n_heads, 1), F32)
    head_col = lax.broadcasted_iota(jnp.int32, (n_heads, 1), 0)
    for h in range(n_heads):
        slope = jnp.where(head == h, float(slopes[h]), slope)
        sink = jnp.where(head_col == h, sink_ref[h], sink)
    scale = HEAD_DIM ** -0.5
    s = jnp.zeros((n_heads, wb), F32)
    for hk in range(N_KV_HEADS):
        s_hk = lax.dot_general(q, kv_head(k_ref, hk), (((1,), (1,)), ((), ())), preferred_element_type=F32)
        s = jnp.where(head // group == hk, s_hk, s)
    s = s * scale - slope * distf
    p = _sink_softmax(s, sink).astype(BF16)
    o = jnp.zeros((n_heads, HEAD_DIM), F32)
    head_o = lax.broadcasted_iota(jnp.int32, (n_heads, HEAD_DIM), 0)
    for hk in range(N_KV_HEADS):
        o_hk = jnp.dot(p, kv_head(v_ref, hk), preferred_element_type=F32)
        o = jnp.where(head_o // group == hk, o_hk, o)
    o_ref[...] = o.astype(o_ref.dtype)


def _swa_sample(q, win_k, win_v, sinks, slopes):
    b, n_heads, _ = q.shape
    rows = win_k.shape[1] * N_KV_HEADS
    win_k = win_k.reshape(b, rows, HEAD_DIM)
    win_v = win_v.reshape(b, rows, HEAD_DIM)
    return pl.pallas_call(
        functools.partial(_swa_sample_kernel, slopes),
        grid=(b,),
        in_specs=[pl.BlockSpec(memory_space=pltpu.SMEM),
                  pl.BlockSpec((None, n_heads, HEAD_DIM), lambda i: (i, 0, 0)),
                  pl.BlockSpec((None, rows, HEAD_DIM), lambda i: (i, 0, 0)),
                  pl.BlockSpec((None, rows, HEAD_DIM), lambda i: (i, 0, 0))],
        out_specs=pl.BlockSpec((None, n_heads, HEAD_DIM), lambda i: (i, 0, 0)),
        out_shape=jax.ShapeDtypeStruct((b, n_heads, HEAD_DIM), BF16),
        compiler_params=_params("parallel"),
        name="swa_sample",
    )(sinks, q, win_k, win_v)


def _lane_pick(x, lane):
    idx = lax.broadcasted_iota(jnp.int32, x.shape, x.ndim - 1)
    return jnp.sum(jnp.where(idx == lane, x, 0.0), axis=-1, keepdims=True)


def _gates(ba, alog, dtb):
    beta = jax.nn.sigmoid(ba)
    g = -jnp.exp(alog) * _softplus(ba + dtb)
    return beta, g


def _bdot(a, b, contract_b_last, precision=None):
    dims = (((2,), (2 if contract_b_last else 1,)), ((0,), (0,)))
    return lax.dot_general(a, b, dims, preferred_element_type=F32, precision=precision)


def _delta_prompt_kernel(n_heads, hb, q_ref, k_ref, v_ref, z_ref, ba_ref, wq_ref, wk_ref, wv_ref,
                         alog_ref, dtb_ref, gn_ref, o_ref, s_ref, pad_ref, beta_ref, g_ref):
    l = q_ref.shape[0]
    c = DELTA_CHUNK
    n = l // c
    hd = HEAD_DIM
    hstep = pl.program_id(1)

    @pl.when(hstep == 0)
    def _():
        beta_all, g_all = _gates(ba_ref[...], alog_ref[...], dtb_ref[...])
        beta_ref[...] = beta_all
        g_ref[...] = g_all

    pad_ref[0:8, :] = jnp.zeros((8, hd), F32)

    def conv(x_ref, w_ref, cols):
        pad_ref[8:8 + l, :] = x_ref[:, cols]
        base = 8 - (CONV_W - 1)
        out = pad_ref[base:base + l, :] * w_ref[0:1, cols]
        for j in range(1, CONV_W):
            out = out + pad_ref[base + j:base + j + l, :] * w_ref[j:j + 1, cols]
        return _silu(out)

    def l2norm(x):
        return x * lax.rsqrt(jnp.sum(x * x, axis=-1, keepdims=True) + EPS)

    row = lax.broadcasted_iota(jnp.int32, (1, c, c), 1)
    col = lax.broadcasted_iota(jnp.int32, (1, c, c), 2)
    incl = row >= col
    strict = row > col
    eye = row == col

    def lower_left(shift):
        return (((row >> (shift + 1)) == (col >> (shift + 1)))
                & (((row >> shift) & 1) == 1) & (((col >> shift) & 1) == 0))

    def prepare(hh):
        cols = slice(hh * hd, (hh + 1) * hd)
        head = hstep * hb + hh
        q3 = (l2norm(conv(q_ref, wq_ref, cols)) * (hd ** -0.5)).reshape(n, c, hd)
        k3 = l2norm(conv(k_ref, wk_ref, cols)).reshape(n, c, hd)
        v3 = conv(v_ref, wv_ref, cols).reshape(n, c, hd)
        beta = _lane_pick(beta_ref[...], head).reshape(n, c, 1)
        g = _lane_pick(g_ref[...], n_heads + head).reshape(n, c, 1)
        gc_row = jnp.sum(jnp.where(row <= col, jnp.broadcast_to(g, (n, c, c)), 0.0), axis=1, keepdims=True)
        gc_col = jnp.sum(jnp.where(eye, jnp.broadcast_to(gc_row, (n, c, c)), 0.0), axis=2, keepdims=True)
        decay = jnp.where(incl, jnp.exp(jnp.where(incl, gc_col - gc_row, 0.0)), 0.0)

        kb = k3.astype(BF16)
        qkk = _bdot(jnp.concatenate([q3, k3], axis=1).astype(BF16), kb, True)
        qk = qkk[:, :c] * decay
        lmat = jnp.where(strict, qkk[:, c:] * decay * beta, 0.0)
        tinv = jnp.where(eye, 1.0, 0.0) - jnp.where(lower_left(0), lmat, 0.0)
        for shift in range(1, int(math.log2(c))):
            cs = jnp.where(lower_left(shift), lmat, 0.0).astype(BF16)
            tb = tinv.astype(BF16)
            tinv = tinv - _bdot(_bdot(tb, cs, False).astype(BF16), tb, False)
        egc = jnp.exp(gc_col)
        rhs = jnp.concatenate([v3 * beta, k3 * (beta * egc)], axis=2).astype(BF16)
        uw = _bdot(tinv.astype(BF16), rhs, False)
        g_last = gc_col[:, c - 1:c, :]
        return dict(
            u=uw[:, :, :hd],
            wq=jnp.concatenate([uw[:, :, hd:], q3 * egc], axis=1).astype(BF16),
            qk=qk.astype(BF16),
            kt=(k3 * jnp.exp(g_last - gc_col)).astype(BF16),
            eg_last=jnp.exp(g_last))

    heads = [prepare(hh) for hh in range(hb)]
    states = [jnp.zeros((hd, hd), F32) for _ in range(hb)]
    for i in range(n):
        for hh, p in enumerate(heads):
            cols = slice(hh * hd, (hh + 1) * hd)
            tok = slice(i * c, (i + 1) * c)
            sb = states[hh].astype(BF16)
            ws = jnp.dot(p["wq"][i], sb, preferred_element_type=F32)
            vnb = (p["u"][i] - ws[:c]).astype(BF16)
            o = ws[c:] + jnp.dot(p["qk"][i], vnb, preferred_element_type=F32)
            states[hh] = states[hh] * p["eg_last"][i] + lax.dot_general(
                p["kt"][i], vnb, (((0,), (0,)), ((), ())), preferred_element_type=F32)
            o = o * lax.rsqrt(jnp.mean(o * o, axis=-1, keepdims=True) + EPS) * gn_ref[...]
            o_ref[tok, cols] = (o * _silu(z_ref[tok, cols])).astype(o_ref.dtype)
    for hh in range(hb):
        s_ref[hh] = states[hh]


def _delta_prompt(proj, ba, w_conv, layer, alog_pad, dtb_pad, gate_norm, n_heads, out_cols):
    b, l, _ = proj.shape
    hd = HEAD_DIM
    tw = n_heads * hd
    hb = DELTA_HEADS_PER_STEP
    assert n_heads % hb == 0
    steps = n_heads // hb
    col = lambda part: pl.BlockSpec((None, l, hb * hd), lambda i, h: (i, 0, part * steps + h))
    wcol = lambda part: pl.BlockSpec((None, CONV_W, hb * hd), lambda i, h: (layer, 0, part * steps + h))
    vec = pl.BlockSpec((1, LANES), lambda i, h: (0, 0))
    return pl.pallas_call(
        functools.partial(_delta_prompt_kernel, n_heads, hb),
        grid=(b, steps),
        in_specs=[col(0), col(1), col(2), col(3),
                  pl.BlockSpec((None, l, LANES), lambda i, h: (i, 0, 0)),
                  wcol(0), wcol(1), wcol(2), vec, vec, vec],
        out_specs=[pl.BlockSpec((None, l, hb * hd), lambda i, h: (i, 0, h)),
                   pl.BlockSpec((None, hb, hd, hd), lambda i, h: (i, h, 0, 0))],
        out_shape=[jax.ShapeDtypeStruct((b, l, out_cols), BF16),
                   jax.ShapeDtypeStruct((b, n_heads, hd, hd), F32)],
        scratch_shapes=[pltpu.VMEM((l + 8, hd), F32), pltpu.VMEM((l, LANES), F32), pltpu.VMEM((l, LANES), F32)],
        compiler_params=_params("parallel", "arbitrary"),
        name="delta_prompt",
    )(proj, proj, proj, proj, ba, w_conv, w_conv, w_conv, alog_pad, dtb_pad, gate_norm)


def _delta_sample_kernel(n_heads, x_ref, z_ref, ba_ref, hist_ref, wc_ref, s_ref, alog_ref, dtb_ref, gn_ref, *refs):
    o_ref, hist_o_ref, s_o_ref = refs[-3:]
    hd = HEAD_DIM
    tw = n_heads * hd
    x = x_ref[...]
    acc = hist_ref[0:1, :] * wc_ref[0:1, :]
    for j in range(1, CONV_W - 1):
        acc = acc + hist_ref[j:j + 1, :] * wc_ref[j:j + 1, :]
    acc = _silu(acc + x * wc_ref[CONV_W - 1:CONV_W, :])
    hist_o_ref[0:CONV_W - 2, :] = hist_ref[1:CONV_W - 1, :]
    hist_o_ref[CONV_W - 2:CONV_W - 1, :] = x

    beta_all, g_all = _gates(ba_ref[...], alog_ref[...], dtb_ref[...])

    def head_columns(off, scale):
        rows = jnp.concatenate([acc[:, off + h * hd: off + (h + 1) * hd] for h in range(n_heads)]
                               + [jnp.zeros((hd - n_heads, hd), F32)], axis=0)
        rows = rows * (lax.rsqrt(jnp.sum(rows * rows, axis=-1, keepdims=True) + EPS) * scale)
        return rows.T

    def per_head(make):
        return jnp.stack([make(h) for h in range(n_heads)], axis=0)

    q_cols = head_columns(0, hd ** -0.5)
    k_cols = head_columns(tw, 1.0)
    qc = per_head(lambda h: q_cols[:, h:h + 1])
    kc = per_head(lambda h: k_cols[:, h:h + 1])
    v = per_head(lambda h: acc[:, 2 * tw + h * hd: 2 * tw + (h + 1) * hd])
    z = per_head(lambda h: z_ref[:, h * hd:(h + 1) * hd])
    beta = per_head(lambda h: _lane_pick(beta_all, h))
    g = per_head(lambda h: _lane_pick(g_all, n_heads + h))
    s = s_ref[...] * jnp.exp(g)
    vn = beta * (v - jnp.sum(s * kc, axis=1, keepdims=True))
    s = s + kc * vn
    s_o_ref[...] = s
    o = jnp.sum(s * qc, axis=1, keepdims=True)
    o = o * lax.rsqrt(jnp.mean(o * o, axis=-1, keepdims=True) + EPS) * gn_ref[...]
    o = (o * _silu(z)).astype(o_ref.dtype)
    for h in range(n_heads):
        o_ref[:, h * hd:(h + 1) * hd] = o[h]


def _delta_sample(proj, ba, state_conv, w_conv, state_delta, layer, alog_pad, dtb_pad, gate_norm, n_heads,
                  new_states=None):
    b = proj.shape[0]
    hd = HEAD_DIM
    tw = n_heads * hd
    hist = CONV_W - 1
    vec = pl.BlockSpec((1, LANES), lambda i: (0, 0))
    in_specs = [pl.BlockSpec((None, 1, 3 * tw), lambda i: (i, 0, 0)),
                pl.BlockSpec((None, 1, tw), lambda i: (i, 0, 3)),
                pl.BlockSpec((None, 1, LANES), lambda i: (i, 0, 0)),
                pl.BlockSpec((None, None, hist, 3 * tw), lambda i: (layer, i, 0, 0)),
                pl.BlockSpec((None, CONV_W, 3 * tw), lambda i: (layer, 0, 0)),
                pl.BlockSpec((None, None, n_heads, hd, hd), lambda i: (layer, i, 0, 0, 0)),
                vec, vec, vec]
    args = [proj, proj, ba, state_conv, w_conv, state_delta, alog_pad, dtb_pad, gate_norm]
    aliases = {}
    if new_states is not None:
        in_specs.append(pl.BlockSpec(memory_space=pl.ANY))
        args.append(new_states)
        aliases = {len(args) - 1: 2}
    return pl.pallas_call(
        functools.partial(_delta_sample_kernel, n_heads),
        grid=(b,),
        in_specs=in_specs,
        out_specs=[pl.BlockSpec((None, 1, tw), lambda i: (i, 0, 0)),
                   pl.BlockSpec((None, hist, 3 * tw), lambda i: (i, 0, 0)),
                   pl.BlockSpec((None, None, n_heads, hd, hd), lambda i: (layer, i, 0, 0, 0))],
        out_shape=[jax.ShapeDtypeStruct((b, 1, tw), BF16),
                   jax.ShapeDtypeStruct((b, hist, 3 * tw), F32),
                   jax.ShapeDtypeStruct(state_delta.shape, F32)],
        input_output_aliases=aliases,
        compiler_params=_params("parallel"),
        name="delta_sample",
    )(*args)


def _trunk(x_p, x_s, mem_kv_p, cache_mk, cache_mv, conv_state, delta_state, buf_k, buf_v, wts):
    bp, lp, d = x_p.shape
    bs, ls, _ = x_s.shape
    assert ls == 1
    depth = wts["w_out"].shape[0]
    n_a = wts["w_in_a_t"].shape[0]
    d_ff = wts["w_down"].shape[1]
    tw = wts["w_conv"].shape[-1] // 3
    n_heads = tw // HEAD_DIM
    mem_w = d - tw
    kvw = N_KV_HEADS * HEAD_DIM
    slopes = _alibi_slopes(n_heads)
    off_b = 4 * tw
    off_qm = off_b + 2 * n_heads

    def lane_pad(vals):
        return jnp.zeros((1, LANES), F32).at[0, n_heads:2 * n_heads].set(vals.astype(F32))

    def resid(xp, fp, xs, fs, w_post, w_next):
        xp, np_ = _resid_norm(xp, fp, w_post, w_next)
        xs, ns_ = _resid_norm(xs, fs, w_post, w_next)
        return xp, xs, np_, ns_

    xp2 = x_p.reshape(bp * lp, d)
    xs2 = x_s.reshape(bs, d)
    (xn_p,) = _norm_cast(xp2, wts["norm_mix_pre"][0:1])
    (xn_s,) = _norm_cast(xs2, wts["norm_mix_pre"][0:1])
    conv_p, conv_s, delta_p = [], [], []
    delta_s = None
    kv_p = win_k = win_v = None
    for layer in range(depth):
        if layer < n_a:
            wt = wts["w_in_a_t"]
            proj_p, proj_s = _matmul(xn_p, wt, tail=xn_s, layer=layer, n_cols=off_b, w_is_nk=True)
            w_ba = jnp.pad(wt[layer, off_b:off_qm, :], ((0, LANES - 2 * n_heads), (0, 0)))
            ba_p, ba_s = _matmul(xn_p, w_ba, tail=xn_s, w_is_nk=True)
            qm_p, qm_s = _matmul(xn_p, wt[layer, off_qm:, :], tail=xn_s, out_dtype=BF16, w_is_nk=True)
            alog_pad = lane_pad(wts["a_log"][layer])
            dtb_pad = lane_pad(wts["dt_bias"][layer])
            gate_norm = wts["w_gate_norm"][layer].reshape(1, HEAD_DIM)
            proj_p = proj_p.reshape(bp, lp, off_b)
            tok_p, s_p = _delta_prompt(proj_p, ba_p.reshape(bp, lp, LANES), wts["w_conv"], layer,
                                       alog_pad, dtb_pad, gate_norm, n_heads, d)
            conv_p.append(proj_p[:, lp - (CONV_W - 1):, :3 * tw])
            tok_s, hist_s, delta_s = _delta_sample(proj_s.reshape(bs, 1, off_b), ba_s.reshape(bs, 1, LANES),
                                                   conv_state, wts["w_conv"], delta_state, layer, alog_pad, dtb_pad,
                                                   gate_norm, n_heads, new_states=delta_s)
            conv_s.append(hist_s)
            delta_p.append(s_p)
            mixed_p = _mem_attn(qm_p.reshape(bp, lp, mem_w), 0, mem_w, mem_kv_p, mem_kv_p, layer, 0, 1, dest=tok_p)
            mo_s = _mem_attn_decode(qm_s.reshape(bs, 1, mem_w), cache_mk, cache_mv, layer)
        else:
            lb = layer - n_a
            sinks = wts["sinks"][lb].astype(F32)
            proj_p, proj_s = _matmul(xn_p, wts["w_in_b"], tail=xn_s, layer=lb, out_dtype=BF16)
            proj_p = proj_p.reshape(bp, lp, d)
            tok_p = _swa_prompt(proj_p, kv_p, sinks, slopes, d)
            mixed_p = _mem_attn(proj_p, tw // mem_w, mem_w, mem_kv_p, mem_kv_p, layer, 0, 1, dest=tok_p)
            q_s = proj_s[:, :tw].reshape(bs, n_heads, HEAD_DIM)
            tok_s = _swa_sample(q_s, win_k, win_v, sinks, slopes).reshape(bs, 1, tw)
            mo_s = _mem_attn_decode(proj_s[:, tw:].reshape(bs, 1, mem_w), cache_mk, cache_mv, layer)
        mixed_p = mixed_p.reshape(bp * lp, d)
        mixed_s = jnp.concatenate([tok_s, mo_s], axis=-1).reshape(bs, d)
        mix_p, mix_s = _matmul(mixed_p, wts["w_out"], tail=mixed_s, layer=layer)
        xp2, xs2, (hn_p,), (hn_s,) = resid(xp2, mix_p, xs2, mix_s, wts["norm_mix_post"][layer],
                                           wts["norm_ffn_pre"][layer:layer + 1])
        hid_p, hid_s, w_down_bf16 = _swiglu_matmul(hn_p, hn_s, wts["w_gate_up"], wts["w_down"], layer)
        f_p = _matmul_bf16w(hid_p, w_down_bf16)
        f_s = _matmul_bf16w(hid_s, w_down_bf16)
        w_post = wts["norm_ffn_post"][layer]
        if layer + 1 == depth:
            xp2, xs2, _, _ = resid(xp2, f_p, xs2, f_s, w_post, None)
        elif layer + 1 == n_a:
            w_next = jnp.stack([wts["norm_mix_pre"][layer + 1], wts["norm_kv"]])
            xp2, xs2, (xn_p, xkv_p), (xn_s, xkv_s) = resid(xp2, f_p, xs2, f_s, w_post, w_next)
            kv_p, kv_s = _matmul(xkv_p, wts["w_kv"], tail=xkv_s)
            kv_p = kv_p.reshape(bp, lp, 2 * kvw)
            new_row = lambda t: t.reshape(bs, 1, N_KV_HEADS, HEAD_DIM)
            win_k = jnp.concatenate([buf_k[:, 1:], new_row(kv_s[:, :kvw])], axis=1)
            win_v = jnp.concatenate([buf_v[:, 1:], new_row(kv_s[:, kvw:])], axis=1)
        else:
            xp2, xs2, (xn_p,), (xn_s,) = resid(xp2, f_p, xs2, f_s, w_post,
                                               wts["norm_mix_pre"][layer + 1:layer + 2])
    wp = min(WINDOW, lp)
    heads = lambda t: t.reshape(t.shape[0], -1, N_KV_HEADS, HEAD_DIM)
    return (xp2.reshape(bp, lp, d), xs2.reshape(bs, 1, d),
            heads(kv_p[:, lp - wp:, :kvw]), heads(kv_p[:, lp - wp:, kvw:]), jnp.stack(conv_p), jnp.stack(delta_p),
            heads(win_k), heads(win_v), jnp.stack(conv_s), delta_s)


def kernel(x_prompt, x_sample, cache_mem_k, cache_mem_v, cache_swa_k, cache_swa_v, state_conv, state_delta, mem_prompt, w_in_a, w_conv, a_log, dt_bias, w_gate_norm, w_in_b, sinks, norm_kv, w_kv, norm_mem, w_mem_kv, w_out, norm_mix_pre, norm_mix_post, norm_ffn_pre, norm_ffn_post, w_gate_up, w_down):
    wts = dict(w_in_a_t=jnp.swapaxes(w_in_a, 1, 2), w_conv=w_conv, a_log=a_log, dt_bias=dt_bias,
               w_gate_norm=w_gate_norm, w_in_b=w_in_b, sinks=sinks, norm_kv=norm_kv, w_kv=w_kv, w_out=w_out,
               norm_mix_pre=norm_mix_pre, norm_mix_post=norm_mix_post, norm_ffn_pre=norm_ffn_pre,
               norm_ffn_post=norm_ffn_post, w_gate_up=w_gate_up, w_down=w_down)
    depth = w_out.shape[0]
    bp, n_mem, d = mem_prompt.shape
    bs = x_sample.shape[0]
    mem_w = w_mem_kv.shape[-1] // 2
    mem_shape = (depth, bp, n_mem, N_MEM_HEADS, mem_w // N_MEM_HEADS)

    memn = _norm_cast(mem_prompt.reshape(bp * n_mem, d), norm_mem)
    mem_kv = jnp.stack([_matmul(memn[i], w_mem_kv, layer=i) for i in range(depth)])
    mem_kv = mem_kv.reshape(depth, bp, n_mem, 2 * mem_w)
    mem_k_p = mem_kv[..., :mem_w].reshape(mem_shape)
    mem_v_p = mem_kv[..., mem_w:].reshape(mem_shape)
    cmk = _tile_rows(cache_mem_k)
    cmv = _tile_rows(cache_mem_v)
    y_p, y_s, swk_p, swv_p, conv_p, delta_p, swk_s, swv_s, conv_s, delta_s = _trunk(
        x_prompt, x_sample, mem_kv, cmk, cmv, state_conv, state_delta, cache_swa_k, cache_swa_v, wts)
    return (y_p, y_s, mem_k_p, mem_v_p, swk_p, swv_p, conv_p, delta_p, swk_s, swv_s, conv_s, delta_s)
```

```python
import functools
import math

import numpy as np
import jax
import jax.numpy as jnp
from jax import lax
from jax.experimental import pallas as pl
from jax.experimental.pallas import tpu as pltpu

F32 = jnp.float32
BF16 = jnp.bfloat16
EPS = 1e-6

HEAD_DIM = 128
N_MEM_HEADS = 4
N_KV_HEADS = 8
WINDOW = 128
CONV_W = 4
DELTA_CHUNK = 128
DELTA_HEADS_PER_STEP = 2
LANES = 128
VMEM_LIMIT_BYTES = 56 * 1024 * 1024


def _params(*sem):
    return pltpu.CompilerParams(dimension_semantics=sem, vmem_limit_bytes=VMEM_LIMIT_BYTES)


def _alibi_slopes(n):
    def pow2_slopes(m):
        start = 2.0 ** (-8.0 / m)
        return [start ** (i + 1) for i in range(m)]
    c = 2 ** int(math.floor(math.log2(n)))
    s = pow2_slopes(c)
    if c < n:
        s = s + pow2_slopes(2 * c)[0::2][: n - c]
    return np.asarray(s, np.float32)


def _rms_rows(x, w):
    return x * lax.rsqrt(jnp.mean(x * x, axis=-1, keepdims=True) + EPS) * w


def _silu(x):
    return x * jax.nn.sigmoid(x)


def _softplus(x):
    return jnp.maximum(x, 0.0) + jnp.log1p(jnp.exp(-jnp.abs(x)))


def _row_block(m, target):
    b = min(m, target)
    assert m % b == 0, (m, b)
    return b


def _norm_kernel(x_ref, w_ref, *o_refs):
    x = x_ref[...]
    for j, o_ref in enumerate(o_refs):
        o_ref[...] = _rms_rows(x, w_ref[j:j + 1, :]).astype(o_ref.dtype)


def _norm_cast(x, ws):
    m, d = x.shape
    n = ws.shape[0]
    br = _row_block(m, 256)
    return pl.pallas_call(
        _norm_kernel,
        grid=(m // br,),
        in_specs=[pl.BlockSpec((br, d), lambda i: (i, 0)),
                  pl.BlockSpec((n, d), lambda i: (0, 0))],
        out_specs=[pl.BlockSpec((br, d), lambda i: (i, 0))] * n,
        out_shape=[jax.ShapeDtypeStruct((m, d), BF16)] * n,
        compiler_params=_params("parallel"),
        name="norm_cast",
    )(x, ws)


def _resid_kernel(n_next, x_ref, f_ref, wpost_ref, *refs):
    if n_next:
        wnext_ref, xo_ref = refs[0], refs[1]
        xn_refs = refs[2:]
    else:
        xo_ref = refs[0]
        xn_refs = ()
    xnew = x_ref[...] + _rms_rows(f_ref[...].astype(F32), wpost_ref[...])
    xo_ref[...] = xnew
    for j, o_ref in enumerate(xn_refs):
        o_ref[...] = _rms_rows(xnew, wnext_ref[j:j + 1, :]).astype(o_ref.dtype)


def _resid_norm(x, f, w_post, w_next):
    m, d = x.shape
    n_next = 0 if w_next is None else w_next.shape[0]
    br = _row_block(m, 256)
    row = pl.BlockSpec((br, d), lambda i: (i, 0))
    in_specs = [row, row, pl.BlockSpec((1, d), lambda i: (0, 0))]
    args = [x, f, w_post.reshape(1, d)]
    if n_next:
        in_specs.append(pl.BlockSpec((n_next, d), lambda i: (0, 0)))
        args.append(w_next)
    outs = pl.pallas_call(
        functools.partial(_resid_kernel, n_next),
        grid=(m // br,),
        in_specs=in_specs,
        out_specs=[row] * (1 + n_next),
        out_shape=[jax.ShapeDtypeStruct((m, d), F32)] + [jax.ShapeDtypeStruct((m, d), BF16)] * n_next,
        compiler_params=_params("parallel"),
        name="resid_norm",
    )(*args)
    return outs[0], list(outs[1:])


def _w_spec(w, layer, k, bn, col_block, w_is_nk=False):
    shape = (bn, k) if w_is_nk else (k, bn)
    pos = (lambda n, m: (col_block(n, m), 0)) if w_is_nk else (lambda n, m: (0, col_block(n, m)))
    if w.ndim == 3:
        return pl.BlockSpec((None,) + shape, lambda n, m: (layer,) + pos(n, m))
    return pl.BlockSpec(shape, lambda n, m: pos(n, m))


def _staged_tile(n_tiles, m_steps, first=0):
    if m_steps == 1:
        return lambda n, m: n + first
    return lambda n, m: first + jnp.minimum(n + jnp.minimum(m, 1), n_tiles - 1)


def _stage_slots(m_steps):
    return 1 if m_steps == 1 else 2


def _prime_weights(m_steps, pairs):
    if m_steps > 1:
        @pl.when((pl.program_id(0) == 0) & (pl.program_id(1) == 0))
        def _():
            for w_ref, wb_ref in pairs:
                wb_ref[0] = w_ref[...].astype(BF16)


def _stage_weights(m_steps, pairs):
    n, m = pl.program_id(0), pl.program_id(1)
    if m_steps == 1:
        for w_ref, wb_ref in pairs:
            wb_ref[0] = w_ref[...].astype(BF16)
        return 0
    j = (m + m_steps - 1) % m_steps
    dst = (n + jnp.minimum(m, 1)) % 2
    for w_ref, wb_ref in pairs:
        slab = w_ref.shape[0] // m_steps
        assert slab * m_steps == w_ref.shape[0] and slab % 16 == 0, (w_ref.shape, m_steps)
        rows = pl.ds(pl.multiple_of(j * slab, slab), slab)
        wb_ref[dst, rows, :] = w_ref[rows, :].astype(BF16)
    return n % 2


def _rows_with_tail(a_ref, tail_ref):
    return jnp.concatenate([a_ref[...], tail_ref[...]], axis=0)


def _mm_kernel(has_tail, w_is_nk, m_steps, *refs):
    if has_tail:
        a_ref, tail_ref, w_ref, o_ref, otail_ref, wb_ref = refs
    else:
        a_ref, w_ref, o_ref, wb_ref = refs
    step = pl.program_id(1)
    pairs = [(w_ref, wb_ref)]
    _prime_weights(m_steps, pairs)

    def mm(rows):
        slot = _stage_weights(m_steps, pairs)
        if w_is_nk:
            return lax.dot_general(rows, wb_ref[slot], (((1,), (1,)), ((), ())), preferred_element_type=F32)
        return jnp.dot(rows, wb_ref[slot], preferred_element_type=F32)

    if not has_tail:
        o_ref[...] = mm(a_ref[...]).astype(o_ref.dtype)
        return
    last = pl.num_programs(1) - 1

    @pl.when(step < last)
    def _():
        o_ref[...] = mm(a_ref[...]).astype(o_ref.dtype)

    @pl.when(step == last)
    def _():
        bm = a_ref.shape[0]
        r = mm(_rows_with_tail(a_ref, tail_ref))
        o_ref[...] = r[:bm].astype(o_ref.dtype)
        otail_ref[...] = r[bm:].astype(otail_ref.dtype)


def _matmul(a, w, *, tail=None, layer=0, col_off=0, n_cols=None, out_dtype=F32, bn=512, bm=1024, w_is_nk=False):
    m, k = a.shape
    n_total = w.shape[-2] if w_is_nk else w.shape[-1]
    n_cols = n_total - col_off if n_cols is None else n_cols
    bn = min(bn, n_cols)
    assert n_cols % bn == 0 and col_off % bn == 0, (n_cols, col_off, bn)
    bm = _row_block(m, bm)
    off = col_off // bn
    in_specs = [pl.BlockSpec((bm, k), lambda n, i: (i, 0))]
    out_specs = [pl.BlockSpec((bm, bn), lambda n, i: (i, n))]
    out_shape = [jax.ShapeDtypeStruct((m, n_cols), out_dtype)]
    args = [a]
    if tail is not None:
        mt = tail.shape[0]
        in_specs.append(pl.BlockSpec((mt, k), lambda n, i: (0, 0)))
        out_specs.append(pl.BlockSpec((mt, bn), lambda n, i: (0, n)))
        out_shape.append(jax.ShapeDtypeStruct((mt, n_cols), out_dtype))
        args.append(tail)
    n_tiles, m_steps = n_cols // bn, m // bm
    in_specs.append(_w_spec(w, layer, k, bn, _staged_tile(n_tiles, m_steps, off), w_is_nk))
    outs = pl.pallas_call(
        functools.partial(_mm_kernel, tail is not None, w_is_nk, m_steps),
        grid=(n_tiles, m_steps),
        in_specs=in_specs,
        out_specs=out_specs,
        out_shape=out_shape,
        scratch_shapes=[pltpu.VMEM((_stage_slots(m_steps),) + ((bn, k) if w_is_nk else (k, bn)), BF16)],
        compiler_params=_params("arbitrary", "arbitrary"),
        name="matmul",
    )(*args, w)
    return outs[0] if tail is None else tuple(outs)


def _swiglu_kernel(m_steps, a_ref, tail_ref, wg_ref, wu_ref, wd_ref, o_ref, otail_ref, wdb_ref, wgb_ref, wub_ref):
    step = pl.program_id(1)
    last = pl.num_programs(1) - 1
    pairs = [(wg_ref, wgb_ref), (wu_ref, wub_ref)]
    _prime_weights(m_steps, pairs)

    def gated(rows):
        wdb_ref[...] = wd_ref[...].astype(BF16)
        slot = _stage_weights(m_steps, pairs)
        g = jnp.dot(rows, wgb_ref[slot], preferred_element_type=F32)
        u = jnp.dot(rows, wub_ref[slot], preferred_element_type=F32)
        return _silu(g) * u

    @pl.when(step < last)
    def _():
        o_ref[...] = gated(a_ref[...]).astype(o_ref.dtype)

    @pl.when(step == last)
    def _():
        bm = a_ref.shape[0]
        r = gated(_rows_with_tail(a_ref, tail_ref))
        o_ref[...] = r[:bm].astype(o_ref.dtype)
        otail_ref[...] = r[bm:].astype(otail_ref.dtype)


def _swiglu_matmul(a, tail, w, w_down, layer, *, bn=256, bm=1024):
    m, k = a.shape
    mt = tail.shape[0]
    d_ff, n_down = w_down.shape[1:]
    assert d_ff % bn == 0
    bm = _row_block(m, bm)
    nb = d_ff // bn
    m_steps = m // bm
    slab = d_ff // (nb * m_steps)
    assert slab * nb * m_steps == d_ff and slab % 16 == 0, (d_ff, nb, m_steps)
    slots = _stage_slots(m_steps)
    return pl.pallas_call(
        functools.partial(_swiglu_kernel, m_steps),
        grid=(nb, m_steps),
        in_specs=[pl.BlockSpec((bm, k), lambda n, i: (i, 0)),
                  pl.BlockSpec((mt, k), lambda n, i: (0, 0)),
                  _w_spec(w, layer, k, bn, _staged_tile(nb, m_steps)),
                  _w_spec(w, layer, k, bn, _staged_tile(nb, m_steps, nb)),
                  pl.BlockSpec((None, slab, n_down), lambda n, i: (layer, n * m_steps + i, 0))],
        out_specs=[pl.BlockSpec((bm, bn), lambda n, i: (i, n)),
                   pl.BlockSpec((mt, bn), lambda n, i: (0, n)),
                   pl.BlockSpec((slab, n_down), lambda n, i: (n * m_steps + i, 0))],
        out_shape=[jax.ShapeDtypeStruct((m, d_ff), BF16), jax.ShapeDtypeStruct((mt, d_ff), BF16),
                   jax.ShapeDtypeStruct((d_ff, n_down), BF16)],
        scratch_shapes=[pltpu.VMEM((slots, k, bn), BF16), pltpu.VMEM((slots, k, bn), BF16)],
        compiler_params=_params("arbitrary", "arbitrary"),
        name="swiglu_matmul",
    )(a, tail, w, w, w_down)


def _mm_bf16w_kernel(a_ref, w_ref, o_ref):
    o_ref[...] = jnp.dot(a_ref[...], w_ref[...], preferred_element_type=F32).astype(o_ref.dtype)


def _matmul_bf16w(a, w, *, bn=512, bm=512):
    m, k = a.shape
    n = w.shape[-1]
    assert n % bn == 0
    bm = _row_block(m, bm)
    return pl.pallas_call(
        _mm_bf16w_kernel,
        grid=(n // bn, m // bm),
        in_specs=[pl.BlockSpec((bm, k), lambda j, i: (i, 0)),
                  pl.BlockSpec((k, bn), lambda j, i: (0, j))],
        out_specs=pl.BlockSpec((bm, bn), lambda j, i: (i, j)),
        out_shape=jax.ShapeDtypeStruct((m, n), BF16),
        compiler_params=_params("parallel", "parallel"),
        name="matmul_bf16w",
    )(a, w)


def _mem_attn_kernel(head_dim, q_ref, k_ref, v_ref, *refs):
    o_ref = refs[-1]
    scale = head_dim ** -0.5
    for h in range(N_MEM_HEADS):
        cols = slice(h * head_dim, (h + 1) * head_dim)
        q = q_ref[:, cols]
        k = k_ref[:, cols].astype(BF16)
        v = v_ref[:, cols].astype(BF16)
        s = lax.dot_general(q, k, (((1,), (1,)), ((), ())), preferred_element_type=F32) * scale
        m = jnp.max(s, axis=-1, keepdims=True)
        p = jnp.exp(s - m)
        p = p * (1.0 / jnp.sum(p, axis=-1, keepdims=True))
        o_ref[:, cols] = jnp.dot(p.astype(BF16), v, preferred_element_type=F32).astype(o_ref.dtype)


def _mem_attn(q, q_col, width, mk, mv, layer, k_col, v_col, dest=None):
    b, l, _ = q.shape
    n_mem = mk.shape[2]
    head_dim = width // N_MEM_HEADS
    bl = _row_block(l, 512)
    in_specs = [pl.BlockSpec((None, bl, width), lambda i, j: (i, j, q_col)),
                pl.BlockSpec((None, None, n_mem, width), lambda i, j: (layer, i, 0, k_col)),
                pl.BlockSpec((None, None, n_mem, width), lambda i, j: (layer, i, 0, v_col))]
    args = [q, mk, mv]
    out_cols, out_col, aliases = width, 0, {}
    if dest is not None:
        out_cols = dest.shape[-1]
        out_col = out_cols // width - 1
        in_specs.append(pl.BlockSpec(memory_space=pl.ANY))
        args.append(dest)
        aliases = {3: 0}
    return pl.pallas_call(
        functools.partial(_mem_attn_kernel, head_dim),
        grid=(b, l // bl),
        in_specs=in_specs,
        out_specs=pl.BlockSpec((None, bl, width), lambda i, j: (i, j, out_col)),
        out_shape=jax.ShapeDtypeStruct((b, l, out_cols), BF16),
        input_output_aliases=aliases,
        compiler_params=_params("parallel", "parallel"),
        name="mem_attn",
    )(*args)


def _tile_rows(cache):
    lyr, b, n_mem, h, hd = cache.shape
    t = hd // LANES
    v = cache.reshape(lyr, b, n_mem, h, t, LANES)
    return jnp.swapaxes(v, 3, 4).reshape(lyr, b, n_mem * t * h, LANES)


def _mem_attn_decode_kernel(n_heads, q_ref, k_ref, v_ref, o_ref):
    tiles = q_ref.shape[-1] // (n_heads * LANES)
    head_dim = tiles * LANES
    group = tiles * n_heads
    n_mem = k_ref.shape[0] // group
    scale = head_dim ** -0.5

    def head_tile(ref, h, t):
        return ref[pl.ds(t * n_heads + h, n_mem, stride=group), :].astype(BF16)

    for h in range(n_heads):
        s = jnp.zeros((q_ref.shape[0], n_mem), F32)
        for t in range(tiles):
            q = q_ref[:, h * head_dim + t * LANES:h * head_dim + (t + 1) * LANES]
            s = s + lax.dot_general(q, head_tile(k_ref, h, t), (((1,), (1,)), ((), ())),
                                    preferred_element_type=F32)
        s = s * scale
        m = jnp.max(s, axis=-1, keepdims=True)
        p = jnp.exp(s - m)
        p = (p * (1.0 / jnp.sum(p, axis=-1, keepdims=True))).astype(BF16)
        for t in range(tiles):
            o = jnp.dot(p, head_tile(v_ref, h, t), preferred_element_type=F32)
            o_ref[:, h * head_dim + t * LANES:h * head_dim + (t + 1) * LANES] = o.astype(o_ref.dtype)


def _mem_attn_decode(q, mk_rows, mv_rows, layer):
    b, l, width = q.shape
    rows = mk_rows.shape[2]
    kv_spec = pl.BlockSpec((None, None, rows, LANES), lambda i: (layer, i, 0, 0))
    return pl.pallas_call(
        functools.partial(_mem_attn_decode_kernel, N_MEM_HEADS),
        grid=(b,),
        in_specs=[pl.BlockSpec((None, l, width), lambda i: (i, 0, 0)), kv_spec, kv_spec],
        out_specs=pl.BlockSpec((None, l, width), lambda i: (i, 0, 0)),
        out_shape=jax.ShapeDtypeStruct((b, l, width), BF16),
        compiler_params=_params("parallel"),
        name="mem_attn_decode",
    )(q, mk_rows, mv_rows)


def _sink_softmax(s, sink):
    m = jnp.maximum(jnp.max(s, axis=-1, keepdims=True), sink)
    p = jnp.exp(s - m)
    return p * (1.0 / (jnp.sum(p, axis=-1, keepdims=True) + jnp.exp(sink - m)))


def _swa_prompt_kernel(slopes, sink_ref, q_ref, kp_ref, kc_ref, vp_ref, vc_ref, o_ref):
    w = WINDOW
    blk = pl.program_id(1)
    qi = lax.broadcasted_iota(jnp.int32, (w, 2 * w), 0)
    kj = lax.broadcasted_iota(jnp.int32, (w, 2 * w), 1)
    dist = w + qi - kj
    valid = (dist >= 0) & (dist < WINDOW) & ((blk - 1) * w + kj >= 0)
    distf = dist.astype(F32)
    scale = HEAD_DIM ** -0.5
    group = len(slopes) // N_KV_HEADS
    for hk in range(N_KV_HEADS):
        kcols = slice(hk * HEAD_DIM, (hk + 1) * HEAD_DIM)
        kcat = jnp.concatenate([kp_ref[:, kcols], kc_ref[:, kcols]], axis=0).astype(BF16)
        vcat = jnp.concatenate([vp_ref[:, kcols], vc_ref[:, kcols]], axis=0).astype(BF16)
        for g in range(group):
            h = hk * group + g
            cols = slice(h * HEAD_DIM, (h + 1) * HEAD_DIM)
            s = lax.dot_general(q_ref[:, cols], kcat, (((1,), (1,)), ((), ())),
                                preferred_element_type=F32) * scale
            s = s - float(slopes[h]) * distf
            s = jnp.where(valid, s, -jnp.inf)
            p = _sink_softmax(s, sink_ref[h])
            o_ref[:, cols] = jnp.dot(p.astype(BF16), vcat, preferred_element_type=F32).astype(o_ref.dtype)


def _swa_prompt(q, kv, sinks, slopes, out_cols):
    b, l, _ = q.shape
    w = WINDOW
    kvw = N_KV_HEADS * HEAD_DIM
    tw = len(slopes) * HEAD_DIM
    assert l % w == 0
    prev = lambda i, j: jnp.maximum(j - 1, 0)
    return pl.pallas_call(
        functools.partial(_swa_prompt_kernel, slopes),
        grid=(b, l // w),
        in_specs=[pl.BlockSpec(memory_space=pltpu.SMEM),
                  pl.BlockSpec((None, w, tw), lambda i, j: (i, j, 0)),
                  pl.BlockSpec((None, w, kvw), lambda i, j: (i, prev(i, j), 0)),
                  pl.BlockSpec((None, w, kvw), lambda i, j: (i, j, 0)),
                  pl.BlockSpec((None, w, kvw), lambda i, j: (i, prev(i, j), 1)),
                  pl.BlockSpec((None, w, kvw), lambda i, j: (i, j, 1))],
        out_specs=pl.BlockSpec((None, w, tw), lambda i, j: (i, j, 0)),
        out_shape=jax.ShapeDtypeStruct((b, l, out_cols), BF16),
        compiler_params=_params("parallel", "parallel"),
        name="swa_prompt",
    )(sinks, q, kv, kv, kv, kv)


def _swa_sample_kernel(slopes, sink_ref, q_ref, k_ref, v_ref, o_ref):
    n_heads = len(slopes)
    group = n_heads // N_KV_HEADS
    wb = k_ref.shape[0] // N_KV_HEADS
    q = q_ref[...].astype(BF16)

    def kv_head(ref, hk):
        return ref[pl.ds(hk, wb, stride=N_KV_HEADS), :].astype(BF16)

    head = lax.broadcasted_iota(jnp.int32, (n_heads, wb), 0)
    pos = lax.broadcasted_iota(jnp.int32, (n_heads, wb), 1)
    distf = (wb - 1 - pos).astype(F32)
    slope = jnp.zeros((n_heads, wb), F32)
    sink = jnp.zeros((n_heads, 1), F32)
    head_col = lax.broadcasted_iota(jnp.int32, (n_heads, 1), 0)
    for h in range(n_heads):
        slope = jnp.where(head == h, float(slopes[h]), slope)
        sink = jnp.where(head_col == h, sink_ref[h], sink)
    scale = HEAD_DIM ** -0.5
    s = jnp.zeros((n_heads, wb), F32)
    for hk in range(N_KV_HEADS):
        s_hk = lax.dot_general(q, kv_head(k_ref, hk), (((1,), (1,)), ((), ())), preferred_element_type=F32)
        s = jnp.where(head // group == hk, s_hk, s)
    s = s * scale - slope * distf
    p = _sink_softmax(s, sink).astype(BF16)
    o = jnp.zeros((n_heads, HEAD_DIM), F32)
    head_o = lax.broadcasted_iota(jnp.int32, (n_heads, HEAD_DIM), 0)
    for hk in range(N_KV_HEADS):
        o_hk = jnp.dot(p, kv_head(v_ref, hk), preferred_element_type=F32)
        o = jnp.where(head_o // group == hk, o_hk, o)
    o_ref[...] = o.astype(o_ref.dtype)


def _swa_sample(q, win_k, win_v, sinks, slopes):
    b, n_heads, _ = q.shape
    rows = win_k.shape[1] * N_KV_HEADS
    win_k = win_k.reshape(b, rows, HEAD_DIM)
    win_v = win_v.reshape(b, rows, HEAD_DIM)
    return pl.pallas_call(
        functools.partial(_swa_sample_kernel, slopes),
        grid=(b,),
        in_specs=[pl.BlockSpec(memory_space=pltpu.SMEM),
                  pl.BlockSpec((None, n_heads, HEAD_DIM), lambda i: (i, 0, 0)),
                  pl.BlockSpec((None, rows, HEAD_DIM), lambda i: (i, 0, 0)),
                  pl.BlockSpec((None, rows, HEAD_DIM), lambda i: (i, 0, 0))],
        out_specs=pl.BlockSpec((None, n_heads, HEAD_DIM), lambda i: (i, 0, 0)),
        out_shape=jax.ShapeDtypeStruct((b, n_heads, HEAD_DIM), BF16),
        compiler_params=_params("parallel"),
        name="swa_sample",
    )(sinks, q, win_k, win_v)


def _lane_pick(x, lane):
    idx = lax.broadcasted_iota(jnp.int32, x.shape, x.ndim - 1)
    return jnp.sum(jnp.where(idx == lane, x, 0.0), axis=-1, keepdims=True)


def _gates(ba, alog, dtb):
    beta = jax.nn.sigmoid(ba)
    g = -jnp.exp(alog) * _softplus(ba + dtb)
    return beta, g


def _bdot(a, b, contract_b_last, precision=None):
    dims = (((2,), (2 if contract_b_last else 1,)), ((0,), (0,)))
    return lax.dot_general(a, b, dims, preferred_element_type=F32, precision=precision)


def _delta_prompt_kernel(n_heads, hb, q_ref, k_ref, v_ref, z_ref, ba_ref, wq_ref, wk_ref, wv_ref,
                         alog_ref, dtb_ref, gn_ref, o_ref, s_ref, pad_ref, beta_ref, g_ref):
    l = q_ref.shape[0]
    c = DELTA_CHUNK
    n = l // c
    hd = HEAD_DIM
    hstep = pl.program_id(1)

    @pl.when(hstep == 0)
    def _():
        beta_all, g_all = _gates(ba_ref[...], alog_ref[...], dtb_ref[...])
        beta_ref[...] = beta_all
        g_ref[...] = g_all

    pad_ref[0:8, :] = jnp.zeros((8, hd), F32)

    def conv(x_ref, w_ref, cols):
        pad_ref[8:8 + l, :] = x_ref[:, cols]
        base = 8 - (CONV_W - 1)
        out = pad_ref[base:base + l, :] * w_ref[0:1, cols]
        for j in range(1, CONV_W):
            out = out + pad_ref[base + j:base + j + l, :] * w_ref[j:j + 1, cols]
        return _silu(out)

    def l2norm(x):
        return x * lax.rsqrt(jnp.sum(x * x, axis=-1, keepdims=True) + EPS)

    row = lax.broadcasted_iota(jnp.int32, (1, c, c), 1)
    col = lax.broadcasted_iota(jnp.int32, (1, c, c), 2)
    incl = row >= col
    strict = row > col
    eye = row == col

    def lower_left(shift):
        return (((row >> (shift + 1)) == (col >> (shift + 1)))
                & (((row >> shift) & 1) == 1) & (((col >> shift) & 1) == 0))

    def prepare(hh):
        cols = slice(hh * hd, (hh + 1) * hd)
        head = hstep * hb + hh
        q3 = (l2norm(conv(q_ref, wq_ref, cols)) * (hd ** -0.5)).reshape(n, c, hd)
        k3 = l2norm(conv(k_ref, wk_ref, cols)).reshape(n, c, hd)
        v3 = conv(v_ref, wv_ref, cols).reshape(n, c, hd)
        beta = _lane_pick(beta_ref[...], head).reshape(n, c, 1)
        g = _lane_pick(g_ref[...], n_heads + head).reshape(n, c, 1)
        gc_row = jnp.sum(jnp.where(row <= col, jnp.broadcast_to(g, (n, c, c)), 0.0), axis=1, keepdims=True)
        gc_col = jnp.sum(jnp.where(eye, jnp.broadcast_to(gc_row, (n, c, c)), 0.0), axis=2, keepdims=True)
        decay = jnp.where(incl, jnp.exp(jnp.where(incl, gc_col - gc_row, 0.0)), 0.0)

        kb = k3.astype(BF16)
        qkk = _bdot(jnp.concatenate([q3, k3], axis=1).astype(BF16), kb, True)
        qk = qkk[:, :c] * decay
        lmat = jnp.where(strict, qkk[:, c:] * decay * beta, 0.0)
        tinv = jnp.where(eye, 1.0, 0.0) - jnp.where(lower_left(0), lmat, 0.0)
        for shift in range(1, int(math.log2(c))):
            cs = jnp.where(lower_left(shift), lmat, 0.0).astype(BF16)
            tb = tinv.astype(BF16)
            tinv = tinv - _bdot(_bdot(tb, cs, False).astype(BF16), tb, False)
        egc = jnp.exp(gc_col)
        rhs = jnp.concatenate([v3 * beta, k3 * (beta * egc)], axis=2).astype(BF16)
        uw = _bdot(tinv.astype(BF16), rhs, False)
        g_last = gc_col[:, c - 1:c, :]
        return dict(
            u=uw[:, :, :hd],
            wq=jnp.concatenate([uw[:, :, hd:], q3 * egc], axis=1).astype(BF16),
            qk=qk.astype(BF16),
            kt=(k3 * jnp.exp(g_last - gc_col)).astype(BF16),
            eg_last=jnp.exp(g_last))

    heads = [prepare(hh) for hh in range(hb)]
    states = [jnp.zeros((hd, hd), F32) for _ in range(hb)]
    for i in range(n):
        for hh, p in enumerate(heads):
            cols = slice(hh * hd, (hh + 1) * hd)
            tok = slice(i * c, (i + 1) * c)
            sb = states[hh].astype(BF16)
            ws = jnp.dot(p["wq"][i], sb, preferred_element_type=F32)
            vnb = (p["u"][i] - ws[:c]).astype(BF16)
            o = ws[c:] + jnp.dot(p["qk"][i], vnb, preferred_element_type=F32)
            states[hh] = states[hh] * p["eg_last"][i] + lax.dot_general(
                p["kt"][i], vnb, (((0,), (0,)), ((), ())), preferred_element_type=F32)
            o = o * lax.rsqrt(jnp.mean(o * o, axis=-1, keepdims=True) + EPS) * gn_ref[...]
            o_ref[tok, cols] = (o * _silu(z_ref[tok, cols])).astype(o_ref.dtype)
    for hh in range(hb):
        s_ref[hh] = states[hh]


def _delta_prompt(proj, ba, w_conv, layer, alog_pad, dtb_pad, gate_norm, n_heads, out_cols):
    b, l, _ = proj.shape
    hd = HEAD_DIM
    tw = n_heads * hd
    hb = DELTA_HEADS_PER_STEP
    assert n_heads % hb == 0
    steps = n_heads // hb
    col = lambda part: pl.BlockSpec((None, l, hb * hd), lambda i, h: (i, 0, part * steps + h))
    wcol = lambda part: pl.BlockSpec((None, CONV_W, hb * hd), lambda i, h: (layer, 0, part * steps + h))
    vec = pl.BlockSpec((1, LANES), lambda i, h: (0, 0))
    return pl.pallas_call(
        functools.partial(_delta_prompt_kernel, n_heads, hb),
        grid=(b, steps),
        in_specs=[col(0), col(1), col(2), col(3),
                  pl.BlockSpec((None, l, LANES), lambda i, h: (i, 0, 0)),
                  wcol(0), wcol(1), wcol(2), vec, vec, vec],
        out_specs=[pl.BlockSpec((None, l, hb * hd), lambda i, h: (i, 0, h)),
                   pl.BlockSpec((None, hb, hd, hd), lambda i, h: (i, h, 0, 0))],
        out_shape=[jax.ShapeDtypeStruct((b, l, out_cols), BF16),
                   jax.ShapeDtypeStruct((b, n_heads, hd, hd), F32)],
        scratch_shapes=[pltpu.VMEM((l + 8, hd), F32), pltpu.VMEM((l, LANES), F32), pltpu.VMEM((l, LANES), F32)],
        compiler_params=_params("parallel", "arbitrary"),
        name="delta_prompt",
    )(proj, proj, proj, proj, ba, w_conv, w_conv, w_conv, alog_pad, dtb_pad, gate_norm)


def _delta_sample_kernel(n_heads, x_ref, z_ref, ba_ref, hist_ref, wc_ref, s_ref, alog_ref, dtb_ref, gn_ref, *refs):
    o_ref, hist_o_ref, s_o_ref = refs[-3:]
    hd = HEAD_DIM
    tw = n_heads * hd
    x = x_ref[...]
    acc = hist_ref[0:1, :] * wc_ref[0:1, :]
    for j in range(1, CONV_W - 1):
        acc = acc + hist_ref[j:j + 1, :] * wc_ref[j:j + 1, :]
    acc = _silu(acc + x * wc_ref[CONV_W - 1:CONV_W, :])
    hist_o_ref[0:CONV_W - 2, :] = hist_ref[1:CONV_W - 1, :]
    hist_o_ref[CONV_W - 2:CONV_W - 1, :] = x

    beta_all, g_all = _gates(ba_ref[...], alog_ref[...], dtb_ref[...])

    def head_columns(off, scale):
        rows = jnp.concatenate([acc[:, off + h * hd: off + (h + 1) * hd] for h in range(n_heads)]
                               + [jnp.zeros((hd - n_heads, hd), F32)], axis=0)
        rows = rows * (lax.rsqrt(jnp.sum(rows * rows, axis=-1, keepdims=True) + EPS) * scale)
        return rows.T

    def per_head(make):
        return jnp.stack([make(h) for h in range(n_heads)], axis=0)

    q_cols = head_columns(0, hd ** -0.5)
    k_cols = head_columns(tw, 1.0)
    qc = per_head(lambda h: q_cols[:, h:h + 1])
    kc = per_head(lambda h: k_cols[:, h:h + 1])
    v = per_head(lambda h: acc[:, 2 * tw + h * hd: 2 * tw + (h + 1) * hd])
    z = per_head(lambda h: z_ref[:, h * hd:(h + 1) * hd])
    beta = per_head(lambda h: _lane_pick(beta_all, h))
    g = per_head(lambda h: _lane_pick(g_all, n_heads + h))
    s = s_ref[...] * jnp.exp(g)
    vn = beta * (v - jnp.sum(s * kc, axis=1, keepdims=True))
    s = s + kc * vn
    s_o_ref[...] = s
    o = jnp.sum(s * qc, axis=1, keepdims=True)
    o = o * lax.rsqrt(jnp.mean(o * o, axis=-1, keepdims=True) + EPS) * gn_ref[...]
    o = (o * _silu(z)).astype(o_ref.dtype)
    for h in range(n_heads):
        o_ref[:, h * hd:(h + 1) * hd] = o[h]


def _delta_sample(proj, ba, state_conv, w_conv, state_delta, layer, alog_pad, dtb_pad, gate_norm, n_heads,
                  new_states=None):
    b = proj.shape[0]
    hd = HEAD_DIM
    tw = n_heads * hd
    hist = CONV_W - 1
    vec = pl.BlockSpec((1, LANES), lambda i: (0, 0))
    in_specs = [pl.BlockSpec((None, 1, 3 * tw), lambda i: (i, 0, 0)),
                pl.BlockSpec((None, 1, tw), lambda i: (i, 0, 3)),
                pl.BlockSpec((None, 1, LANES), lambda i: (i, 0, 0)),
                pl.BlockSpec((None, None, hist, 3 * tw), lambda i: (layer, i, 0, 0)),
                pl.BlockSpec((None, CONV_W, 3 * tw), lambda i: (layer, 0, 0)),
                pl.BlockSpec((None, None, n_heads, hd, hd), lambda i: (layer, i, 0, 0, 0)),
                vec, vec, vec]
    args = [proj, proj, ba, state_conv, w_conv, state_delta, alog_pad, dtb_pad, gate_norm]
    aliases = {}
    if new_states is not None:
        in_specs.append(pl.BlockSpec(memory_space=pl.ANY))
        args.append(new_states)
        aliases = {len(args) - 1: 2}
    return pl.pallas_call(
        functools.partial(_delta_sample_kernel, n_heads),
        grid=(b,),
        in_specs=in_specs,
        out_specs=[pl.BlockSpec((None, 1, tw), lambda i: (i, 0, 0)),
                   pl.BlockSpec((None, hist, 3 * tw), lambda i: (i, 0, 0)),
                   pl.BlockSpec((None, None, n_heads, hd, hd), lambda i: (layer, i, 0, 0, 0))],
        out_shape=[jax.ShapeDtypeStruct((b, 1, tw), BF16),
                   jax.ShapeDtypeStruct((b, hist, 3 * tw), F32),
                   jax.ShapeDtypeStruct(state_delta.shape, F32)],
        input_output_aliases=aliases,
        compiler_params=_params("parallel"),
        name="delta_sample",
    )(*args)


def _trunk(x_p, x_s, mem_kv_p, cache_mk, cache_mv, conv_state, delta_state, buf_k, buf_v, wts):
    bp, lp, d = x_p.shape
    bs, ls, _ = x_s.shape
    assert ls == 1
    depth = wts["w_out"].shape[0]
    n_a = wts["w_in_a_t"].shape[0]
    d_ff = wts["w_down"].shape[1]
    tw = wts["w_conv"].shape[-1] // 3
    n_heads = tw // HEAD_DIM
    mem_w = d - tw
    kvw = N_KV_HEADS * HEAD_DIM
    slopes = _alibi_slopes(n_heads)
    off_b = 4 * tw
    off_qm = off_b + 2 * n_heads

    def lane_pad(vals):
        return jnp.zeros((1, LANES), F32).at[0, n_heads:2 * n_heads].set(vals.astype(F32))

    def resid(xp, fp, xs, fs, w_post, w_next):
        xp, np_ = _resid_norm(xp, fp, w_post, w_next)
        xs, ns_ = _resid_norm(xs, fs, w_post, w_next)
        return xp, xs, np_, ns_

    xp2 = x_p.reshape(bp * lp, d)
    xs2 = x_s.reshape(bs, d)
    (xn_p,) = _norm_cast(xp2, wts["norm_mix_pre"][0:1])
    (xn_s,) = _norm_cast(xs2, wts["norm_mix_pre"][0:1])
    conv_p, conv_s, delta_p = [], [], []
    delta_s = None
    kv_p = win_k = win_v = None
    for layer in range(depth):
        if layer < n_a:
            wt = wts["w_in_a_t"]
            proj_p, proj_s = _matmul(xn_p, wt, tail=xn_s, layer=layer, n_cols=off_b, w_is_nk=True)
            w_ba = jnp.pad(wt[layer, off_b:off_qm, :], ((0, LANES - 2 * n_heads), (0, 0)))
            ba_p, ba_s = _matmul(xn_p, w_ba, tail=xn_s, w_is_nk=True)
            qm_p, qm_s = _matmul(xn_p, wt[layer, off_qm:, :], tail=xn_s, out_dtype=BF16, w_is_nk=True)
            alog_pad = lane_pad(wts["a_log"][layer])
            dtb_pad = lane_pad(wts["dt_bias"][layer])
            gate_norm = wts["w_gate_norm"][layer].reshape(1, HEAD_DIM)
            proj_p = proj_p.reshape(bp, lp, off_b)
            tok_p, s_p = _delta_prompt(proj_p, ba_p.reshape(bp, lp, LANES), wts["w_conv"], layer,
                                       alog_pad, dtb_pad, gate_norm, n_heads, d)
            conv_p.append(proj_p[:, lp - (CONV_W - 1):, :3 * tw])
            tok_s, hist_s, delta_s = _delta_sample(proj_s.reshape(bs, 1, off_b), ba_s.reshape(bs, 1, LANES),
                                                   conv_state, wts["w_conv"], delta_state, layer, alog_pad, dtb_pad,
                                                   gate_norm, n_heads, new_states=delta_s)
            conv_s.append(hist_s)
            delta_p.append(s_p)
            mixed_p = _mem_attn(qm_p.reshape(bp, lp, mem_w), 0, mem_w, mem_kv_p, mem_kv_p, layer, 0, 1, dest=tok_p)
            mo_s = _mem_attn_decode(qm_s.reshape(bs, 1, mem_w), cache_mk, cache_mv, layer)
        else:
            lb = layer - n_a
            sinks = wts["sinks"][lb].astype(F32)
            proj_p, proj_s = _matmul(xn_p, wts["w_in_b"], tail=xn_s, layer=lb, out_dtype=BF16)
            proj_p = proj_p.reshape(bp, lp, d)
            tok_p = _swa_prompt(proj_p, kv_p, sinks, slopes, d)
            mixed_p = _mem_attn(proj_p, tw // mem_w, mem_w, mem_kv_p, mem_kv_p, layer, 0, 1, dest=tok_p)
            q_s = proj_s[:, :tw].reshape(bs, n_heads, HEAD_DIM)
            tok_s = _swa_sample(q_s, win_k, win_v, sinks, slopes).reshape(bs, 1, tw)
            mo_s = _mem_attn_decode(proj_s[:, tw:].reshape(bs, 1, mem_w), cache_mk, cache_mv, layer)
        mixed_p = mixed_p.reshape(bp * lp, d)
        mixed_s = jnp.concatenate([tok_s, mo_s], axis=-1).reshape(bs, d)
        mix_p, mix_s = _matmul(mixed_p, wts["w_out"], tail=mixed_s, layer=layer, out_dtype=BF16)
        xp2, xs2, (hn_p,), (hn_s,) = resid(xp2, mix_p, xs2, mix_s, wts["norm_mix_post"][layer],
                                           wts["norm_ffn_pre"][layer:layer + 1])
        hid_p, hid_s, w_down_bf16 = _swiglu_matmul(hn_p, hn_s, wts["w_gate_up"], wts["w_down"], layer)
        f_p = _matmul_bf16w(hid_p, w_down_bf16)
        f_s = _matmul_bf16w(hid_s, w_down_bf16)
        w_post = wts["norm_ffn_post"][layer]
        if layer + 1 == depth:
            xp2, xs2, _, _ = resid(xp2, f_p, xs2, f_s, w_post, None)
        elif layer + 1 == n_a:
            w_next = jnp.stack([wts["norm_mix_pre"][layer + 1], wts["norm_kv"]])
            xp2, xs2, (xn_p, xkv_p), (xn_s, xkv_s) = resid(xp2, f_p, xs2, f_s, w_post, w_next)
            kv_p, kv_s = _matmul(xkv_p, wts["w_kv"], tail=xkv_s)
            kv_p = kv_p.reshape(bp, lp, 2 * kvw)
            new_row = lambda t: t.reshape(bs, 1, N_KV_HEADS, HEAD_DIM)
            win_k = jnp.concatenate([buf_k[:, 1:], new_row(kv_s[:, :kvw])], axis=1)
            win_v = jnp.concatenate([buf_v[:, 1:], new_row(kv_s[:, kvw:])], axis=1)
        else:
            xp2, xs2, (xn_p,), (xn_s,) = resid(xp2, f_p, xs2, f_s, w_post,
                                               wts["norm_mix_pre"][layer + 1:layer + 2])
    wp = min(WINDOW, lp)
    heads = lambda t: t.reshape(t.shape[0], -1, N_KV_HEADS, HEAD_DIM)
    return (xp2.reshape(bp, lp, d), xs2.reshape(bs, 1, d),
            heads(kv_p[:, lp - wp:, :kvw]), heads(kv_p[:, lp - wp:, kvw:]), jnp.stack(conv_p), jnp.stack(delta_p),
            heads(win_k), heads(win_v), jnp.stack(conv_s), delta_s)


def kernel(x_prompt, x_sample, cache_mem_k, cache_mem_v, cache_swa_k, cache_swa_v, state_conv, state_delta, mem_prompt, w_in_a, w_conv, a_log, dt_bias, w_gate_norm, w_in_b, sinks, norm_kv, w_kv, norm_mem, w_mem_kv, w_out, norm_mix_pre, norm_mix_post, norm_ffn_pre, norm_ffn_post, w_gate_up, w_down):
    wts = dict(w_in_a_t=jnp.swapaxes(w_in_a, 1, 2), w_conv=w_conv, a_log=a_log, dt_bias=dt_bias,
               w_gate_norm=w_gate_norm, w_in_b=w_in_b, sinks=sinks, norm_kv=norm_kv, w_kv=w_kv, w_out=w_out,
               norm_mix_pre=norm_mix_pre, norm_mix_post=norm_mix_post, norm_ffn_pre=norm_ffn_pre,
               norm_ffn_post=norm_ffn_post, w_gate_up=w_gate_up, w_down=w_down)
    depth = w_out.shape[0]
    bp, n_mem, d = mem_prompt.shape
    bs = x_sample.shape[0]
    mem_w = w_mem_kv.shape[-1] // 2
    mem_shape = (depth, bp, n_mem, N_MEM_HEADS, mem_w // N_MEM_HEADS)

    memn = _norm_cast(mem_prompt.reshape(bp * n_mem, d), norm_mem)
    mem_kv = jnp.stack([_matmul(memn[i], w_mem_kv, layer=i) for i in range(depth)])
    mem_kv = mem_kv.reshape(depth, bp, n_mem, 2 * mem_w)
    mem_k_p = mem_kv[..., :mem_w].reshape(mem_shape)
    mem_v_p = mem_kv[..., mem_w:].reshape(mem_shape)
    cmk = _tile_rows(cache_mem_k)
    cmv = _tile_rows(cache_mem_v)
    y_p, y_s, swk_p, swv_p, conv_p, delta_p, swk_s, swv_s, conv_s, delta_s = _trunk(
        x_prompt, x_sample, mem_kv, cmk, cmv, state_conv, state_delta, cache_swa_k, cache_swa_v, wts)
    return (y_p, y_s, mem_k_p, mem_v_p, swk_p, swv_p, conv_p, delta_p, swk_s, swv_s, conv_s, delta_s)
```

```python
import functools
import math

import numpy as np
import jax
import jax.numpy as jnp
from jax import lax
from jax.experimental import pallas as pl
from jax.experimental.pallas import tpu as pltpu

F32 = jnp.float32
BF16 = jnp.bfloat16
EPS = 1e-6

HEAD_DIM = 128
N_MEM_HEADS = 4
N_KV_HEADS = 8
WINDOW = 128
CONV_W = 4
DELTA_CHUNK = 128
DELTA_HEADS_PER_STEP = 2
LANES = 128
VMEM_LIMIT_BYTES = 56 * 1024 * 1024
WIDE_TILE = 1024


def _params(*sem):
    return pltpu.CompilerParams(dimension_semantics=sem, vmem_limit_bytes=VMEM_LIMIT_BYTES)


def _alibi_slopes(n):
    def pow2_slopes(m):
        start = 2.0 ** (-8.0 / m)
        return [start ** (i + 1) for i in range(m)]
    c = 2 ** int(math.floor(math.log2(n)))
    s = pow2_slopes(c)
    if c < n:
        s = s + pow2_slopes(2 * c)[0::2][: n - c]
    return np.asarray(s, np.float32)


def _rms_rows(x, w):
    return x * lax.rsqrt(jnp.mean(x * x, axis=-1, keepdims=True) + EPS) * w


def _silu(x):
    return x * jax.nn.sigmoid(x)


def _softplus(x):
    return jnp.maximum(x, 0.0) + jnp.log1p(jnp.exp(-jnp.abs(x)))


def _row_block(m, target):
    b = min(m, target)
    assert m % b == 0, (m, b)
    return b


def _norm_kernel(x_ref, w_ref, *o_refs):
    x = x_ref[...]
    for j, o_ref in enumerate(o_refs):
        o_ref[...] = _rms_rows(x, w_ref[j:j + 1, :]).astype(o_ref.dtype)


def _norm_cast(x, ws):
    m, d = x.shape
    n = ws.shape[0]
    br = _row_block(m, 256)
    return pl.pallas_call(
        _norm_kernel,
        grid=(m // br,),
        in_specs=[pl.BlockSpec((br, d), lambda i: (i, 0)),
                  pl.BlockSpec((n, d), lambda i: (0, 0))],
        out_specs=[pl.BlockSpec((br, d), lambda i: (i, 0))] * n,
        out_shape=[jax.ShapeDtypeStruct((m, d), BF16)] * n,
        compiler_params=_params("parallel"),
        name="norm_cast",
    )(x, ws)


def _resid_kernel(n_next, x_ref, f_ref, wpost_ref, *refs):
    if n_next:
        wnext_ref, xo_ref = refs[0], refs[1]
        xn_refs = refs[2:]
    else:
        xo_ref = refs[0]
        xn_refs = ()
    xnew = x_ref[...] + _rms_rows(f_ref[...].astype(F32), wpost_ref[...])
    xo_ref[...] = xnew
    for j, o_ref in enumerate(xn_refs):
        o_ref[...] = _rms_rows(xnew, wnext_ref[j:j + 1, :]).astype(o_ref.dtype)


def _resid_norm(x, f, w_post, w_next):
    m, d = x.shape
    n_next = 0 if w_next is None else w_next.shape[0]
    br = _row_block(m, 256)
    row = pl.BlockSpec((br, d), lambda i: (i, 0))
    in_specs = [row, row, pl.BlockSpec((1, d), lambda i: (0, 0))]
    args = [x, f, w_post.reshape(1, d)]
    if n_next:
        in_specs.append(pl.BlockSpec((n_next, d), lambda i: (0, 0)))
        args.append(w_next)
    outs = pl.pallas_call(
        functools.partial(_resid_kernel, n_next),
        grid=(m // br,),
        in_specs=in_specs,
        out_specs=[row] * (1 + n_next),
        out_shape=[jax.ShapeDtypeStruct((m, d), F32)] + [jax.ShapeDtypeStruct((m, d), BF16)] * n_next,
        compiler_params=_params("parallel"),
        name="resid_norm",
    )(*args)
    return outs[0], list(outs[1:])


def _w_slab_spec(w, layer, k, bn, n_tiles, m_steps, first=0, w_is_nk=False):
    tile = lambda n: first + jnp.minimum(n, n_tiles - 1)
    if w_is_nk:
        shape, pos = (bn // m_steps, k), (lambda n, m: (tile(n) * m_steps + m, 0))
    else:
        shape, pos = (k // m_steps, bn), (lambda n, m: (m, tile(n)))
    assert shape[0] * m_steps == (bn if w_is_nk else k) and shape[0] % 16 == 0, (k, bn, m_steps)
    if w.ndim == 3:
        return pl.BlockSpec((None,) + shape, lambda n, m: (layer,) + pos(n, m))
    return pl.BlockSpec(shape, lambda n, m: pos(n, m))


def _stage_slab(w_ref, wb_ref):
    n, m = pl.program_id(0), pl.program_id(1)
    slab = w_ref.shape[0]
    rows = pl.ds(pl.multiple_of(m * slab, slab), slab)
    wb_ref[n % 2, rows, :] = w_ref[...].astype(BF16)


def _row_step(n, m):
    return jnp.where(n >= 1, m, 0)


def _rows_with_tail(a_ref, tail_ref):
    return jnp.concatenate([a_ref[...], tail_ref[...]], axis=0)


def _mm_kernel(has_tail, w_is_nk, *refs):
    if has_tail:
        a_ref, tail_ref, w_ref, o_ref, otail_ref, wb_ref = refs
    else:
        a_ref, w_ref, o_ref, wb_ref = refs
    n, step = pl.program_id(0), pl.program_id(1)

    def mm(rows):
        _stage_slab(w_ref, wb_ref)
        wb = wb_ref[(n + 1) % 2]
        if w_is_nk:
            return lax.dot_general(rows, wb, (((1,), (1,)), ((), ())), preferred_element_type=F32)
        return jnp.dot(rows, wb, preferred_element_type=F32)

    @pl.when(n == 0)
    def _():
        _stage_slab(w_ref, wb_ref)

    if not has_tail:
        @pl.when(n >= 1)
        def _():
            o_ref[...] = mm(a_ref[...]).astype(o_ref.dtype)
        return
    last = pl.num_programs(1) - 1

    @pl.when((n >= 1) & (step < last))
    def _():
        o_ref[...] = mm(a_ref[...]).astype(o_ref.dtype)

    @pl.when((n >= 1) & (step == last))
    def _():
        bm = a_ref.shape[0]
        r = mm(_rows_with_tail(a_ref, tail_ref))
        o_ref[...] = r[:bm].astype(o_ref.dtype)
        otail_ref[...] = r[bm:].astype(otail_ref.dtype)


def _matmul(a, w, *, tail=None, layer=0, col_off=0, n_cols=None, out_dtype=F32, bn=512, bm=1024, w_is_nk=False):
    m, k = a.shape
    n_total = w.shape[-2] if w_is_nk else w.shape[-1]
    n_cols = n_total - col_off if n_cols is None else n_cols
    bn = min(bn, n_cols)
    assert n_cols % bn == 0 and col_off % bn == 0, (n_cols, col_off, bn)
    bm = _row_block(m, bm)
    off = col_off // bn
    out_tile = lambda n: jnp.maximum(n - 1, 0)
    in_specs = [pl.BlockSpec((bm, k), lambda n, i: (_row_step(n, i), 0))]
    out_specs = [pl.BlockSpec((bm, bn), lambda n, i: (_row_step(n, i), out_tile(n)))]
    out_shape = [jax.ShapeDtypeStruct((m, n_cols), out_dtype)]
    args = [a]
    if tail is not None:
        mt = tail.shape[0]
        in_specs.append(pl.BlockSpec((mt, k), lambda n, i: (0, 0)))
        out_specs.append(pl.BlockSpec((mt, bn), lambda n, i: (0, out_tile(n))))
        out_shape.append(jax.ShapeDtypeStruct((mt, n_cols), out_dtype))
        args.append(tail)
    n_tiles, m_steps = n_cols // bn, m // bm
    in_specs.append(_w_slab_spec(w, layer, k, bn, n_tiles, m_steps, off, w_is_nk))
    outs = pl.pallas_call(
        functools.partial(_mm_kernel, tail is not None, w_is_nk),
        grid=(n_tiles + 1, m_steps),
        in_specs=in_specs,
        out_specs=out_specs,
        out_shape=out_shape,
        scratch_shapes=[pltpu.VMEM((2,) + ((bn, k) if w_is_nk else (k, bn)), BF16)],
        compiler_params=_params("arbitrary", "arbitrary"),
        name="matmul",
    )(*args, w)
    return outs[0] if tail is None else tuple(outs)


def _swiglu_kernel(a_ref, tail_ref, wg_ref, wu_ref, wd_ref, o_ref, otail_ref, wdb_ref, wgb_ref, wub_ref):
    n, step = pl.program_id(0), pl.program_id(1)
    last = pl.num_programs(1) - 1

    def stage():
        _stage_slab(wg_ref, wgb_ref)
        _stage_slab(wu_ref, wub_ref)

    def gated(rows):
        wdb_ref[...] = wd_ref[...].astype(BF16)
        stage()
        slot = (n + 1) % 2
        g = jnp.dot(rows, wgb_ref[slot], preferred_element_type=F32)
        u = jnp.dot(rows, wub_ref[slot], preferred_element_type=F32)
        return _silu(g) * u

    @pl.when(n == 0)
    def _():
        stage()

    @pl.when((n >= 1) & (step < last))
    def _():
        o_ref[...] = gated(a_ref[...]).astype(o_ref.dtype)

    @pl.when((n >= 1) & (step == last))
    def _():
        bm = a_ref.shape[0]
        r = gated(_rows_with_tail(a_ref, tail_ref))
        o_ref[...] = r[:bm].astype(o_ref.dtype)
        otail_ref[...] = r[bm:].astype(otail_ref.dtype)


def _swiglu_matmul(a, tail, w, w_down, layer, *, bn=256, bm=1024):
    m, k = a.shape
    mt = tail.shape[0]
    d_ff, n_down = w_down.shape[1:]
    assert d_ff % bn == 0
    bm = _row_block(m, bm)
    nb = d_ff // bn
    m_steps = m // bm
    slab = d_ff // (nb * m_steps)
    assert slab * nb * m_steps == d_ff and slab % 16 == 0, (d_ff, nb, m_steps)
    out_tile = lambda n: jnp.maximum(n - 1, 0)
    down_slab = lambda n, i: out_tile(n) * m_steps + _row_step(n, i)
    return pl.pallas_call(
        _swiglu_kernel,
        grid=(nb + 1, m_steps),
        in_specs=[pl.BlockSpec((bm, k), lambda n, i: (_row_step(n, i), 0)),
                  pl.BlockSpec((mt, k), lambda n, i: (0, 0)),
                  _w_slab_spec(w, layer, k, bn, nb, m_steps),
                  _w_slab_spec(w, layer, k, bn, nb, m_steps, nb),
                  pl.BlockSpec((None, slab, n_down), lambda n, i: (layer, down_slab(n, i), 0))],
        out_specs=[pl.BlockSpec((bm, bn), lambda n, i: (_row_step(n, i), out_tile(n))),
                   pl.BlockSpec((mt, bn), lambda n, i: (0, out_tile(n))),
                   pl.BlockSpec((slab, n_down), lambda n, i: (down_slab(n, i), 0))],
        out_shape=[jax.ShapeDtypeStruct((m, d_ff), BF16), jax.ShapeDtypeStruct((mt, d_ff), BF16),
                   jax.ShapeDtypeStruct((d_ff, n_down), BF16)],
        scratch_shapes=[pltpu.VMEM((2, k, bn), BF16), pltpu.VMEM((2, k, bn), BF16)],
        compiler_params=_params("arbitrary", "arbitrary"),
        name="swiglu_matmul",
    )(a, tail, w, w, w_down)


def _mm_bf16w_kernel(a_ref, w_ref, o_ref):
    o_ref[...] = jnp.dot(a_ref[...], w_ref[...], preferred_element_type=F32).astype(o_ref.dtype)


def _matmul_bf16w(a, w, *, bn=512, bm=512):
    m, k = a.shape
    n = w.shape[-1]
    assert n % bn == 0
    bm = _row_block(m, bm)
    return pl.pallas_call(
        _mm_bf16w_kernel,
        grid=(n // bn, m // bm),
        in_specs=[pl.BlockSpec((bm, k), lambda j, i: (i, 0)),
                  pl.BlockSpec((k, bn), lambda j, i: (0, j))],
        out_specs=pl.BlockSpec((bm, bn), lambda j, i: (i, j)),
        out_shape=jax.ShapeDtypeStruct((m, n), BF16),
        compiler_params=_params("parallel", "parallel"),
        name="matmul_bf16w",
    )(a, w)


def _mem_attn_kernel(head_dim, q_ref, k_ref, v_ref, *refs):
    o_ref = refs[-1]
    scale = head_dim ** -0.5
    for h in range(N_MEM_HEADS):
        cols = slice(h * head_dim, (h + 1) * head_dim)
        q = q_ref[:, cols]
        k = k_ref[:, cols].astype(BF16)
        v = v_ref[:, cols].astype(BF16)
        s = lax.dot_general(q, k, (((1,), (1,)), ((), ())), preferred_element_type=F32) * scale
        m = jnp.max(s, axis=-1, keepdims=True)
        p = jnp.exp(s - m)
        p = p * (1.0 / jnp.sum(p, axis=-1, keepdims=True))
        o_ref[:, cols] = jnp.dot(p.astype(BF16), v, preferred_element_type=F32).astype(o_ref.dtype)


def _mem_attn(q, q_col, width, mk, mv, layer, k_col, v_col, dest=None):
    b, l, _ = q.shape
    n_mem = mk.shape[2]
    head_dim = width // N_MEM_HEADS
    bl = _row_block(l, 512)
    in_specs = [pl.BlockSpec((None, bl, width), lambda i, j: (i, j, q_col)),
                pl.BlockSpec((None, None, n_mem, width), lambda i, j: (layer, i, 0, k_col)),
                pl.BlockSpec((None, None, n_mem, width), lambda i, j: (layer, i, 0, v_col))]
    args = [q, mk, mv]
    out_cols, out_col, aliases = width, 0, {}
    if dest is not None:
        out_cols = dest.shape[-1]
        out_col = out_cols // width - 1
        in_specs.append(pl.BlockSpec(memory_space=pl.ANY))
        args.append(dest)
        aliases = {3: 0}
    return pl.pallas_call(
        functools.partial(_mem_attn_kernel, head_dim),
        grid=(b, l // bl),
        in_specs=in_specs,
        out_specs=pl.BlockSpec((None, bl, width), lambda i, j: (i, j, out_col)),
        out_shape=jax.ShapeDtypeStruct((b, l, out_cols), BF16),
        input_output_aliases=aliases,
        compiler_params=_params("parallel", "parallel"),
        name="mem_attn",
    )(*args)


def _tile_rows(cache):
    lyr, b, n_mem, h, hd = cache.shape
    t = hd // LANES
    v = cache.reshape(lyr, b, n_mem, h, t, LANES)
    return jnp.swapaxes(v, 3, 4).reshape(lyr, b, n_mem * t * h, LANES)


def _mem_attn_decode_kernel(n_heads, q_ref, k_ref, v_ref, o_ref):
    tiles = q_ref.shape[-1] // (n_heads * LANES)
    head_dim = tiles * LANES
    group = tiles * n_heads
    n_mem = k_ref.shape[0] // group
    scale = head_dim ** -0.5

    def head_tile(ref, h, t):
        return ref[pl.ds(t * n_heads + h, n_mem, stride=group), :].astype(BF16)

    for h in range(n_heads):
        s = jnp.zeros((q_ref.shape[0], n_mem), F32)
        for t in range(tiles):
            q = q_ref[:, h * head_dim + t * LANES:h * head_dim + (t + 1) * LANES]
            s = s + lax.dot_general(q, head_tile(k_ref, h, t), (((1,), (1,)), ((), ())),
                                    preferred_element_type=F32)
        s = s * scale
        m = jnp.max(s, axis=-1, keepdims=True)
        p = jnp.exp(s - m)
        p = (p * (1.0 / jnp.sum(p, axis=-1, keepdims=True))).astype(BF16)
        for t in range(tiles):
            o = jnp.dot(p, head_tile(v_ref, h, t), preferred_element_type=F32)
            o_ref[:, h * head_dim + t * LANES:h * head_dim + (t + 1) * LANES] = o.astype(o_ref.dtype)


def _mem_attn_decode(q, mk_rows, mv_rows, layer):
    b, l, width = q.shape
    rows = mk_rows.shape[2]
    kv_spec = pl.BlockSpec((None, None, rows, LANES), lambda i: (layer, i, 0, 0))
    return pl.pallas_call(
        functools.partial(_mem_attn_decode_kernel, N_MEM_HEADS),
        grid=(b,),
        in_specs=[pl.BlockSpec((None, l, width), lambda i: (i, 0, 0)), kv_spec, kv_spec],
        out_specs=pl.BlockSpec((None, l, width), lambda i: (i, 0, 0)),
        out_shape=jax.ShapeDtypeStruct((b, l, width), BF16),
        compiler_params=_params("parallel"),
        name="mem_attn_decode",
    )(q, mk_rows, mv_rows)


def _sink_softmax(s, sink):
    m = jnp.maximum(jnp.max(s, axis=-1, keepdims=True), sink)
    p = jnp.exp(s - m)
    return p * (1.0 / (jnp.sum(p, axis=-1, keepdims=True) + jnp.exp(sink - m)))


def _swa_prompt_kernel(slopes, sink_ref, q_ref, kp_ref, kc_ref, vp_ref, vc_ref, o_ref):
    w = WINDOW
    blk = pl.program_id(1)
    qi = lax.broadcasted_iota(jnp.int32, (w, 2 * w), 0)
    kj = lax.broadcasted_iota(jnp.int32, (w, 2 * w), 1)
    dist = w + qi - kj
    valid = (dist >= 0) & (dist < WINDOW) & ((blk - 1) * w + kj >= 0)
    distf = dist.astype(F32)
    scale = HEAD_DIM ** -0.5
    group = len(slopes) // N_KV_HEADS
    for hk in range(N_KV_HEADS):
        kcols = slice(hk * HEAD_DIM, (hk + 1) * HEAD_DIM)
        kcat = jnp.concatenate([kp_ref[:, kcols], kc_ref[:, kcols]], axis=0).astype(BF16)
        vcat = jnp.concatenate([vp_ref[:, kcols], vc_ref[:, kcols]], axis=0).astype(BF16)
        for g in range(group):
            h = hk * group + g
            cols = slice(h * HEAD_DIM, (h + 1) * HEAD_DIM)
            s = lax.dot_general(q_ref[:, cols], kcat, (((1,), (1,)), ((), ())),
                                preferred_element_type=F32) * scale
            s = s - float(slopes[h]) * distf
            s = jnp.where(valid, s, -jnp.inf)
            p = _sink_softmax(s, sink_ref[h])
            o_ref[:, cols] = jnp.dot(p.astype(BF16), vcat, preferred_element_type=F32).astype(o_ref.dtype)


def _swa_prompt(q, kv, sinks, slopes, out_cols):
    b, l, _ = q.shape
    w = WINDOW
    kvw = N_KV_HEADS * HEAD_DIM
    tw = len(slopes) * HEAD_DIM
    assert l % w == 0
    prev = lambda i, j: jnp.maximum(j - 1, 0)
    return pl.pallas_call(
        functools.partial(_swa_prompt_kernel, slopes),
        grid=(b, l // w),
        in_specs=[pl.BlockSpec(memory_space=pltpu.SMEM),
                  pl.BlockSpec((None, w, tw), lambda i, j: (i, j, 0)),
                  pl.BlockSpec((None, w, kvw), lambda i, j: (i, prev(i, j), 0)),
                  pl.BlockSpec((None, w, kvw), lambda i, j: (i, j, 0)),
                  pl.BlockSpec((None, w, kvw), lambda i, j: (i, prev(i, j), 1)),
                  pl.BlockSpec((None, w, kvw), lambda i, j: (i, j, 1))],
        out_specs=pl.BlockSpec((None, w, tw), lambda i, j: (i, j, 0)),
        out_shape=jax.ShapeDtypeStruct((b, l, out_cols), BF16),
        compiler_params=_params("parallel", "parallel"),
        name="swa_prompt",
    )(sinks, q, kv, kv, kv, kv)


def _swa_sample_kernel(slopes, sink_ref, q_ref, k_ref, v_ref, o_ref):
    n_heads = len(slopes)
    group = n_heads // N_KV_HEADS
    wb = k_ref.shape[0] // N_KV_HEADS
    q = q_ref[...].astype(BF16)

    def kv_head(ref, hk):
        return ref[pl.ds(hk, wb, stride=N_KV_HEADS), :].astype(BF16)

    head = lax.broadcasted_iota(jnp.int32, (n_heads, wb), 0)
    pos = lax.broadcasted_iota(jnp.int32, (n_heads, wb), 1)
    distf = (wb - 1 - pos).astype(F32)
    slope = jnp.zeros((n_heads, wb), F32)
    sink = jnp.zeros((n_heads, 1), F32)
    head_col = lax.broadcasted_iota(jnp.int32, (n_heads, 1), 0)
    for h in range(n_heads):
        slope = jnp.where(head == h, float(slopes[h]), slope)
        sink = jnp.where(head_col == h, sink_ref[h], sink)
    scale = HEAD_DIM ** -0.5
    s = jnp.zeros((n_heads, wb), F32)
    for hk in range(N_KV_HEADS):
        s_hk = lax.dot_general(q, kv_head(k_ref, hk), (((1,), (1,)), ((), ())), preferred_element_type=F32)
        s = jnp.where(head // group == hk, s_hk, s)
    s = s * scale - slope * distf
    p = _sink_softmax(s, sink).astype(BF16)
    o = jnp.zeros((n_heads, HEAD_DIM), F32)
    head_o = lax.broadcasted_iota(jnp.int32, (n_heads, HEAD_DIM), 0)
    for hk in range(N_KV_HEADS):
        o_hk = jnp.dot(p, kv_head(v_ref, hk), preferred_element_type=F32)
        o = jnp.where(head_o // group == hk, o_hk, o)
    o_ref[...] = o.astype(o_ref.dtype)


def _swa_sample(q, win_k, win_v, sinks, slopes):
    b, n_heads, _ = q.shape
    rows = win_k.shape[1] * N_KV_HEADS
    win_k = win_k.reshape(b, rows, HEAD_DIM)
    win_v = win_v.reshape(b, rows, HEAD_DIM)
    return pl.pallas_call(
        functools.partial(_swa_sample_kernel, slopes),
        grid=(b,),
        in_specs=[pl.BlockSpec(memory_space=pltpu.SMEM),
                  pl.BlockSpec((None, n_heads, HEAD_DIM), lambda i: (i, 0, 0)),
                  pl.BlockSpec((None, rows, HEAD_DIM), lambda i: (i, 0, 0)),
                  pl.BlockSpec((None, rows, HEAD_DIM), lambda i: (i, 0, 0))],
        out_specs=pl.BlockSpec((None, n_heads, HEAD_DIM), lambda i: (i, 0, 0)),
        out_shape=jax.ShapeDtypeStruct((b, n_heads, HEAD_DIM), BF16),
        compiler_params=_params("parallel"),
        name="swa_sample",
    )(sinks, q, win_k, win_v)


def _lane_pick(x, lane):
    idx = lax.broadcasted_iota(jnp.int32, x.shape, x.ndim - 1)
    return jnp.sum(jnp.where(idx == lane, x, 0.0), axis=-1, keepdims=True)


def _gates(ba, alog, dtb):
    beta = jax.nn.sigmoid(ba)
    g = -jnp.exp(alog) * _softplus(ba + dtb)
    return beta, g


def _bdot(a, b, contract_b_last, precision=None):
    dims = (((2,), (2 if contract_b_last else 1,)), ((0,), (0,)))
    return lax.dot_general(a, b, dims, preferred_element_type=F32, precision=precision)


def _delta_prompt_kernel(n_heads, hb, q_ref, k_ref, v_ref, z_ref, ba_ref, wq_ref, wk_ref, wv_ref,
                         alog_ref, dtb_ref, gn_ref, o_ref, s_ref, pad_ref, beta_ref, g_ref):
    l = q_ref.shape[0]
    c = DELTA_CHUNK
    n = l // c
    hd = HEAD_DIM
    hstep = pl.program_id(1)

    @pl.when(hstep == 0)
    def _():
        beta_all, g_all = _gates(ba_ref[...], alog_ref[...], dtb_ref[...])
        beta_ref[...] = beta_all
        g_ref[...] = g_all

    pad_ref[0:8, :] = jnp.zeros((8, hd), F32)

    def conv(x_ref, w_ref, cols):
        pad_ref[8:8 + l, :] = x_ref[:, cols]
        base = 8 - (CONV_W - 1)
        out = pad_ref[base:base + l, :] * w_ref[0:1, cols]
        for j in range(1, CONV_W):
            out = out + pad_ref[base + j:base + j + l, :] * w_ref[j:j + 1, cols]
        return _silu(out)

    def l2norm(x):
        return x * lax.rsqrt(jnp.sum(x * x, axis=-1, keepdims=True) + EPS)

    row = lax.broadcasted_iota(jnp.int32, (1, c, c), 1)
    col = lax.broadcasted_iota(jnp.int32, (1, c, c), 2)
    incl = row >= col
    strict = row > col
    eye = row == col

    def lower_left(shift):
        return (((row >> (shift + 1)) == (col >> (shift + 1)))
                & (((row >> shift) & 1) == 1) & (((col >> shift) & 1) == 0))

    def prepare(hh):
        cols = slice(hh * hd, (hh + 1) * hd)
        head = hstep * hb + hh
        q3 = (l2norm(conv(q_ref, wq_ref, cols)) * (hd ** -0.5)).reshape(n, c, hd)
        k3 = l2norm(conv(k_ref, wk_ref, cols)).reshape(n, c, hd)
        v3 = conv(v_ref, wv_ref, cols).reshape(n, c, hd)
        beta = _lane_pick(beta_ref[...], head).reshape(n, c, 1)
        g = _lane_pick(g_ref[...], n_heads + head).reshape(n, c, 1)
        gc_row = jnp.sum(jnp.where(row <= col, jnp.broadcast_to(g, (n, c, c)), 0.0), axis=1, keepdims=True)
        gc_col = jnp.sum(jnp.where(eye, jnp.broadcast_to(gc_row, (n, c, c)), 0.0), axis=2, keepdims=True)
        decay = jnp.where(incl, jnp.exp(jnp.where(incl, gc_col - gc_row, 0.0)), 0.0)

        kb = k3.astype(BF16)
        qkk = _bdot(jnp.concatenate([q3, k3], axis=1).astype(BF16), kb, True)
        qk = qkk[:, :c] * decay
        lmat = jnp.where(strict, qkk[:, c:] * decay * beta, 0.0)
        tinv = jnp.where(eye, 1.0, 0.0) - jnp.where(lower_left(0), lmat, 0.0)
        for shift in range(1, int(math.log2(c))):
            cs = jnp.where(lower_left(shift), lmat, 0.0).astype(BF16)
            tb = tinv.astype(BF16)
            tinv = tinv - _bdot(_bdot(tb, cs, False).astype(BF16), tb, False)
        egc = jnp.exp(gc_col)
        rhs = jnp.concatenate([v3 * beta, k3 * (beta * egc)], axis=2).astype(BF16)
        uw = _bdot(tinv.astype(BF16), rhs, False)
        g_last = gc_col[:, c - 1:c, :]
        return dict(
            u=uw[:, :, :hd],
            wq=jnp.concatenate([uw[:, :, hd:], q3 * egc], axis=1).astype(BF16),
            qk=qk.astype(BF16),
            kt=(k3 * jnp.exp(g_last - gc_col)).astype(BF16),
            eg_last=jnp.exp(g_last))

    heads = [prepare(hh) for hh in range(hb)]
    states = [jnp.zeros((hd, hd), F32) for _ in range(hb)]
    for i in range(n):
        for hh, p in enumerate(heads):
            cols = slice(hh * hd, (hh + 1) * hd)
            tok = slice(i * c, (i + 1) * c)
            sb = states[hh].astype(BF16)
            ws = jnp.dot(p["wq"][i], sb, preferred_element_type=F32)
            vnb = (p["u"][i] - ws[:c]).astype(BF16)
            o = ws[c:] + jnp.dot(p["qk"][i], vnb, preferred_element_type=F32)
            states[hh] = states[hh] * p["eg_last"][i] + lax.dot_general(
                p["kt"][i], vnb, (((0,), (0,)), ((), ())), preferred_element_type=F32)
            o = o * lax.rsqrt(jnp.mean(o * o, axis=-1, keepdims=True) + EPS) * gn_ref[...]
            o_ref[tok, cols] = (o * _silu(z_ref[tok, cols])).astype(o_ref.dtype)
    for hh in range(hb):
        s_ref[hh] = states[hh]


def _delta_prompt(proj, ba, w_conv, layer, alog_pad, dtb_pad, gate_norm, n_heads, out_cols):
    b, l, _ = proj.shape
    hd = HEAD_DIM
    tw = n_heads * hd
    hb = DELTA_HEADS_PER_STEP
    assert n_heads % hb == 0
    steps = n_heads // hb
    col = lambda part: pl.BlockSpec((None, l, hb * hd), lambda i, h: (i, 0, part * steps + h))
    wcol = lambda part: pl.BlockSpec((None, CONV_W, hb * hd), lambda i, h: (layer, 0, part * steps + h))
    vec = pl.BlockSpec((1, LANES), lambda i, h: (0, 0))
    return pl.pallas_call(
        functools.partial(_delta_prompt_kernel, n_heads, hb),
        grid=(b, steps),
        in_specs=[col(0), col(1), col(2), col(3),
                  pl.BlockSpec((None, l, LANES), lambda i, h: (i, 0, 0)),
                  wcol(0), wcol(1), wcol(2), vec, vec, vec],
        out_specs=[pl.BlockSpec((None, l, hb * hd), lambda i, h: (i, 0, h)),
                   pl.BlockSpec((None, hb, hd, hd), lambda i, h: (i, h, 0, 0))],
        out_shape=[jax.ShapeDtypeStruct((b, l, out_cols), BF16),
                   jax.ShapeDtypeStruct((b, n_heads, hd, hd), F32)],
        scratch_shapes=[pltpu.VMEM((l + 8, hd), F32), pltpu.VMEM((l, LANES), F32), pltpu.VMEM((l, LANES), F32)],
        compiler_params=_params("parallel", "arbitrary"),
        name="delta_prompt",
    )(proj, proj, proj, proj, ba, w_conv, w_conv, w_conv, alog_pad, dtb_pad, gate_norm)


def _delta_sample_kernel(n_heads, x_ref, z_ref, ba_ref, hist_ref, wc_ref, s_ref, alog_ref, dtb_ref, gn_ref, *refs):
    o_ref, hist_o_ref, s_o_ref = refs[-3:]
    hd = HEAD_DIM
    tw = n_heads * hd
    x = x_ref[...]
    acc = hist_ref[0:1, :] * wc_ref[0:1, :]
    for j in range(1, CONV_W - 1):
        acc = acc + hist_ref[j:j + 1, :] * wc_ref[j:j + 1, :]
    acc = _silu(acc + x * wc_ref[CONV_W - 1:CONV_W, :])
    hist_o_ref[0:CONV_W - 2, :] = hist_ref[1:CONV_W - 1, :]
    hist_o_ref[CONV_W - 2:CONV_W - 1, :] = x

    beta_all, g_all = _gates(ba_ref[...], alog_ref[...], dtb_ref[...])

    def head_columns(off, scale):
        rows = jnp.concatenate([acc[:, off + h * hd: off + (h + 1) * hd] for h in range(n_heads)]
                               + [jnp.zeros((hd - n_heads, hd), F32)], axis=0)
        rows = rows * (lax.rsqrt(jnp.sum(rows * rows, axis=-1, keepdims=True) + EPS) * scale)
        return rows.T

    def per_head(make):
        return jnp.stack([make(h) for h in range(n_heads)], axis=0)

    q_cols = head_columns(0, hd ** -0.5)
    k_cols = head_columns(tw, 1.0)
    qc = per_head(lambda h: q_cols[:, h:h + 1])
    kc = per_head(lambda h: k_cols[:, h:h + 1])
    v = per_head(lambda h: acc[:, 2 * tw + h * hd: 2 * tw + (h + 1) * hd])
    z = per_head(lambda h: z_ref[:, h * hd:(h + 1) * hd])
    beta = per_head(lambda h: _lane_pick(beta_all, h))
    g = per_head(lambda h: _lane_pick(g_all, n_heads + h))
    s = s_ref[...] * jnp.exp(g)
    vn = beta * (v - jnp.sum(s * kc, axis=1, keepdims=True))
    s = s + kc * vn
    s_o_ref[...] = s
    o = jnp.sum(s * qc, axis=1, keepdims=True)
    o = o * lax.rsqrt(jnp.mean(o * o, axis=-1, keepdims=True) + EPS) * gn_ref[...]
    o = (o * _silu(z)).astype(o_ref.dtype)
    for h in range(n_heads):
        o_ref[:, h * hd:(h + 1) * hd] = o[h]


def _delta_sample(proj, ba, state_conv, w_conv, state_delta, layer, alog_pad, dtb_pad, gate_norm, n_heads,
                  new_states=None):
    b = proj.shape[0]
    hd = HEAD_DIM
    tw = n_heads * hd
    hist = CONV_W - 1
    vec = pl.BlockSpec((1, LANES), lambda i: (0, 0))
    in_specs = [pl.BlockSpec((None, 1, 3 * tw), lambda i: (i, 0, 0)),
                pl.BlockSpec((None, 1, tw), lambda i: (i, 0, 3)),
                pl.BlockSpec((None, 1, LANES), lambda i: (i, 0, 0)),
                pl.BlockSpec((None, None, hist, 3 * tw), lambda i: (layer, i, 0, 0)),
                pl.BlockSpec((None, CONV_W, 3 * tw), lambda i: (layer, 0, 0)),
                pl.BlockSpec((None, None, n_heads, hd, hd), lambda i: (layer, i, 0, 0, 0)),
                vec, vec, vec]
    args = [proj, proj, ba, state_conv, w_conv, state_delta, alog_pad, dtb_pad, gate_norm]
    aliases = {}
    if new_states is not None:
        in_specs.append(pl.BlockSpec(memory_space=pl.ANY))
        args.append(new_states)
        aliases = {len(args) - 1: 2}
    return pl.pallas_call(
        functools.partial(_delta_sample_kernel, n_heads),
        grid=(b,),
        in_specs=in_specs,
        out_specs=[pl.BlockSpec((None, 1, tw), lambda i: (i, 0, 0)),
                   pl.BlockSpec((None, hist, 3 * tw), lambda i: (i, 0, 0)),
                   pl.BlockSpec((None, None, n_heads, hd, hd), lambda i: (layer, i, 0, 0, 0))],
        out_shape=[jax.ShapeDtypeStruct((b, 1, tw), BF16),
                   jax.ShapeDtypeStruct((b, hist, 3 * tw), F32),
                   jax.ShapeDtypeStruct(state_delta.shape, F32)],
        input_output_aliases=aliases,
        compiler_params=_params("parallel"),
        name="delta_sample",
    )(*args)


def _trunk(x_p, x_s, mem_kv_p, cache_mk, cache_mv, conv_state, delta_state, buf_k, buf_v, wts):
    bp, lp, d = x_p.shape
    bs, ls, _ = x_s.shape
    assert ls == 1
    depth = wts["w_out"].shape[0]
    n_a = wts["w_in_a_t"].shape[0]
    d_ff = wts["w_down"].shape[1]
    tw = wts["w_conv"].shape[-1] // 3
    n_heads = tw // HEAD_DIM
    mem_w = d - tw
    kvw = N_KV_HEADS * HEAD_DIM
    slopes = _alibi_slopes(n_heads)
    off_b = 4 * tw
    off_qm = off_b + 2 * n_heads

    def lane_pad(vals):
        return jnp.zeros((1, LANES), F32).at[0, n_heads:2 * n_heads].set(vals.astype(F32))

    def resid(xp, fp, xs, fs, w_post, w_next):
        xp, np_ = _resid_norm(xp, fp, w_post, w_next)
        xs, ns_ = _resid_norm(xs, fs, w_post, w_next)
        return xp, xs, np_, ns_

    xp2 = x_p.reshape(bp * lp, d)
    xs2 = x_s.reshape(bs, d)
    (xn_p,) = _norm_cast(xp2, wts["norm_mix_pre"][0:1])
    (xn_s,) = _norm_cast(xs2, wts["norm_mix_pre"][0:1])
    conv_p, conv_s, delta_p = [], [], []
    delta_s = None
    kv_p = win_k = win_v = None
    for layer in range(depth):
        if layer < n_a:
            wt = wts["w_in_a_t"]
            proj_p, proj_s = _matmul(xn_p, wt, tail=xn_s, layer=layer, n_cols=off_b, w_is_nk=True)
            w_ba = jnp.pad(wt[layer, off_b:off_qm, :], ((0, LANES - 2 * n_heads), (0, 0)))
            ba_p, ba_s = _matmul(xn_p, w_ba, tail=xn_s, w_is_nk=True)
            qm_p, qm_s = _matmul(xn_p, wt[layer, off_qm:, :], tail=xn_s, out_dtype=BF16, w_is_nk=True)
            alog_pad = lane_pad(wts["a_log"][layer])
            dtb_pad = lane_pad(wts["dt_bias"][layer])
            gate_norm = wts["w_gate_norm"][layer].reshape(1, HEAD_DIM)
            proj_p = proj_p.reshape(bp, lp, off_b)
            tok_p, s_p = _delta_prompt(proj_p, ba_p.reshape(bp, lp, LANES), wts["w_conv"], layer,
                                       alog_pad, dtb_pad, gate_norm, n_heads, d)
            conv_p.append(proj_p[:, lp - (CONV_W - 1):, :3 * tw])
            tok_s, hist_s, delta_s = _delta_sample(proj_s.reshape(bs, 1, off_b), ba_s.reshape(bs, 1, LANES),
                                                   conv_state, wts["w_conv"], delta_state, layer, alog_pad, dtb_pad,
                                                   gate_norm, n_heads, new_states=delta_s)
            conv_s.append(hist_s)
            delta_p.append(s_p)
            mixed_p = _mem_attn(qm_p.reshape(bp, lp, mem_w), 0, mem_w, mem_kv_p, mem_kv_p, layer, 0, 1, dest=tok_p)
            mo_s = _mem_attn_decode(qm_s.reshape(bs, 1, mem_w), cache_mk, cache_mv, layer)
        else:
            lb = layer - n_a
            sinks = wts["sinks"][lb].astype(F32)
            proj_p, proj_s = _matmul(xn_p, wts["w_in_b"], tail=xn_s, layer=lb, out_dtype=BF16, bn=WIDE_TILE)
            proj_p = proj_p.reshape(bp, lp, d)
            tok_p = _swa_prompt(proj_p, kv_p, sinks, slopes, d)
            mixed_p = _mem_attn(proj_p, tw // mem_w, mem_w, mem_kv_p, mem_kv_p, layer, 0, 1, dest=tok_p)
            q_s = proj_s[:, :tw].reshape(bs, n_heads, HEAD_DIM)
            tok_s = _swa_sample(q_s, win_k, win_v, sinks, slopes).reshape(bs, 1, tw)
            mo_s = _mem_attn_decode(proj_s[:, tw:].reshape(bs, 1, mem_w), cache_mk, cache_mv, layer)
        mixed_p = mixed_p.reshape(bp * lp, d)
        mixed_s = jnp.concatenate([tok_s, mo_s], axis=-1).reshape(bs, d)
        mix_p, mix_s = _matmul(mixed_p, wts["w_out"], tail=mixed_s, layer=layer, out_dtype=BF16, bn=WIDE_TILE)
        xp2, xs2, (hn_p,), (hn_s,) = resid(xp2, mix_p, xs2, mix_s, wts["norm_mix_post"][layer],
                                           wts["norm_ffn_pre"][layer:layer + 1])
        hid_p, hid_s, w_down_bf16 = _swiglu_matmul(hn_p, hn_s, wts["w_gate_up"], wts["w_down"], layer)
        f_p = _matmul_bf16w(hid_p, w_down_bf16)
        f_s = _matmul_bf16w(hid_s, w_down_bf16)
        w_post = wts["norm_ffn_post"][layer]
        if layer + 1 == depth:
            xp2, xs2, _, _ = resid(xp2, f_p, xs2, f_s, w_post, None)
        elif layer + 1 == n_a:
            w_next = jnp.stack([wts["norm_mix_pre"][layer + 1], wts["norm_kv"]])
            xp2, xs2, (xn_p, xkv_p), (xn_s, xkv_s) = resid(xp2, f_p, xs2, f_s, w_post, w_next)
            kv_p, kv_s = _matmul(xkv_p, wts["w_kv"], tail=xkv_s)
            kv_p = kv_p.reshape(bp, lp, 2 * kvw)
            new_row = lambda t: t.reshape(bs, 1, N_KV_HEADS, HEAD_DIM)
            win_k = jnp.concatenate([buf_k[:, 1:], new_row(kv_s[:, :kvw])], axis=1)
            win_v = jnp.concatenate([buf_v[:, 1:], new_row(kv_s[:, kvw:])], axis=1)
        else:
            xp2, xs2, (xn_p,), (xn_s,) = resid(xp2, f_p, xs2, f_s, w_post,
                                               wts["norm_mix_pre"][layer + 1:layer + 2])
    wp = min(WINDOW, lp)
    heads = lambda t: t.reshape(t.shape[0], -1, N_KV_HEADS, HEAD_DIM)
    return (xp2.reshape(bp, lp, d), xs2.reshape(bs, 1, d),
            heads(kv_p[:, lp - wp:, :kvw]), heads(kv_p[:, lp - wp:, kvw:]), jnp.stack(conv_p), jnp.stack(delta_p),
            heads(win_k), heads(win_v), jnp.stack(conv_s), delta_s)


def kernel(x_prompt, x_sample, cache_mem_k, cache_mem_v, cache_swa_k, cache_swa_v, state_conv, state_delta, mem_prompt, w_in_a, w_conv, a_log, dt_bias, w_gate_norm, w_in_b, sinks, norm_kv, w_kv, norm_mem, w_mem_kv, w_out, norm_mix_pre, norm_mix_post, norm_ffn_pre, norm_ffn_post, w_gate_up, w_down):
    wts = dict(w_in_a_t=jnp.swapaxes(w_in_a, 1, 2), w_conv=w_conv, a_log=a_log, dt_bias=dt_bias,
               w_gate_norm=w_gate_norm, w_in_b=w_in_b, sinks=sinks, norm_kv=norm_kv, w_kv=w_kv, w_out=w_out,
               norm_mix_pre=norm_mix_pre, norm_mix_post=norm_mix_post, norm_ffn_pre=norm_ffn_pre,
               norm_ffn_post=norm_ffn_post, w_gate_up=w_gate_up, w_down=w_down)
    depth = w_out.shape[0]
    bp, n_mem, d = mem_prompt.shape
    bs = x_sample.shape[0]
    mem_w = w_mem_kv.shape[-1] // 2
    mem_shape = (depth, bp, n_mem, N_MEM_HEADS, mem_w // N_MEM_HEADS)

    memn = _norm_cast(mem_prompt.reshape(bp * n_mem, d), norm_mem)
    mem_kv = jnp.stack([_matmul(memn[i], w_mem_kv, layer=i) for i in range(depth)])
    mem_kv = mem_kv.reshape(depth, bp, n_mem, 2 * mem_w)
    mem_k_p = mem_kv[..., :mem_w].reshape(mem_shape)
    mem_v_p = mem_kv[..., mem_w:].reshape(mem_shape)
    cmk = _tile_rows(cache_mem_k)
    cmv = _tile_rows(cache_mem_v)
    y_p, y_s, swk_p, swv_p, conv_p, delta_p, swk_s, swv_s, conv_s, delta_s = _trunk(
        x_prompt, x_sample, mem_kv, cmk, cmv, state_conv, state_delta, cache_swa_k, cache_swa_v, wts)
    return (y_p, y_s, mem_k_p, mem_v_p, swk_p, swv_p, conv_p, delta_p, swk_s, swv_s, conv_s, delta_s)
```

```python
import functools
import math

import numpy as np
import jax
import jax.numpy as jnp
from jax import lax
from jax.experimental import pallas as pl
from jax.experimental.pallas import tpu as pltpu

F32 = jnp.float32
BF16 = jnp.bfloat16
EPS = 1e-6

HEAD_DIM = 128
N_MEM_HEADS = 4
N_KV_HEADS = 8
WINDOW = 128
CONV_W = 4
DELTA_CHUNK = 128
DELTA_HEADS_PER_STEP = 2
LANES = 128
VMEM_LIMIT_BYTES = 56 * 1024 * 1024
WIDE_TILE = 1024


def _params(*sem):
    return pltpu.CompilerParams(dimension_semantics=sem, vmem_limit_bytes=VMEM_LIMIT_BYTES)


def _alibi_slopes(n):
    def pow2_slopes(m):
        start = 2.0 ** (-8.0 / m)
        return [start ** (i + 1) for i in range(m)]
    c = 2 ** int(math.floor(math.log2(n)))
    s = pow2_slopes(c)
    if c < n:
        s = s + pow2_slopes(2 * c)[0::2][: n - c]
    return np.asarray(s, np.float32)


def _rms_rows(x, w):
    return x * lax.rsqrt(jnp.mean(x * x, axis=-1, keepdims=True) + EPS) * w


def _silu(x):
    return x * jax.nn.sigmoid(x)


def _softplus(x):
    return jnp.maximum(x, 0.0) + jnp.log1p(jnp.exp(-jnp.abs(x)))


def _row_block(m, target):
    b = min(m, target)
    assert m % b == 0, (m, b)
    return b


def _norm_kernel(x_ref, w_ref, *o_refs):
    x = x_ref[...]
    for j, o_ref in enumerate(o_refs):
        o_ref[...] = _rms_rows(x, w_ref[j:j + 1, :]).astype(o_ref.dtype)


def _norm_cast(x, ws):
    m, d = x.shape
    n = ws.shape[0]
    br = _row_block(m, 256)
    return pl.pallas_call(
        _norm_kernel,
        grid=(m // br,),
        in_specs=[pl.BlockSpec((br, d), lambda i: (i, 0)),
                  pl.BlockSpec((n, d), lambda i: (0, 0))],
        out_specs=[pl.BlockSpec((br, d), lambda i: (i, 0))] * n,
        out_shape=[jax.ShapeDtypeStruct((m, d), BF16)] * n,
        compiler_params=_params("parallel"),
        name="norm_cast",
    )(x, ws)


def _resid_kernel(n_next, x_ref, f_ref, wpost_ref, *refs):
    if n_next:
        wnext_ref, xo_ref = refs[0], refs[1]
        xn_refs = refs[2:]
    else:
        xo_ref = refs[0]
        xn_refs = ()
    xnew = x_ref[...] + _rms_rows(f_ref[...].astype(F32), wpost_ref[...])
    xo_ref[...] = xnew
    for j, o_ref in enumerate(xn_refs):
        o_ref[...] = _rms_rows(xnew, wnext_ref[j:j + 1, :]).astype(o_ref.dtype)


def _resid_norm(x, f, w_post, w_next):
    m, d = x.shape
    n_next = 0 if w_next is None else w_next.shape[0]
    br = _row_block(m, 256)
    row = pl.BlockSpec((br, d), lambda i: (i, 0))
    in_specs = [row, row, pl.BlockSpec((1, d), lambda i: (0, 0))]
    args = [x, f, w_post.reshape(1, d)]
    if n_next:
        in_specs.append(pl.BlockSpec((n_next, d), lambda i: (0, 0)))
        args.append(w_next)
    outs = pl.pallas_call(
        functools.partial(_resid_kernel, n_next),
        grid=(m // br,),
        in_specs=in_specs,
        out_specs=[row] * (1 + n_next),
        out_shape=[jax.ShapeDtypeStruct((m, d), F32)] + [jax.ShapeDtypeStruct((m, d), BF16)] * n_next,
        compiler_params=_params("parallel"),
        name="resid_norm",
    )(*args)
    return outs[0], list(outs[1:])


def _w_slab_spec(w, layer, k, bn, n_tiles, m_steps, first=0, w_is_nk=False):
    tile = lambda n: first + jnp.minimum(n, n_tiles - 1)
    if w_is_nk:
        shape, pos = (bn // m_steps, k), (lambda n, m: (tile(n) * m_steps + m, 0))
    else:
        shape, pos = (k // m_steps, bn), (lambda n, m: (m, tile(n)))
    assert shape[0] * m_steps == (bn if w_is_nk else k) and shape[0] % 16 == 0, (k, bn, m_steps)
    if w.ndim == 3:
        return pl.BlockSpec((None,) + shape, lambda n, m: (layer,) + pos(n, m))
    return pl.BlockSpec(shape, lambda n, m: pos(n, m))


def _stage_slab(w_ref, wb_ref):
    n, m = pl.program_id(0), pl.program_id(1)
    slab = w_ref.shape[0]
    rows = pl.ds(pl.multiple_of(m * slab, slab), slab)
    wb_ref[n % 2, rows, :] = w_ref[...].astype(BF16)


def _row_step(n, m):
    return jnp.where(n >= 1, m, 0)


def _rows_with_tail(a_ref, tail_ref):
    return jnp.concatenate([a_ref[...], tail_ref[...]], axis=0)


def _mm_kernel(has_tail, w_is_nk, *refs):
    if has_tail:
        a_ref, tail_ref, w_ref, o_ref, otail_ref, wb_ref = refs
    else:
        a_ref, w_ref, o_ref, wb_ref = refs
    n, step = pl.program_id(0), pl.program_id(1)

    def mm(rows):
        _stage_slab(w_ref, wb_ref)
        wb = wb_ref[(n + 1) % 2]
        if w_is_nk:
            return lax.dot_general(rows, wb, (((1,), (1,)), ((), ())), preferred_element_type=F32)
        return jnp.dot(rows, wb, preferred_element_type=F32)

    @pl.when(n == 0)
    def _():
        _stage_slab(w_ref, wb_ref)

    if not has_tail:
        @pl.when(n >= 1)
        def _():
            o_ref[...] = mm(a_ref[...]).astype(o_ref.dtype)
        return
    last = pl.num_programs(1) - 1

    @pl.when((n >= 1) & (step < last))
    def _():
        o_ref[...] = mm(a_ref[...]).astype(o_ref.dtype)

    @pl.when((n >= 1) & (step == last))
    def _():
        bm = a_ref.shape[0]
        r = mm(_rows_with_tail(a_ref, tail_ref))
        o_ref[...] = r[:bm].astype(o_ref.dtype)
        otail_ref[...] = r[bm:].astype(otail_ref.dtype)


def _matmul(a, w, *, tail=None, layer=0, col_off=0, n_cols=None, out_dtype=F32, bn=512, bm=1024, w_is_nk=False):
    m, k = a.shape
    n_total = w.shape[-2] if w_is_nk else w.shape[-1]
    n_cols = n_total - col_off if n_cols is None else n_cols
    bn = min(bn, n_cols)
    assert n_cols % bn == 0 and col_off % bn == 0, (n_cols, col_off, bn)
    bm = _row_block(m, bm)
    off = col_off // bn
    out_tile = lambda n: jnp.maximum(n - 1, 0)
    in_specs = [pl.BlockSpec((bm, k), lambda n, i: (_row_step(n, i), 0))]
    out_specs = [pl.BlockSpec((bm, bn), lambda n, i: (_row_step(n, i), out_tile(n)))]
    out_shape = [jax.ShapeDtypeStruct((m, n_cols), out_dtype)]
    args = [a]
    if tail is not None:
        mt = tail.shape[0]
        in_specs.append(pl.BlockSpec((mt, k), lambda n, i: (0, 0)))
        out_specs.append(pl.BlockSpec((mt, bn), lambda n, i: (0, out_tile(n))))
        out_shape.append(jax.ShapeDtypeStruct((mt, n_cols), out_dtype))
        args.append(tail)
    n_tiles, m_steps = n_cols // bn, m // bm
    in_specs.append(_w_slab_spec(w, layer, k, bn, n_tiles, m_steps, off, w_is_nk))
    outs = pl.pallas_call(
        functools.partial(_mm_kernel, tail is not None, w_is_nk),
        grid=(n_tiles + 1, m_steps),
        in_specs=in_specs,
        out_specs=out_specs,
        out_shape=out_shape,
        scratch_shapes=[pltpu.VMEM((2,) + ((bn, k) if w_is_nk else (k, bn)), BF16)],
        compiler_params=_params("arbitrary", "arbitrary"),
        name="matmul",
    )(*args, w)
    return outs[0] if tail is None else tuple(outs)


def _swiglu_kernel(a_ref, tail_ref, wg_ref, wu_ref, wd_ref, o_ref, otail_ref, wdb_ref, wgb_ref, wub_ref):
    n, step = pl.program_id(0), pl.program_id(1)
    last = pl.num_programs(1) - 1

    def stage():
        _stage_slab(wg_ref, wgb_ref)
        _stage_slab(wu_ref, wub_ref)

    def gated(rows):
        wdb_ref[...] = wd_ref[...].astype(BF16)
        stage()
        slot = (n + 1) % 2
        g = jnp.dot(rows, wgb_ref[slot], preferred_element_type=F32)
        u = jnp.dot(rows, wub_ref[slot], preferred_element_type=F32)
        return _silu(g) * u

    @pl.when(n == 0)
    def _():
        stage()

    @pl.when((n >= 1) & (step < last))
    def _():
        o_ref[...] = gated(a_ref[...]).astype(o_ref.dtype)

    @pl.when((n >= 1) & (step == last))
    def _():
        bm = a_ref.shape[0]
        r = gated(_rows_with_tail(a_ref, tail_ref))
        o_ref[...] = r[:bm].astype(o_ref.dtype)
        otail_ref[...] = r[bm:].astype(otail_ref.dtype)


def _swiglu_matmul(a, tail, w, w_down, layer, *, bn=256, bm=2048):
    m, k = a.shape
    mt = tail.shape[0]
    d_ff, n_down = w_down.shape[1:]
    assert d_ff % bn == 0
    bm = _row_block(m, bm)
    nb = d_ff // bn
    m_steps = m // bm
    slab = d_ff // (nb * m_steps)
    assert slab * nb * m_steps == d_ff and slab % 16 == 0, (d_ff, nb, m_steps)
    out_tile = lambda n: jnp.maximum(n - 1, 0)
    down_slab = lambda n, i: out_tile(n) * m_steps + _row_step(n, i)
    return pl.pallas_call(
        _swiglu_kernel,
        grid=(nb + 1, m_steps),
        in_specs=[pl.BlockSpec((bm, k), lambda n, i: (_row_step(n, i), 0)),
                  pl.BlockSpec((mt, k), lambda n, i: (0, 0)),
                  _w_slab_spec(w, layer, k, bn, nb, m_steps),
                  _w_slab_spec(w, layer, k, bn, nb, m_steps, nb),
                  pl.BlockSpec((None, slab, n_down), lambda n, i: (layer, down_slab(n, i), 0))],
        out_specs=[pl.BlockSpec((bm, bn), lambda n, i: (_row_step(n, i), out_tile(n))),
                   pl.BlockSpec((mt, bn), lambda n, i: (0, out_tile(n))),
                   pl.BlockSpec((slab, n_down), lambda n, i: (down_slab(n, i), 0))],
        out_shape=[jax.ShapeDtypeStruct((m, d_ff), BF16), jax.ShapeDtypeStruct((mt, d_ff), BF16),
                   jax.ShapeDtypeStruct((d_ff, n_down), BF16)],
        scratch_shapes=[pltpu.VMEM((2, k, bn), BF16), pltpu.VMEM((2, k, bn), BF16)],
        compiler_params=_params("arbitrary", "arbitrary"),
        name="swiglu_matmul",
    )(a, tail, w, w, w_down)


def _mm_bf16w_kernel(a_ref, w_ref, o_ref):
    o_ref[...] = jnp.dot(a_ref[...], w_ref[...], preferred_element_type=F32).astype(o_ref.dtype)


def _matmul_bf16w(a, w, *, bn=512, bm=512):
    m, k = a.shape
    n = w.shape[-1]
    assert n % bn == 0
    bm = _row_block(m, bm)
    return pl.pallas_call(
        _mm_bf16w_kernel,
        grid=(n // bn, m // bm),
        in_specs=[pl.BlockSpec((bm, k), lambda j, i: (i, 0)),
                  pl.BlockSpec((k, bn), lambda j, i: (0, j))],
        out_specs=pl.BlockSpec((bm, bn), lambda j, i: (i, j)),
        out_shape=jax.ShapeDtypeStruct((m, n), BF16),
        compiler_params=_params("parallel", "parallel"),
        name="matmul_bf16w",
    )(a, w)


def _mem_attn_kernel(head_dim, q_ref, k_ref, v_ref, *refs):
    o_ref = refs[-1]
    scale = head_dim ** -0.5
    for h in range(N_MEM_HEADS):
        cols = slice(h * head_dim, (h + 1) * head_dim)
        q = q_ref[:, cols]
        k = k_ref[:, cols].astype(BF16)
        v = v_ref[:, cols].astype(BF16)
        s = lax.dot_general(q, k, (((1,), (1,)), ((), ())), preferred_element_type=F32) * scale
        m = jnp.max(s, axis=-1, keepdims=True)
        p = jnp.exp(s - m)
        p = p * (1.0 / jnp.sum(p, axis=-1, keepdims=True))
        o_ref[:, cols] = jnp.dot(p.astype(BF16), v, preferred_element_type=F32).astype(o_ref.dtype)


def _mem_attn(q, q_col, width, mk, mv, layer, k_col, v_col, dest=None):
    b, l, _ = q.shape
    n_mem = mk.shape[2]
    head_dim = width // N_MEM_HEADS
    bl = _row_block(l, 512)
    in_specs = [pl.BlockSpec((None, bl, width), lambda i, j: (i, j, q_col)),
                pl.BlockSpec((None, None, n_mem, width), lambda i, j: (layer, i, 0, k_col)),
                pl.BlockSpec((None, None, n_mem, width), lambda i, j: (layer, i, 0, v_col))]
    args = [q, mk, mv]
    out_cols, out_col, aliases = width, 0, {}
    if dest is not None:
        out_cols = dest.shape[-1]
        out_col = out_cols // width - 1
        in_specs.append(pl.BlockSpec(memory_space=pl.ANY))
        args.append(dest)
        aliases = {3: 0}
    return pl.pallas_call(
        functools.partial(_mem_attn_kernel, head_dim),
        grid=(b, l // bl),
        in_specs=in_specs,
        out_specs=pl.BlockSpec((None, bl, width), lambda i, j: (i, j, out_col)),
        out_shape=jax.ShapeDtypeStruct((b, l, out_cols), BF16),
        input_output_aliases=aliases,
        compiler_params=_params("parallel", "parallel"),
        name="mem_attn",
    )(*args)


def _tile_rows(cache):
    lyr, b, n_mem, h, hd = cache.shape
    t = hd // LANES
    v = cache.reshape(lyr, b, n_mem, h, t, LANES)
    return jnp.swapaxes(v, 3, 4).reshape(lyr, b, n_mem * t * h, LANES)


def _mem_attn_decode_kernel(n_heads, q_ref, k_ref, v_ref, o_ref):
    tiles = q_ref.shape[-1] // (n_heads * LANES)
    head_dim = tiles * LANES
    group = tiles * n_heads
    n_mem = k_ref.shape[0] // group
    scale = head_dim ** -0.5

    def head_tile(ref, h, t):
        return ref[pl.ds(t * n_heads + h, n_mem, stride=group), :].astype(BF16)

    for h in range(n_heads):
        s = jnp.zeros((q_ref.shape[0], n_mem), F32)
        for t in range(tiles):
            q = q_ref[:, h * head_dim + t * LANES:h * head_dim + (t + 1) * LANES]
            s = s + lax.dot_general(q, head_tile(k_ref, h, t), (((1,), (1,)), ((), ())),
                                    preferred_element_type=F32)
        s = s * scale
        m = jnp.max(s, axis=-1, keepdims=True)
        p = jnp.exp(s - m)
        p = (p * (1.0 / jnp.sum(p, axis=-1, keepdims=True))).astype(BF16)
        for t in range(tiles):
            o = jnp.dot(p, head_tile(v_ref, h, t), preferred_element_type=F32)
            o_ref[:, h * head_dim + t * LANES:h * head_dim + (t + 1) * LANES] = o.astype(o_ref.dtype)


def _mem_attn_decode(q, mk_rows, mv_rows, layer):
    b, l, width = q.shape
    rows = mk_rows.shape[2]
    kv_spec = pl.BlockSpec((None, None, rows, LANES), lambda i: (layer, i, 0, 0))
    return pl.pallas_call(
        functools.partial(_mem_attn_decode_kernel, N_MEM_HEADS),
        grid=(b,),
        in_specs=[pl.BlockSpec((None, l, width), lambda i: (i, 0, 0)), kv_spec, kv_spec],
        out_specs=pl.BlockSpec((None, l, width), lambda i: (i, 0, 0)),
        out_shape=jax.ShapeDtypeStruct((b, l, width), BF16),
        compiler_params=_params("parallel"),
        name="mem_attn_decode",
    )(q, mk_rows, mv_rows)


def _sink_softmax(s, sink):
    m = jnp.maximum(jnp.max(s, axis=-1, keepdims=True), sink)
    p = jnp.exp(s - m)
    return p * (1.0 / (jnp.sum(p, axis=-1, keepdims=True) + jnp.exp(sink - m)))


def _swa_prompt_kernel(slopes, sink_ref, q_ref, kp_ref, kc_ref, vp_ref, vc_ref, o_ref):
    w = WINDOW
    blk = pl.program_id(1)
    qi = lax.broadcasted_iota(jnp.int32, (w, 2 * w), 0)
    kj = lax.broadcasted_iota(jnp.int32, (w, 2 * w), 1)
    dist = w + qi - kj
    valid = (dist >= 0) & (dist < WINDOW) & ((blk - 1) * w + kj >= 0)
    distf = dist.astype(F32)
    scale = HEAD_DIM ** -0.5
    group = len(slopes) // N_KV_HEADS
    for hk in range(N_KV_HEADS):
        kcols = slice(hk * HEAD_DIM, (hk + 1) * HEAD_DIM)
        kcat = jnp.concatenate([kp_ref[:, kcols], kc_ref[:, kcols]], axis=0).astype(BF16)
        vcat = jnp.concatenate([vp_ref[:, kcols], vc_ref[:, kcols]], axis=0).astype(BF16)
        for g in range(group):
            h = hk * group + g
            cols = slice(h * HEAD_DIM, (h + 1) * HEAD_DIM)
            s = lax.dot_general(q_ref[:, cols], kcat, (((1,), (1,)), ((), ())),
                                preferred_element_type=F32) * scale
            s = s - float(slopes[h]) * distf
            s = jnp.where(valid, s, -jnp.inf)
            p = _sink_softmax(s, sink_ref[h])
            o_ref[:, cols] = jnp.dot(p.astype(BF16), vcat, preferred_element_type=F32).astype(o_ref.dtype)


def _swa_prompt(q, kv, sinks, slopes, out_cols):
    b, l, _ = q.shape
    w = WINDOW
    kvw = N_KV_HEADS * HEAD_DIM
    tw = len(slopes) * HEAD_DIM
    assert l % w == 0
    prev = lambda i, j: jnp.maximum(j - 1, 0)
    return pl.pallas_call(
        functools.partial(_swa_prompt_kernel, slopes),
        grid=(b, l // w),
        in_specs=[pl.BlockSpec(memory_space=pltpu.SMEM),
                  pl.BlockSpec((None, w, tw), lambda i, j: (i, j, 0)),
                  pl.BlockSpec((None, w, kvw), lambda i, j: (i, prev(i, j), 0)),
                  pl.BlockSpec((None, w, kvw), lambda i, j: (i, j, 0)),
                  pl.BlockSpec((None, w, kvw), lambda i, j: (i, prev(i, j), 1)),
                  pl.BlockSpec((None, w, kvw), lambda i, j: (i, j, 1))],
        out_specs=pl.BlockSpec((None, w, tw), lambda i, j: (i, j, 0)),
        out_shape=jax.ShapeDtypeStruct((b, l, out_cols), BF16),
        compiler_params=_params("parallel", "parallel"),
        name="swa_prompt",
    )(sinks, q, kv, kv, kv, kv)


def _swa_sample_kernel(slopes, sink_ref, q_ref, k_ref, v_ref, o_ref):
    n_heads = len(slopes)
    group = n_heads // N_KV_HEADS
    wb = k_ref.shape[0] // N_KV_HEADS
    q = q_ref[...].astype(BF16)

    def kv_head(ref, hk):
        return ref[pl.ds(hk, wb, stride=N_KV_HEADS), :].astype(BF16)

    head = lax.broadcasted_iota(jnp.int32, (n_heads, wb), 0)
    pos = lax.broadcasted_iota(jnp.int32, (n_heads, wb), 1)
    distf = (wb - 1 - pos).astype(F32)
    slope = jnp.zeros((n_heads, wb), F32)
    sink = jnp.zeros((n_heads, 1), F32)
    head_col = lax.broadcasted_iota(jnp.int32, (n_heads, 1), 0)
    for h in range(n_heads):
        slope = jnp.where(head == h, float(slopes[h]), slope)
        sink = jnp.where(head_col == h, sink_ref[h], sink)
    scale = HEAD_DIM ** -0.5
    s = jnp.zeros((n_heads, wb), F32)
    for hk in range(N_KV_HEADS):
        s_hk = lax.dot_general(q, kv_head(k_ref, hk), (((1,), (1,)), ((), ())), preferred_element_type=F32)
        s = jnp.where(head // group == hk, s_hk, s)
    s = s * scale - slope * distf
    p = _sink_softmax(s, sink).astype(BF16)
    o = jnp.zeros((n_heads, HEAD_DIM), F32)
    head_o = lax.broadcasted_iota(jnp.int32, (n_heads, HEAD_DIM), 0)
    for hk in range(N_KV_HEADS):
        o_hk = jnp.dot(p, kv_head(v_ref, hk), preferred_element_type=F32)
        o = jnp.where(head_o // group == hk, o_hk, o)
    o_ref[...] = o.astype(o_ref.dtype)


def _swa_sample(q, win_k, win_v, sinks, slopes):
    b, n_heads, _ = q.shape
    rows = win_k.shape[1] * N_KV_HEADS
    win_k = win_k.reshape(b, rows, HEAD_DIM)
    win_v = win_v.reshape(b, rows, HEAD_DIM)
    return pl.pallas_call(
        functools.partial(_swa_sample_kernel, slopes),
        grid=(b,),
        in_specs=[pl.BlockSpec(memory_space=pltpu.SMEM),
                  pl.BlockSpec((None, n_heads, HEAD_DIM), lambda i: (i, 0, 0)),
                  pl.BlockSpec((None, rows, HEAD_DIM), lambda i: (i, 0, 0)),
                  pl.BlockSpec((None, rows, HEAD_DIM), lambda i: (i, 0, 0))],
        out_specs=pl.BlockSpec((None, n_heads, HEAD_DIM), lambda i: (i, 0, 0)),
        out_shape=jax.ShapeDtypeStruct((b, n_heads, HEAD_DIM), BF16),
        compiler_params=_params("parallel"),
        name="swa_sample",
    )(sinks, q, win_k, win_v)


def _lane_pick(x, lane):
    idx = lax.broadcasted_iota(jnp.int32, x.shape, x.ndim - 1)
    return jnp.sum(jnp.where(idx == lane, x, 0.0), axis=-1, keepdims=True)


def _gates(ba, alog, dtb):
    beta = jax.nn.sigmoid(ba)
    g = -jnp.exp(alog) * _softplus(ba + dtb)
    return beta, g


def _bdot(a, b, contract_b_last, precision=None):
    dims = (((2,), (2 if contract_b_last else 1,)), ((0,), (0,)))
    return lax.dot_general(a, b, dims, preferred_element_type=F32, precision=precision)


def _delta_prompt_kernel(n_heads, hb, q_ref, k_ref, v_ref, z_ref, ba_ref, wq_ref, wk_ref, wv_ref,
                         alog_ref, dtb_ref, gn_ref, o_ref, s_ref, pad_ref, beta_ref, g_ref):
    l = q_ref.shape[0]
    c = DELTA_CHUNK
    n = l // c
    hd = HEAD_DIM
    hstep = pl.program_id(1)

    @pl.when(hstep == 0)
    def _():
        beta_all, g_all = _gates(ba_ref[...], alog_ref[...], dtb_ref[...])
        beta_ref[...] = beta_all
        g_ref[...] = g_all

    pad_ref[0:8, :] = jnp.zeros((8, hd), F32)

    def conv(x_ref, w_ref, cols):
        pad_ref[8:8 + l, :] = x_ref[:, cols]
        base = 8 - (CONV_W - 1)
        out = pad_ref[base:base + l, :] * w_ref[0:1, cols]
        for j in range(1, CONV_W):
            out = out + pad_ref[base + j:base + j + l, :] * w_ref[j:j + 1, cols]
        return _silu(out)

    def l2norm(x):
        return x * lax.rsqrt(jnp.sum(x * x, axis=-1, keepdims=True) + EPS)

    row = lax.broadcasted_iota(jnp.int32, (1, c, c), 1)
    col = lax.broadcasted_iota(jnp.int32, (1, c, c), 2)
    incl = row >= col
    strict = row > col
    eye = row == col

    def lower_left(shift):
        return (((row >> (shift + 1)) == (col >> (shift + 1)))
                & (((row >> shift) & 1) == 1) & (((col >> shift) & 1) == 0))

    def prepare(hh):
        cols = slice(hh * hd, (hh + 1) * hd)
        head = hstep * hb + hh
        q3 = (l2norm(conv(q_ref, wq_ref, cols)) * (hd ** -0.5)).reshape(n, c, hd)
        k3 = l2norm(conv(k_ref, wk_ref, cols)).reshape(n, c, hd)
        v3 = conv(v_ref, wv_ref, cols).reshape(n, c, hd)
        beta = _lane_pick(beta_ref[...], head).reshape(n, c, 1)
        g = _lane_pick(g_ref[...], n_heads + head).reshape(n, c, 1)
        gc_row = jnp.sum(jnp.where(row <= col, jnp.broadcast_to(g, (n, c, c)), 0.0), axis=1, keepdims=True)
        gc_col = jnp.sum(jnp.where(eye, jnp.broadcast_to(gc_row, (n, c, c)), 0.0), axis=2, keepdims=True)
        decay = jnp.where(incl, jnp.exp(jnp.where(incl, gc_col - gc_row, 0.0)), 0.0)

        kb = k3.astype(BF16)
        qkk = _bdot(jnp.concatenate([q3, k3], axis=1).astype(BF16), kb, True)
        qk = qkk[:, :c] * decay
        lmat = jnp.where(strict, qkk[:, c:] * decay * beta, 0.0)
        tinv = jnp.where(eye, 1.0, 0.0) - jnp.where(lower_left(0), lmat, 0.0)
        for shift in range(1, int(math.log2(c))):
            cs = jnp.where(lower_left(shift), lmat, 0.0).astype(BF16)
            tb = tinv.astype(BF16)
            tinv = tinv - _bdot(_bdot(tb, cs, False).astype(BF16), tb, False)
        egc = jnp.exp(gc_col)
        rhs = jnp.concatenate([v3 * beta, k3 * (beta * egc)], axis=2).astype(BF16)
        uw = _bdot(tinv.astype(BF16), rhs, False)
        g_last = gc_col[:, c - 1:c, :]
        return dict(
            u=uw[:, :, :hd],
            wq=jnp.concatenate([uw[:, :, hd:], q3 * egc], axis=1).astype(BF16),
            qk=qk.astype(BF16),
            kt=(k3 * jnp.exp(g_last - gc_col)).astype(BF16),
            eg_last=jnp.exp(g_last))

    heads = [prepare(hh) for hh in range(hb)]
    states = [jnp.zeros((hd, hd), F32) for _ in range(hb)]
    for i in range(n):
        for hh, p in enumerate(heads):
            cols = slice(hh * hd, (hh + 1) * hd)
            tok = slice(i * c, (i + 1) * c)
            sb = states[hh].astype(BF16)
            ws = jnp.dot(p["wq"][i], sb, preferred_element_type=F32)
            vnb = (p["u"][i] - ws[:c]).astype(BF16)
            o = ws[c:] + jnp.dot(p["qk"][i], vnb, preferred_element_type=F32)
            states[hh] = states[hh] * p["eg_last"][i] + lax.dot_general(
                p["kt"][i], vnb, (((0,), (0,)), ((), ())), preferred_element_type=F32)
            o = o * lax.rsqrt(jnp.mean(o * o, axis=-1, keepdims=True) + EPS) * gn_ref[...]
            o_ref[tok, cols] = (o * _silu(z_ref[tok, cols])).astype(o_ref.dtype)
    for hh in range(hb):
        s_ref[hh] = states[hh]


def _delta_prompt(proj, ba, w_conv, layer, alog_pad, dtb_pad, gate_norm, n_heads, out_cols):
    b, l, _ = proj.shape
    hd = HEAD_DIM
    tw = n_heads * hd
    hb = DELTA_HEADS_PER_STEP
    assert n_heads % hb == 0
    steps = n_heads // hb
    col = lambda part: pl.BlockSpec((None, l, hb * hd), lambda i, h: (i, 0, part * steps + h))
    wcol = lambda part: pl.BlockSpec((None, CONV_W, hb * hd), lambda i, h: (layer, 0, part * steps + h))
    vec = pl.BlockSpec((1, LANES), lambda i, h: (0, 0))
    return pl.pallas_call(
        functools.partial(_delta_prompt_kernel, n_heads, hb),
        grid=(b, steps),
        in_specs=[col(0), col(1), col(2), col(3),
                  pl.BlockSpec((None, l, LANES), lambda i, h: (i, 0, 0)),
                  wcol(0), wcol(1), wcol(2), vec, vec, vec],
        out_specs=[pl.BlockSpec((None, l, hb * hd), lambda i, h: (i, 0, h)),
                   pl.BlockSpec((None, hb, hd, hd), lambda i, h: (i, h, 0, 0))],
        out_shape=[jax.ShapeDtypeStruct((b, l, out_cols), BF16),
                   jax.ShapeDtypeStruct((b, n_heads, hd, hd), F32)],
        scratch_shapes=[pltpu.VMEM((l + 8, hd), F32), pltpu.VMEM((l, LANES), F32), pltpu.VMEM((l, LANES), F32)],
        compiler_params=_params("parallel", "arbitrary"),
        name="delta_prompt",
    )(proj, proj, proj, proj, ba, w_conv, w_conv, w_conv, alog_pad, dtb_pad, gate_norm)


def _delta_sample_kernel(n_heads, x_ref, z_ref, ba_ref, hist_ref, wc_ref, s_ref, alog_ref, dtb_ref, gn_ref, *refs):
    o_ref, hist_o_ref, s_o_ref = refs[-3:]
    hd = HEAD_DIM
    tw = n_heads * hd
    x = x_ref[...]
    acc = hist_ref[0:1, :] * wc_ref[0:1, :]
    for j in range(1, CONV_W - 1):
        acc = acc + hist_ref[j:j + 1, :] * wc_ref[j:j + 1, :]
    acc = _silu(acc + x * wc_ref[CONV_W - 1:CONV_W, :])
    hist_o_ref[0:CONV_W - 2, :] = hist_ref[1:CONV_W - 1, :]
    hist_o_ref[CONV_W - 2:CONV_W - 1, :] = x

    beta_all, g_all = _gates(ba_ref[...], alog_ref[...], dtb_ref[...])

    def head_columns(off, scale):
        rows = jnp.concatenate([acc[:, off + h * hd: off + (h + 1) * hd] for h in range(n_heads)]
                               + [jnp.zeros((hd - n_heads, hd), F32)], axis=0)
        rows = rows * (lax.rsqrt(jnp.sum(rows * rows, axis=-1, keepdims=True) + EPS) * scale)
        return rows.T

    def per_head(make):
        return jnp.stack([make(h) for h in range(n_heads)], axis=0)

    q_cols = head_columns(0, hd ** -0.5)
    k_cols = head_columns(tw, 1.0)
    qc = per_head(lambda h: q_cols[:, h:h + 1])
    kc = per_head(lambda h: k_cols[:, h:h + 1])
    v = per_head(lambda h: acc[:, 2 * tw + h * hd: 2 * tw + (h + 1) * hd])
    z = per_head(lambda h: z_ref[:, h * hd:(h + 1) * hd])
    beta = per_head(lambda h: _lane_pick(beta_all, h))
    g = per_head(lambda h: _lane_pick(g_all, n_heads + h))
    s = s_ref[...] * jnp.exp(g)
    vn = beta * (v - jnp.sum(s * kc, axis=1, keepdims=True))
    s = s + kc * vn
    s_o_ref[...] = s
    o = jnp.sum(s * qc, axis=1, keepdims=True)
    o = o * lax.rsqrt(jnp.mean(o * o, axis=-1, keepdims=True) + EPS) * gn_ref[...]
    o = (o * _silu(z)).astype(o_ref.dtype)
    for h in range(n_heads):
        o_ref[:, h * hd:(h + 1) * hd] = o[h]


def _delta_sample(proj, ba, state_conv, w_conv, state_delta, layer, alog_pad, dtb_pad, gate_norm, n_heads,
                  new_states=None):
    b = proj.shape[0]
    hd = HEAD_DIM
    tw = n_heads * hd
    hist = CONV_W - 1
    vec = pl.BlockSpec((1, LANES), lambda i: (0, 0))
    in_specs = [pl.BlockSpec((None, 1, 3 * tw), lambda i: (i, 0, 0)),
                pl.BlockSpec((None, 1, tw), lambda i: (i, 0, 3)),
                pl.BlockSpec((None, 1, LANES), lambda i: (i, 0, 0)),
                pl.BlockSpec((None, None, hist, 3 * tw), lambda i: (layer, i, 0, 0)),
                pl.BlockSpec((None, CONV_W, 3 * tw), lambda i: (layer, 0, 0)),
                pl.BlockSpec((None, None, n_heads, hd, hd), lambda i: (layer, i, 0, 0, 0)),
                vec, vec, vec]
    args = [proj, proj, ba, state_conv, w_conv, state_delta, alog_pad, dtb_pad, gate_norm]
    aliases = {}
    if new_states is not None:
        in_specs.append(pl.BlockSpec(memory_space=pl.ANY))
        args.append(new_states)
        aliases = {len(args) - 1: 2}
    return pl.pallas_call(
        functools.partial(_delta_sample_kernel, n_heads),
        grid=(b,),
        in_specs=in_specs,
        out_specs=[pl.BlockSpec((None, 1, tw), lambda i: (i, 0, 0)),
                   pl.BlockSpec((None, hist, 3 * tw), lambda i: (i, 0, 0)),
                   pl.BlockSpec((None, None, n_heads, hd, hd), lambda i: (layer, i, 0, 0, 0))],
        out_shape=[jax.ShapeDtypeStruct((b, 1, tw), BF16),
                   jax.ShapeDtypeStruct((b, hist, 3 * tw), F32),
                   jax.ShapeDtypeStruct(state_delta.shape, F32)],
        input_output_aliases=aliases,
        compiler_params=_params("parallel"),
        name="delta_sample",
    )(*args)


def _trunk(x_p, x_s, mem_kv_p, cache_mk, cache_mv, conv_state, delta_state, buf_k, buf_v, wts):
    bp, lp, d = x_p.shape
    bs, ls, _ = x_s.shape
    assert ls == 1
    depth = wts["w_out"].shape[0]
    n_a = wts["w_in_a_t"].shape[0]
    d_ff = wts["w_down"].shape[1]
    tw = wts["w_conv"].shape[-1] // 3
    n_heads = tw // HEAD_DIM
    mem_w = d - tw
    kvw = N_KV_HEADS * HEAD_DIM
    slopes = _alibi_slopes(n_heads)
    off_b = 4 * tw
    off_qm = off_b + 2 * n_heads

    def lane_pad(vals):
        return jnp.zeros((1, LANES), F32).at[0, n_heads:2 * n_heads].set(vals.astype(F32))

    def resid(xp, fp, xs, fs, w_post, w_next):
        xp, np_ = _resid_norm(xp, fp, w_post, w_next)
        xs, ns_ = _resid_norm(xs, fs, w_post, w_next)
        return xp, xs, np_, ns_

    xp2 = x_p.reshape(bp * lp, d)
    xs2 = x_s.reshape(bs, d)
    (xn_p,) = _norm_cast(xp2, wts["norm_mix_pre"][0:1])
    (xn_s,) = _norm_cast(xs2, wts["norm_mix_pre"][0:1])
    conv_p, conv_s, delta_p = [], [], []
    delta_s = None
    kv_p = win_k = win_v = None
    for layer in range(depth):
        if layer < n_a:
            wt = wts["w_in_a_t"]
            proj_p, proj_s = _matmul(xn_p, wt, tail=xn_s, layer=layer, n_cols=off_b, w_is_nk=True, bn=WIDE_TILE)
            w_ba = jnp.pad(wt[layer, off_b:off_qm, :], ((0, LANES - 2 * n_heads), (0, 0)))
            ba_p, ba_s = _matmul(xn_p, w_ba, tail=xn_s, w_is_nk=True)
            qm_p, qm_s = _matmul(xn_p, wt[layer, off_qm:, :], tail=xn_s, out_dtype=BF16, w_is_nk=True)
            alog_pad = lane_pad(wts["a_log"][layer])
            dtb_pad = lane_pad(wts["dt_bias"][layer])
            gate_norm = wts["w_gate_norm"][layer].reshape(1, HEAD_DIM)
            proj_p = proj_p.reshape(bp, lp, off_b)
            tok_p, s_p = _delta_prompt(proj_p, ba_p.reshape(bp, lp, LANES), wts["w_conv"], layer,
                                       alog_pad, dtb_pad, gate_norm, n_heads, d)
            conv_p.append(proj_p[:, lp - (CONV_W - 1):, :3 * tw])
            tok_s, hist_s, delta_s = _delta_sample(proj_s.reshape(bs, 1, off_b), ba_s.reshape(bs, 1, LANES),
                                                   conv_state, wts["w_conv"], delta_state, layer, alog_pad, dtb_pad,
                                                   gate_norm, n_heads, new_states=delta_s)
            conv_s.append(hist_s)
            delta_p.append(s_p)
            mixed_p = _mem_attn(qm_p.reshape(bp, lp, mem_w), 0, mem_w, mem_kv_p, mem_kv_p, layer, 0, 1, dest=tok_p)
            mo_s = _mem_attn_decode(qm_s.reshape(bs, 1, mem_w), cache_mk, cache_mv, layer)
        else:
            lb = layer - n_a
            sinks = wts["sinks"][lb].astype(F32)
            proj_p, proj_s = _matmul(xn_p, wts["w_in_b"], tail=xn_s, layer=lb, out_dtype=BF16, bn=WIDE_TILE)
            proj_p = proj_p.reshape(bp, lp, d)
            tok_p = _swa_prompt(proj_p, kv_p, sinks, slopes, d)
            mixed_p = _mem_attn(proj_p, tw // mem_w, mem_w, mem_kv_p, mem_kv_p, layer, 0, 1, dest=tok_p)
            q_s = proj_s[:, :tw].reshape(bs, n_heads, HEAD_DIM)
            tok_s = _swa_sample(q_s, win_k, win_v, sinks, slopes).reshape(bs, 1, tw)
            mo_s = _mem_attn_decode(proj_s[:, tw:].reshape(bs, 1, mem_w), cache_mk, cache_mv, layer)
        mixed_p = mixed_p.reshape(bp * lp, d)
        mixed_s = jnp.concatenate([tok_s, mo_s], axis=-1).reshape(bs, d)
        mix_p, mix_s = _matmul(mixed_p, wts["w_out"], tail=mixed_s, layer=layer, out_dtype=BF16, bn=WIDE_TILE)
        xp2, xs2, (hn_p,), (hn_s,) = resid(xp2, mix_p, xs2, mix_s, wts["norm_mix_post"][layer],
                                           wts["norm_ffn_pre"][layer:layer + 1])
        hid_p, hid_s, w_down_bf16 = _swiglu_matmul(hn_p, hn_s, wts["w_gate_up"], wts["w_down"], layer)
        f_p = _matmul(hid_p, w_down_bf16, out_dtype=BF16, bm=512)
        f_s = _matmul(hid_s, w_down_bf16, out_dtype=BF16)
        w_post = wts["norm_ffn_post"][layer]
        if layer + 1 == depth:
            xp2, xs2, _, _ = resid(xp2, f_p, xs2, f_s, w_post, None)
        elif layer + 1 == n_a:
            w_next = jnp.stack([wts["norm_mix_pre"][layer + 1], wts["norm_kv"]])
            xp2, xs2, (xn_p, xkv_p), (xn_s, xkv_s) = resid(xp2, f_p, xs2, f_s, w_post, w_next)
            kv_p, kv_s = _matmul(xkv_p, wts["w_kv"], tail=xkv_s, bn=WIDE_TILE)
            kv_p = kv_p.reshape(bp, lp, 2 * kvw)
            new_row = lambda t: t.reshape(bs, 1, N_KV_HEADS, HEAD_DIM)
            win_k = jnp.concatenate([buf_k[:, 1:], new_row(kv_s[:, :kvw])], axis=1)
            win_v = jnp.concatenate([buf_v[:, 1:], new_row(kv_s[:, kvw:])], axis=1)
        else:
            xp2, xs2, (xn_p,), (xn_s,) = resid(xp2, f_p, xs2, f_s, w_post,
                                               wts["norm_mix_pre"][layer + 1:layer + 2])
    wp = min(WINDOW, lp)
    heads = lambda t: t.reshape(t.shape[0], -1, N_KV_HEADS, HEAD_DIM)
    return (xp2.reshape(bp, lp, d), xs2.reshape(bs, 1, d),
            heads(kv_p[:, lp - wp:, :kvw]), heads(kv_p[:, lp - wp:, kvw:]), jnp.stack(conv_p), jnp.stack(delta_p),
            heads(win_k), heads(win_v), jnp.stack(conv_s), delta_s)


def kernel(x_prompt, x_sample, cache_mem_k, cache_mem_v, cache_swa_k, cache_swa_v, state_conv, state_delta, mem_prompt, w_in_a, w_conv, a_log, dt_bias, w_gate_norm, w_in_b, sinks, norm_kv, w_kv, norm_mem, w_mem_kv, w_out, norm_mix_pre, norm_mix_post, norm_ffn_pre, norm_ffn_post, w_gate_up, w_down):
    wts = dict(w_in_a_t=jnp.swapaxes(w_in_a, 1, 2), w_conv=w_conv, a_log=a_log, dt_bias=dt_bias,
               w_gate_norm=w_gate_norm, w_in_b=w_in_b, sinks=sinks, norm_kv=norm_kv, w_kv=w_kv, w_out=w_out,
               norm_mix_pre=norm_mix_pre, norm_mix_post=norm_mix_post, norm_ffn_pre=norm_ffn_pre,
               norm_ffn_post=norm_ffn_post, w_gate_up=w_gate_up, w_down=w_down)
    depth = w_out.shape[0]
    bp, n_mem, d = mem_prompt.shape
    bs = x_sample.shape[0]
    mem_w = w_mem_kv.shape[-1] // 2
    mem_shape = (depth, bp, n_mem, N_MEM_HEADS, mem_w // N_MEM_HEADS)

    memn = _norm_cast(mem_prompt.reshape(bp * n_mem, d), norm_mem)
    mem_kv = jnp.stack([_matmul(memn[i], w_mem_kv, layer=i) for i in range(depth)])
    mem_kv = mem_kv.reshape(depth, bp, n_mem, 2 * mem_w)
    mem_k_p = mem_kv[..., :mem_w].reshape(mem_shape)
    mem_v_p = mem_kv[..., mem_w:].reshape(mem_shape)
    cmk = _tile_rows(cache_mem_k)
    cmv = _tile_rows(cache_mem_v)
    y_p, y_s, swk_p, swv_p, conv_p, delta_p, swk_s, swv_s, conv_s, delta_s = _trunk(
        x_prompt, x_sample, mem_kv, cmk, cmv, state_conv, state_delta, cache_swa_k, cache_swa_v, wts)
    return (y_p, y_s, mem_k_p, mem_v_p, swk_p, swv_p, conv_p, delta_p, swk_s, swv_s, conv_s, delta_s)
```

```python
import functools
import math

import numpy as np
import jax
import jax.numpy as jnp
from jax import lax
from jax.experimental import pallas as pl
from jax.experimental.pallas import tpu as pltpu

F32 = jnp.float32
BF16 = jnp.bfloat16
EPS = 1e-6

HEAD_DIM = 128
N_MEM_HEADS = 4
N_KV_HEADS = 8
WINDOW = 128
CONV_W = 4
DELTA_CHUNK = 128
DELTA_HEADS_PER_STEP = 2
LANES = 128
VMEM_LIMIT_BYTES = 56 * 1024 * 1024
SWIGLU_SUB_ROWS = 1024
WIDE_TILE = 1024


def _params(*sem):
    return pltpu.CompilerParams(dimension_semantics=sem, vmem_limit_bytes=VMEM_LIMIT_BYTES)


def _alibi_slopes(n):
    def pow2_slopes(m):
        start = 2.0 ** (-8.0 / m)
        return [start ** (i + 1) for i in range(m)]
    c = 2 ** int(math.floor(math.log2(n)))
    s = pow2_slopes(c)
    if c < n:
        s = s + pow2_slopes(2 * c)[0::2][: n - c]
    return np.asarray(s, np.float32)


def _rms_rows(x, w):
    return x * lax.rsqrt(jnp.mean(x * x, axis=-1, keepdims=True) + EPS) * w


def _silu(x):
    return x * jax.nn.sigmoid(x)


def _softplus(x):
    return jnp.maximum(x, 0.0) + jnp.log1p(jnp.exp(-jnp.abs(x)))


def _row_block(m, target):
    b = min(m, target)
    assert m % b == 0, (m, b)
    return b


def _norm_kernel(x_ref, w_ref, *o_refs):
    x = x_ref[...]
    for j, o_ref in enumerate(o_refs):
        o_ref[...] = _rms_rows(x, w_ref[j:j + 1, :]).astype(o_ref.dtype)


def _norm_cast(x, ws):
    m, d = x.shape
    n = ws.shape[0]
    br = _row_block(m, 256)
    return pl.pallas_call(
        _norm_kernel,
        grid=(m // br,),
        in_specs=[pl.BlockSpec((br, d), lambda i: (i, 0)),
                  pl.BlockSpec((n, d), lambda i: (0, 0))],
        out_specs=[pl.BlockSpec((br, d), lambda i: (i, 0))] * n,
        out_shape=[jax.ShapeDtypeStruct((m, d), BF16)] * n,
        compiler_params=_params("parallel"),
        name="norm_cast",
    )(x, ws)


def _resid_kernel(n_next, x_ref, f_ref, wpost_ref, *refs):
    if n_next:
        wnext_ref, xo_ref = refs[0], refs[1]
        xn_refs = refs[2:]
    else:
        xo_ref = refs[0]
        xn_refs = ()
    xnew = x_ref[...] + _rms_rows(f_ref[...].astype(F32), wpost_ref[...])
    xo_ref[...] = xnew
    for j, o_ref in enumerate(xn_refs):
        o_ref[...] = _rms_rows(xnew, wnext_ref[j:j + 1, :]).astype(o_ref.dtype)


def _resid_norm(x, f, w_post, w_next):
    m, d = x.shape
    n_next = 0 if w_next is None else w_next.shape[0]
    br = _row_block(m, 256)
    row = pl.BlockSpec((br, d), lambda i: (i, 0))
    in_specs = [row, row, pl.BlockSpec((1, d), lambda i: (0, 0))]
    args = [x, f, w_post.reshape(1, d)]
    if n_next:
        in_specs.append(pl.BlockSpec((n_next, d), lambda i: (0, 0)))
        args.append(w_next)
    outs = pl.pallas_call(
        functools.partial(_resid_kernel, n_next),
        grid=(m // br,),
        in_specs=in_specs,
        out_specs=[row] * (1 + n_next),
        out_shape=[jax.ShapeDtypeStruct((m, d), F32)] + [jax.ShapeDtypeStruct((m, d), BF16)] * n_next,
        compiler_params=_params("parallel"),
        name="resid_norm",
    )(*args)
    return outs[0], list(outs[1:])


def _w_slab_spec(w, layer, k, bn, n_tiles, m_steps, first=0, w_is_nk=False):
    tile = lambda n: first + jnp.minimum(n, n_tiles - 1)
    if w_is_nk:
        shape, pos = (bn // m_steps, k), (lambda n, m: (tile(n) * m_steps + m, 0))
    else:
        shape, pos = (k // m_steps, bn), (lambda n, m: (m, tile(n)))
    assert shape[0] * m_steps == (bn if w_is_nk else k) and shape[0] % 16 == 0, (k, bn, m_steps)
    if w.ndim == 3:
        return pl.BlockSpec((None,) + shape, lambda n, m: (layer,) + pos(n, m))
    return pl.BlockSpec(shape, lambda n, m: pos(n, m))


def _stage_slab(w_ref, wb_ref):
    n, m = pl.program_id(0), pl.program_id(1)
    slab = w_ref.shape[0]
    rows = pl.ds(pl.multiple_of(m * slab, slab), slab)
    wb_ref[n % 2, rows, :] = w_ref[...].astype(BF16)


def _row_step(n, m):
    return jnp.where(n >= 1, m, 0)


def _rows_with_tail(a_ref, tail_ref):
    return jnp.concatenate([a_ref[...], tail_ref[...]], axis=0)


def _mm_kernel(has_tail, w_is_nk, *refs):
    if has_tail:
        a_ref, tail_ref, w_ref, o_ref, otail_ref, wb_ref = refs
    else:
        a_ref, w_ref, o_ref, wb_ref = refs
    n, step = pl.program_id(0), pl.program_id(1)

    def mm(rows):
        _stage_slab(w_ref, wb_ref)
        wb = wb_ref[(n + 1) % 2]
        if w_is_nk:
            return lax.dot_general(rows, wb, (((1,), (1,)), ((), ())), preferred_element_type=F32)
        return jnp.dot(rows, wb, preferred_element_type=F32)

    @pl.when(n == 0)
    def _():
        _stage_slab(w_ref, wb_ref)

    if not has_tail:
        @pl.when(n >= 1)
        def _():
            o_ref[...] = mm(a_ref[...]).astype(o_ref.dtype)
        return
    last = pl.num_programs(1) - 1

    @pl.when((n >= 1) & (step < last))
    def _():
        o_ref[...] = mm(a_ref[...]).astype(o_ref.dtype)

    @pl.when((n >= 1) & (step == last))
    def _():
        bm = a_ref.shape[0]
        r = mm(_rows_with_tail(a_ref, tail_ref))
        o_ref[...] = r[:bm].astype(o_ref.dtype)
        otail_ref[...] = r[bm:].astype(otail_ref.dtype)


def _matmul(a, w, *, tail=None, layer=0, col_off=0, n_cols=None, out_dtype=F32, bn=512, bm=1024, w_is_nk=False):
    m, k = a.shape
    n_total = w.shape[-2] if w_is_nk else w.shape[-1]
    n_cols = n_total - col_off if n_cols is None else n_cols
    bn = min(bn, n_cols)
    assert n_cols % bn == 0 and col_off % bn == 0, (n_cols, col_off, bn)
    bm = _row_block(m, bm)
    off = col_off // bn
    out_tile = lambda n: jnp.maximum(n - 1, 0)
    in_specs = [pl.BlockSpec((bm, k), lambda n, i: (_row_step(n, i), 0))]
    out_specs = [pl.BlockSpec((bm, bn), lambda n, i: (_row_step(n, i), out_tile(n)))]
    out_shape = [jax.ShapeDtypeStruct((m, n_cols), out_dtype)]
    args = [a]
    if tail is not None:
        mt = tail.shape[0]
        in_specs.append(pl.BlockSpec((mt, k), lambda n, i: (0, 0)))
        out_specs.append(pl.BlockSpec((mt, bn), lambda n, i: (0, out_tile(n))))
        out_shape.append(jax.ShapeDtypeStruct((mt, n_cols), out_dtype))
        args.append(tail)
    n_tiles, m_steps = n_cols // bn, m // bm
    in_specs.append(_w_slab_spec(w, layer, k, bn, n_tiles, m_steps, off, w_is_nk))
    outs = pl.pallas_call(
        functools.partial(_mm_kernel, tail is not None, w_is_nk),
        grid=(n_tiles + 1, m_steps),
        in_specs=in_specs,
        out_specs=out_specs,
        out_shape=out_shape,
        scratch_shapes=[pltpu.VMEM((2,) + ((bn, k) if w_is_nk else (k, bn)), BF16)],
        compiler_params=_params("arbitrary", "arbitrary"),
        name="matmul",
    )(*args, w)
    return outs[0] if tail is None else tuple(outs)


def _swiglu_kernel(a_ref, tail_ref, wg_ref, wu_ref, wd_ref, o_ref, otail_ref, wdb_ref, wgb_ref, wub_ref):
    n, step = pl.program_id(0), pl.program_id(1)
    last = pl.num_programs(1) - 1

    def stage():
        _stage_slab(wg_ref, wgb_ref)
        _stage_slab(wu_ref, wub_ref)

    bm = a_ref.shape[0]
    sub = min(bm, SWIGLU_SUB_ROWS)
    assert bm % sub == 0

    def gated(rows):
        slot = (n + 1) % 2
        g = jnp.dot(rows, wgb_ref[slot], preferred_element_type=F32)
        u = jnp.dot(rows, wub_ref[slot], preferred_element_type=F32)
        return _silu(g) * u

    def run(with_tail):
        wdb_ref[...] = wd_ref[...].astype(BF16)
        stage()
        for r0 in range(0, bm, sub):
            if with_tail and r0 + sub == bm:
                r = gated(jnp.concatenate([a_ref[r0:, :], tail_ref[...]], axis=0))
                o_ref[r0:, :] = r[:sub].astype(o_ref.dtype)
                otail_ref[...] = r[sub:].astype(otail_ref.dtype)
            else:
                o_ref[r0:r0 + sub, :] = gated(a_ref[r0:r0 + sub, :]).astype(o_ref.dtype)

    @pl.when(n == 0)
    def _():
        stage()

    @pl.when((n >= 1) & (step < last))
    def _():
        run(False)

    @pl.when((n >= 1) & (step == last))
    def _():
        run(True)


def _swiglu_matmul(a, tail, w, w_down, layer, *, bn=256, bm=2048):
    m, k = a.shape
    mt = tail.shape[0]
    d_ff, n_down = w_down.shape[1:]
    assert d_ff % bn == 0
    bm = _row_block(m, bm)
    nb = d_ff // bn
    m_steps = m // bm
    slab = d_ff // (nb * m_steps)
    assert slab * nb * m_steps == d_ff and slab % 16 == 0, (d_ff, nb, m_steps)
    out_tile = lambda n: jnp.maximum(n - 1, 0)
    down_slab = lambda n, i: out_tile(n) * m_steps + _row_step(n, i)
    return pl.pallas_call(
        _swiglu_kernel,
        grid=(nb + 1, m_steps),
        in_specs=[pl.BlockSpec((bm, k), lambda n, i: (_row_step(n, i), 0)),
                  pl.BlockSpec((mt, k), lambda n, i: (0, 0)),
                  _w_slab_spec(w, layer, k, bn, nb, m_steps),
                  _w_slab_spec(w, layer, k, bn, nb, m_steps, nb),
                  pl.BlockSpec((None, slab, n_down), lambda n, i: (layer, down_slab(n, i), 0))],
        out_specs=[pl.BlockSpec((bm, bn), lambda n, i: (_row_step(n, i), out_tile(n))),
                   pl.BlockSpec((mt, bn), lambda n, i: (0, out_tile(n))),
                   pl.BlockSpec((slab, n_down), lambda n, i: (down_slab(n, i), 0))],
        out_shape=[jax.ShapeDtypeStruct((m, d_ff), BF16), jax.ShapeDtypeStruct((mt, d_ff), BF16),
                   jax.ShapeDtypeStruct((d_ff, n_down), BF16)],
        scratch_shapes=[pltpu.VMEM((2, k, bn), BF16), pltpu.VMEM((2, k, bn), BF16)],
        compiler_params=_params("arbitrary", "arbitrary"),
        name="swiglu_matmul",
    )(a, tail, w, w, w_down)


def _mm_bf16w_kernel(a_ref, w_ref, o_ref):
    o_ref[...] = jnp.dot(a_ref[...], w_ref[...], preferred_element_type=F32).astype(o_ref.dtype)


def _matmul_bf16w(a, w, *, bn=512, bm=512):
    m, k = a.shape
    n = w.shape[-1]
    assert n % bn == 0
    bm = _row_block(m, bm)
    return pl.pallas_call(
        _mm_bf16w_kernel,
        grid=(n // bn, m // bm),
        in_specs=[pl.BlockSpec((bm, k), lambda j, i: (i, 0)),
                  pl.BlockSpec((k, bn), lambda j, i: (0, j))],
        out_specs=pl.BlockSpec((bm, bn), lambda j, i: (i, j)),
        out_shape=jax.ShapeDtypeStruct((m, n), BF16),
        compiler_params=_params("parallel", "parallel"),
        name="matmul_bf16w",
    )(a, w)


def _mem_attn_kernel(head_dim, q_ref, k_ref, v_ref, *refs):
    o_ref = refs[-1]
    scale = head_dim ** -0.5
    for h in range(N_MEM_HEADS):
        cols = slice(h * head_dim, (h + 1) * head_dim)
        q = q_ref[:, cols]
        k = k_ref[:, cols].astype(BF16)
        v = v_ref[:, cols].astype(BF16)
        s = lax.dot_general(q, k, (((1,), (1,)), ((), ())), preferred_element_type=F32) * scale
        m = jnp.max(s, axis=-1, keepdims=True)
        p = jnp.exp(s - m)
        p = p * (1.0 / jnp.sum(p, axis=-1, keepdims=True))
        o_ref[:, cols] = jnp.dot(p.astype(BF16), v, preferred_element_type=F32).astype(o_ref.dtype)


def _mem_attn(q, q_col, width, mk, mv, layer, k_col, v_col, dest=None):
    b, l, _ = q.shape
    n_mem = mk.shape[2]
    head_dim = width // N_MEM_HEADS
    bl = _row_block(l, 512)
    in_specs = [pl.BlockSpec((None, bl, width), lambda i, j: (i, j, q_col)),
                pl.BlockSpec((None, None, n_mem, width), lambda i, j: (layer, i, 0, k_col)),
                pl.BlockSpec((None, None, n_mem, width), lambda i, j: (layer, i, 0, v_col))]
    args = [q, mk, mv]
    out_cols, out_col, aliases = width, 0, {}
    if dest is not None:
        out_cols = dest.shape[-1]
        out_col = out_cols // width - 1
        in_specs.append(pl.BlockSpec(memory_space=pl.ANY))
        args.append(dest)
        aliases = {3: 0}
    return pl.pallas_call(
        functools.partial(_mem_attn_kernel, head_dim),
        grid=(b, l // bl),
        in_specs=in_specs,
        out_specs=pl.BlockSpec((None, bl, width), lambda i, j: (i, j, out_col)),
        out_shape=jax.ShapeDtypeStruct((b, l, out_cols), BF16),
        input_output_aliases=aliases,
        compiler_params=_params("parallel", "parallel"),
        name="mem_attn",
    )(*args)


def _tile_rows(cache):
    lyr, b, n_mem, h, hd = cache.shape
    t = hd // LANES
    v = cache.reshape(lyr, b, n_mem, h, t, LANES)
    return jnp.swapaxes(v, 3, 4).reshape(lyr, b, n_mem * t * h, LANES)


def _mem_attn_decode_kernel(n_heads, q_ref, k_ref, v_ref, o_ref):
    tiles = q_ref.shape[-1] // (n_heads * LANES)
    head_dim = tiles * LANES
    group = tiles * n_heads
    n_mem = k_ref.shape[0] // group
    scale = head_dim ** -0.5

    def head_tile(ref, h, t):
        return ref[pl.ds(t * n_heads + h, n_mem, stride=group), :].astype(BF16)

    for h in range(n_heads):
        s = jnp.zeros((q_ref.shape[0], n_mem), F32)
        for t in range(tiles):
            q = q_ref[:, h * head_dim + t * LANES:h * head_dim + (t + 1) * LANES]
            s = s + lax.dot_general(q, head_tile(k_ref, h, t), (((1,), (1,)), ((), ())),
                                    preferred_element_type=F32)
        s = s * scale
        m = jnp.max(s, axis=-1, keepdims=True)
        p = jnp.exp(s - m)
        p = (p * (1.0 / jnp.sum(p, axis=-1, keepdims=True))).astype(BF16)
        for t in range(tiles):
            o = jnp.dot(p, head_tile(v_ref, h, t), preferred_element_type=F32)
            o_ref[:, h * head_dim + t * LANES:h * head_dim + (t + 1) * LANES] = o.astype(o_ref.dtype)


def _mem_attn_decode(q, mk_rows, mv_rows, layer):
    b, l, width = q.shape
    rows = mk_rows.shape[2]
    kv_spec = pl.BlockSpec((None, None, rows, LANES), lambda i: (layer, i, 0, 0))
    return pl.pallas_call(
        functools.partial(_mem_attn_decode_kernel, N_MEM_HEADS),
        grid=(b,),
        in_specs=[pl.BlockSpec((None, l, width), lambda i: (i, 0, 0)), kv_spec, kv_spec],
        out_specs=pl.BlockSpec((None, l, width), lambda i: (i, 0, 0)),
        out_shape=jax.ShapeDtypeStruct((b, l, width), BF16),
        compiler_params=_params("parallel"),
        name="mem_attn_decode",
    )(q, mk_rows, mv_rows)


def _sink_softmax(s, sink):
    m = jnp.maximum(jnp.max(s, axis=-1, keepdims=True), sink)
    p = jnp.exp(s - m)
    return p * (1.0 / (jnp.sum(p, axis=-1, keepdims=True) + jnp.exp(sink - m)))


def _swa_prompt_kernel(slopes, sink_ref, q_ref, kp_ref, kc_ref, vp_ref, vc_ref, o_ref):
    w = WINDOW
    blk = pl.program_id(1)
    qi = lax.broadcasted_iota(jnp.int32, (w, 2 * w), 0)
    kj = lax.broadcasted_iota(jnp.int32, (w, 2 * w), 1)
    dist = w + qi - kj
    valid = (dist >= 0) & (dist < WINDOW) & ((blk - 1) * w + kj >= 0)
    distf = dist.astype(F32)
    scale = HEAD_DIM ** -0.5
    group = len(slopes) // N_KV_HEADS
    for hk in range(N_KV_HEADS):
        kcols = slice(hk * HEAD_DIM, (hk + 1) * HEAD_DIM)
        kcat = jnp.concatenate([kp_ref[:, kcols], kc_ref[:, kcols]], axis=0).astype(BF16)
        vcat = jnp.concatenate([vp_ref[:, kcols], vc_ref[:, kcols]], axis=0).astype(BF16)
        for g in range(group):
            h = hk * group + g
            cols = slice(h * HEAD_DIM, (h + 1) * HEAD_DIM)
            s = lax.dot_general(q_ref[:, cols], kcat, (((1,), (1,)), ((), ())),
                                preferred_element_type=F32) * scale
            s = s - float(slopes[h]) * distf
            s = jnp.where(valid, s, -jnp.inf)
            p = _sink_softmax(s, sink_ref[h])
            o_ref[:, cols] = jnp.dot(p.astype(BF16), vcat, preferred_element_type=F32).astype(o_ref.dtype)


def _swa_prompt(q, kv, sinks, slopes, out_cols):
    b, l, _ = q.shape
    w = WINDOW
    kvw = N_KV_HEADS * HEAD_DIM
    tw = len(slopes) * HEAD_DIM
    assert l % w == 0
    prev = lambda i, j: jnp.maximum(j - 1, 0)
    return pl.pallas_call(
        functools.partial(_swa_prompt_kernel, slopes),
        grid=(b, l // w),
        in_specs=[pl.BlockSpec(memory_space=pltpu.SMEM),
                  pl.BlockSpec((None, w, tw), lambda i, j: (i, j, 0)),
                  pl.BlockSpec((None, w, kvw), lambda i, j: (i, prev(i, j), 0)),
                  pl.BlockSpec((None, w, kvw), lambda i, j: (i, j, 0)),
                  pl.BlockSpec((None, w, kvw), lambda i, j: (i, prev(i, j), 1)),
                  pl.BlockSpec((None, w, kvw), lambda i, j: (i, j, 1))],
        out_specs=pl.BlockSpec((None, w, tw), lambda i, j: (i, j, 0)),
        out_shape=jax.ShapeDtypeStruct((b, l, out_cols), BF16),
        compiler_params=_params("parallel", "parallel"),
        name="swa_prompt",
    )(sinks, q, kv, kv, kv, kv)


def _swa_sample_kernel(slopes, sink_ref, q_ref, k_ref, v_ref, o_ref):
    n_heads = len(slopes)
    group = n_heads // N_KV_HEADS
    wb = k_ref.shape[0] // N_KV_HEADS
    q = q_ref[...].astype(BF16)

    def kv_head(ref, hk):
        return ref[pl.ds(hk, wb, stride=N_KV_HEADS), :].astype(BF16)

    head = lax.broadcasted_iota(jnp.int32, (n_heads, wb), 0)
    pos = lax.broadcasted_iota(jnp.int32, (n_heads, wb), 1)
    distf = (wb - 1 - pos).astype(F32)
    slope = jnp.zeros((n_heads, wb), F32)
    sink = jnp.zeros((n_heads, 1), F32)
    head_col = lax.broadcasted_iota(jnp.int32, (n_heads, 1), 0)
    for h in range(n_heads):
        slope = jnp.where(head == h, float(slopes[h]), slope)
        sink = jnp.where(head_col == h, sink_ref[h], sink)
    scale = HEAD_DIM ** -0.5
    s = jnp.zeros((n_heads, wb), F32)
    for hk in range(N_KV_HEADS):
        s_hk = lax.dot_general(q, kv_head(k_ref, hk), (((1,), (1,)), ((), ())), preferred_element_type=F32)
        s = jnp.where(head // group == hk, s_hk, s)
    s = s * scale - slope * distf
    p = _sink_softmax(s, sink).astype(BF16)
    o = jnp.zeros((n_heads, HEAD_DIM), F32)
    head_o = lax.broadcasted_iota(jnp.int32, (n_heads, HEAD_DIM), 0)
    for hk in range(N_KV_HEADS):
        o_hk = jnp.dot(p, kv_head(v_ref, hk), preferred_element_type=F32)
        o = jnp.where(head_o // group == hk, o_hk, o)
    o_ref[...] = o.astype(o_ref.dtype)


def _swa_sample(q, win_k, win_v, sinks, slopes):
    b, n_heads, _ = q.shape
    rows = win_k.shape[1] * N_KV_HEADS
    win_k = win_k.reshape(b, rows, HEAD_DIM)
    win_v = win_v.reshape(b, rows, HEAD_DIM)
    return pl.pallas_call(
        functools.partial(_swa_sample_kernel, slopes),
        grid=(b,),
        in_specs=[pl.BlockSpec(memory_space=pltpu.SMEM),
                  pl.BlockSpec((None, n_heads, HEAD_DIM), lambda i: (i, 0, 0)),
                  pl.BlockSpec((None, rows, HEAD_DIM), lambda i: (i, 0, 0)),
                  pl.BlockSpec((None, rows, HEAD_DIM), lambda i: (i, 0, 0))],
        out_specs=pl.BlockSpec((None, n_heads, HEAD_DIM), lambda i: (i, 0, 0)),
        out_shape=jax.ShapeDtypeStruct((b, n_heads, HEAD_DIM), BF16),
        compiler_params=_params("parallel"),
        name="swa_sample",
    )(sinks, q, win_k, win_v)


def _lane_pick(x, lane):
    idx = lax.broadcasted_iota(jnp.int32, x.shape, x.ndim - 1)
    return jnp.sum(jnp.where(idx == lane, x, 0.0), axis=-1, keepdims=True)


def _gates(ba, alog, dtb):
    beta = jax.nn.sigmoid(ba)
    g = -jnp.exp(alog) * _softplus(ba + dtb)
    return beta, g


def _bdot(a, b, contract_b_last, precision=None):
    dims = (((2,), (2 if contract_b_last else 1,)), ((0,), (0,)))
    return lax.dot_general(a, b, dims, preferred_element_type=F32, precision=precision)


def _delta_prompt_kernel(n_heads, hb, q_ref, k_ref, v_ref, z_ref, ba_ref, wq_ref, wk_ref, wv_ref,
                         alog_ref, dtb_ref, gn_ref, o_ref, s_ref, pad_ref, beta_ref, g_ref):
    l = q_ref.shape[0]
    c = DELTA_CHUNK
    n = l // c
    hd = HEAD_DIM
    hstep = pl.program_id(1)

    @pl.when(hstep == 0)
    def _():
        beta_all, g_all = _gates(ba_ref[...], alog_ref[...], dtb_ref[...])
        beta_ref[...] = beta_all
        g_ref[...] = g_all

    pad_ref[0:8, :] = jnp.zeros((8, hd), F32)

    def conv(x_ref, w_ref, cols):
        pad_ref[8:8 + l, :] = x_ref[:, cols]
        base = 8 - (CONV_W - 1)
        out = pad_ref[base:base + l, :] * w_ref[0:1, cols]
        for j in range(1, CONV_W):
            out = out + pad_ref[base + j:base + j + l, :] * w_ref[j:j + 1, cols]
        return _silu(out)

    def l2norm(x):
        return x * lax.rsqrt(jnp.sum(x * x, axis=-1, keepdims=True) + EPS)

    row = lax.broadcasted_iota(jnp.int32, (1, c, c), 1)
    col = lax.broadcasted_iota(jnp.int32, (1, c, c), 2)
    incl = row >= col
    strict = row > col
    eye = row == col

    def lower_left(shift):
        return (((row >> (shift + 1)) == (col >> (shift + 1)))
                & (((row >> shift) & 1) == 1) & (((col >> shift) & 1) == 0))

    def prepare(hh):
        cols = slice(hh * hd, (hh + 1) * hd)
        head = hstep * hb + hh
        q3 = (l2norm(conv(q_ref, wq_ref, cols)) * (hd ** -0.5)).reshape(n, c, hd)
        k3 = l2norm(conv(k_ref, wk_ref, cols)).reshape(n, c, hd)
        v3 = conv(v_ref, wv_ref, cols).reshape(n, c, hd)
        beta = _lane_pick(beta_ref[...], head).reshape(n, c, 1)
        g = _lane_pick(g_ref[...], n_heads + head).reshape(n, c, 1)
        gc_row = jnp.sum(jnp.where(row <= col, jnp.broadcast_to(g, (n, c, c)), 0.0), axis=1, keepdims=True)
        gc_col = jnp.sum(jnp.where(eye, jnp.broadcast_to(gc_row, (n, c, c)), 0.0), axis=2, keepdims=True)
        decay = jnp.where(incl, jnp.exp(jnp.where(incl, gc_col - gc_row, 0.0)), 0.0)

        kb = k3.astype(BF16)
        qkk = _bdot(jnp.concatenate([q3, k3], axis=1).astype(BF16), kb, True)
        qk = qkk[:, :c] * decay
        lmat = jnp.where(strict, qkk[:, c:] * decay * beta, 0.0)
        tinv = jnp.where(eye, 1.0, 0.0) - jnp.where(lower_left(0), lmat, 0.0)
        for shift in range(1, int(math.log2(c))):
            cs = jnp.where(lower_left(shift), lmat, 0.0).astype(BF16)
            tb = tinv.astype(BF16)
            tinv = tinv - _bdot(_bdot(tb, cs, False).astype(BF16), tb, False)
        egc = jnp.exp(gc_col)
        rhs = jnp.concatenate([v3 * beta, k3 * (beta * egc)], axis=2).astype(BF16)
        uw = _bdot(tinv.astype(BF16), rhs, False)
        g_last = gc_col[:, c - 1:c, :]
        return dict(
            u=uw[:, :, :hd],
            wq=jnp.concatenate([uw[:, :, hd:], q3 * egc], axis=1).astype(BF16),
            qk=qk.astype(BF16),
            kt=(k3 * jnp.exp(g_last - gc_col)).astype(BF16),
            eg_last=jnp.exp(g_last))

    heads = [prepare(hh) for hh in range(hb)]
    states = [jnp.zeros((hd, hd), F32) for _ in range(hb)]
    for i in range(n):
        for hh, p in enumerate(heads):
            cols = slice(hh * hd, (hh + 1) * hd)
            tok = slice(i * c, (i + 1) * c)
            sb = states[hh].astype(BF16)
            ws = jnp.dot(p["wq"][i], sb, preferred_element_type=F32)
            vnb = (p["u"][i] - ws[:c]).astype(BF16)
            o = ws[c:] + jnp.dot(p["qk"][i], vnb, preferred_element_type=F32)
            states[hh] = states[hh] * p["eg_last"][i] + lax.dot_general(
                p["kt"][i], vnb, (((0,), (0,)), ((), ())), preferred_element_type=F32)
            o = o * lax.rsqrt(jnp.mean(o * o, axis=-1, keepdims=True) + EPS) * gn_ref[...]
            o_ref[tok, cols] = (o * _silu(z_ref[tok, cols])).astype(o_ref.dtype)
    for hh in range(hb):
        s_ref[hh] = states[hh]


def _delta_prompt(proj, ba, w_conv, layer, alog_pad, dtb_pad, gate_norm, n_heads, out_cols):
    b, l, _ = proj.shape
    hd = HEAD_DIM
    tw = n_heads * hd
    hb = DELTA_HEADS_PER_STEP
    assert n_heads % hb == 0
    steps = n_heads // hb
    col = lambda part: pl.BlockSpec((None, l, hb * hd), lambda i, h: (i, 0, part * steps + h))
    wcol = lambda part: pl.BlockSpec((None, CONV_W, hb * hd), lambda i, h: (layer, 0, part * steps + h))
    vec = pl.BlockSpec((1, LANES), lambda i, h: (0, 0))
    return pl.pallas_call(
        functools.partial(_delta_prompt_kernel, n_heads, hb),
        grid=(b, steps),
        in_specs=[col(0), col(1), col(2), col(3),
                  pl.BlockSpec((None, l, LANES), lambda i, h: (i, 0, 0)),
                  wcol(0), wcol(1), wcol(2), vec, vec, vec],
        out_specs=[pl.BlockSpec((None, l, hb * hd), lambda i, h: (i, 0, h)),
                   pl.BlockSpec((None, hb, hd, hd), lambda i, h: (i, h, 0, 0))],
        out_shape=[jax.ShapeDtypeStruct((b, l, out_cols), BF16),
                   jax.ShapeDtypeStruct((b, n_heads, hd, hd), F32)],
        scratch_shapes=[pltpu.VMEM((l + 8, hd), F32), pltpu.VMEM((l, LANES), F32), pltpu.VMEM((l, LANES), F32)],
        compiler_params=_params("parallel", "arbitrary"),
        name="delta_prompt",
    )(proj, proj, proj, proj, ba, w_conv, w_conv, w_conv, alog_pad, dtb_pad, gate_norm)


def _delta_sample_kernel(n_heads, x_ref, z_ref, ba_ref, hist_ref, wc_ref, s_ref, alog_ref, dtb_ref, gn_ref, *refs):
    o_ref, hist_o_ref, s_o_ref = refs[-3:]
    hd = HEAD_DIM
    tw = n_heads * hd
    x = x_ref[...]
    acc = hist_ref[0:1, :] * wc_ref[0:1, :]
    for j in range(1, CONV_W - 1):
        acc = acc + hist_ref[j:j + 1, :] * wc_ref[j:j + 1, :]
    acc = _silu(acc + x * wc_ref[CONV_W - 1:CONV_W, :])
    hist_o_ref[0:CONV_W - 2, :] = hist_ref[1:CONV_W - 1, :]
    hist_o_ref[CONV_W - 2:CONV_W - 1, :] = x

    beta_all, g_all = _gates(ba_ref[...], alog_ref[...], dtb_ref[...])

    def head_columns(off, scale):
        rows = jnp.concatenate([acc[:, off + h * hd: off + (h + 1) * hd] for h in range(n_heads)]
                               + [jnp.zeros((hd - n_heads, hd), F32)], axis=0)
        rows = rows * (lax.rsqrt(jnp.sum(rows * rows, axis=-1, keepdims=True) + EPS) * scale)
        return rows.T

    def per_head(make):
        return jnp.stack([make(h) for h in range(n_heads)], axis=0)

    q_cols = head_columns(0, hd ** -0.5)
    k_cols = head_columns(tw, 1.0)
    qc = per_head(lambda h: q_cols[:, h:h + 1])
    kc = per_head(lambda h: k_cols[:, h:h + 1])
    v = per_head(lambda h: acc[:, 2 * tw + h * hd: 2 * tw + (h + 1) * hd])
    z = per_head(lambda h: z_ref[:, h * hd:(h + 1) * hd])
    beta = per_head(lambda h: _lane_pick(beta_all, h))
    g = per_head(lambda h: _lane_pick(g_all, n_heads + h))
    s = s_ref[...] * jnp.exp(g)
    vn = beta * (v - jnp.sum(s * kc, axis=1, keepdims=True))
    s = s + kc * vn
    s_o_ref[...] = s
    o = jnp.sum(s * qc, axis=1, keepdims=True)
    o = o * lax.rsqrt(jnp.mean(o * o, axis=-1, keepdims=True) + EPS) * gn_ref[...]
    o = (o * _silu(z)).astype(o_ref.dtype)
    for h in range(n_heads):
        o_ref[:, h * hd:(h + 1) * hd] = o[h]


def _delta_sample(proj, ba, state_conv, w_conv, state_delta, layer, alog_pad, dtb_pad, gate_norm, n_heads,
                  new_states=None):
    b = proj.shape[0]
    hd = HEAD_DIM
    tw = n_heads * hd
    hist = CONV_W - 1
    vec = pl.BlockSpec((1, LANES), lambda i: (0, 0))
    in_specs = [pl.BlockSpec((None, 1, 3 * tw), lambda i: (i, 0, 0)),
                pl.BlockSpec((None, 1, tw), lambda i: (i, 0, 3)),
                pl.BlockSpec((None, 1, LANES), lambda i: (i, 0, 0)),
                pl.BlockSpec((None, None, hist, 3 * tw), lambda i: (layer, i, 0, 0)),
                pl.BlockSpec((None, CONV_W, 3 * tw), lambda i: (layer, 0, 0)),
                pl.BlockSpec((None, None, n_heads, hd, hd), lambda i: (layer, i, 0, 0, 0)),
                vec, vec, vec]
    args = [proj, proj, ba, state_conv, w_conv, state_delta, alog_pad, dtb_pad, gate_norm]
    aliases = {}
    if new_states is not None:
        in_specs.append(pl.BlockSpec(memory_space=pl.ANY))
        args.append(new_states)
        aliases = {len(args) - 1: 2}
    return pl.pallas_call(
        functools.partial(_delta_sample_kernel, n_heads),
        grid=(b,),
        in_specs=in_specs,
        out_specs=[pl.BlockSpec((None, 1, tw), lambda i: (i, 0, 0)),
                   pl.BlockSpec((None, hist, 3 * tw), lambda i: (i, 0, 0)),
                   pl.BlockSpec((None, None, n_heads, hd, hd), lambda i: (layer, i, 0, 0, 0))],
        out_shape=[jax.ShapeDtypeStruct((b, 1, tw), BF16),
                   jax.ShapeDtypeStruct((b, hist, 3 * tw), F32),
                   jax.ShapeDtypeStruct(state_delta.shape, F32)],
        input_output_aliases=aliases,
        compiler_params=_params("parallel"),
        name="delta_sample",
    )(*args)


def _trunk(x_p, x_s, mem_kv_p, cache_mk, cache_mv, conv_state, delta_state, buf_k, buf_v, wts):
    bp, lp, d = x_p.shape
    bs, ls, _ = x_s.shape
    assert ls == 1
    depth = wts["w_out"].shape[0]
    n_a = wts["w_in_a_t"].shape[0]
    d_ff = wts["w_down"].shape[1]
    tw = wts["w_conv"].shape[-1] // 3
    n_heads = tw // HEAD_DIM
    mem_w = d - tw
    kvw = N_KV_HEADS * HEAD_DIM
    slopes = _alibi_slopes(n_heads)
    off_b = 4 * tw
    off_qm = off_b + 2 * n_heads

    def lane_pad(vals):
        return jnp.zeros((1, LANES), F32).at[0, n_heads:2 * n_heads].set(vals.astype(F32))

    def resid(xp, fp, xs, fs, w_post, w_next):
        xp, np_ = _resid_norm(xp, fp, w_post, w_next)
        xs, ns_ = _resid_norm(xs, fs, w_post, w_next)
        return xp, xs, np_, ns_

    xp2 = x_p.reshape(bp * lp, d)
    xs2 = x_s.reshape(bs, d)
    (xn_p,) = _norm_cast(xp2, wts["norm_mix_pre"][0:1])
    (xn_s,) = _norm_cast(xs2, wts["norm_mix_pre"][0:1])
    conv_p, conv_s, delta_p = [], [], []
    delta_s = None
    kv_p = win_k = win_v = None
    for layer in range(depth):
        if layer < n_a:
            wt = wts["w_in_a_t"]
            proj_p, proj_s = _matmul(xn_p, wt, tail=xn_s, layer=layer, n_cols=off_b, w_is_nk=True, bn=WIDE_TILE)
            w_ba = jnp.pad(wt[layer, off_b:off_qm, :], ((0, LANES - 2 * n_heads), (0, 0)))
            ba_p, ba_s = _matmul(xn_p, w_ba, tail=xn_s, w_is_nk=True)
            qm_p, qm_s = _matmul(xn_p, wt[layer, off_qm:, :], tail=xn_s, out_dtype=BF16, w_is_nk=True,
                                 bn=WIDE_TILE)
            alog_pad = lane_pad(wts["a_log"][layer])
            dtb_pad = lane_pad(wts["dt_bias"][layer])
            gate_norm = wts["w_gate_norm"][layer].reshape(1, HEAD_DIM)
            proj_p = proj_p.reshape(bp, lp, off_b)
            tok_p, s_p = _delta_prompt(proj_p, ba_p.reshape(bp, lp, LANES), wts["w_conv"], layer,
                                       alog_pad, dtb_pad, gate_norm, n_heads, d)
            conv_p.append(proj_p[:, lp - (CONV_W - 1):, :3 * tw])
            tok_s, hist_s, delta_s = _delta_sample(proj_s.reshape(bs, 1, off_b), ba_s.reshape(bs, 1, LANES),
                                                   conv_state, wts["w_conv"], delta_state, layer, alog_pad, dtb_pad,
                                                   gate_norm, n_heads, new_states=delta_s)
            conv_s.append(hist_s)
            delta_p.append(s_p)
            mixed_p = _mem_attn(qm_p.reshape(bp, lp, mem_w), 0, mem_w, mem_kv_p, mem_kv_p, layer, 0, 1, dest=tok_p)
            mo_s = _mem_attn_decode(qm_s.reshape(bs, 1, mem_w), cache_mk, cache_mv, layer)
        else:
            lb = layer - n_a
            sinks = wts["sinks"][lb].astype(F32)
            proj_p, proj_s = _matmul(xn_p, wts["w_in_b"], tail=xn_s, layer=lb, out_dtype=BF16, bn=WIDE_TILE)
            proj_p = proj_p.reshape(bp, lp, d)
            tok_p = _swa_prompt(proj_p, kv_p, sinks, slopes, d)
            mixed_p = _mem_attn(proj_p, tw // mem_w, mem_w, mem_kv_p, mem_kv_p, layer, 0, 1, dest=tok_p)
            q_s = proj_s[:, :tw].reshape(bs, n_heads, HEAD_DIM)
            tok_s = _swa_sample(q_s, win_k, win_v, sinks, slopes).reshape(bs, 1, tw)
            mo_s = _mem_attn_decode(proj_s[:, tw:].reshape(bs, 1, mem_w), cache_mk, cache_mv, layer)
        mixed_p = mixed_p.reshape(bp * lp, d)
        mixed_s = jnp.concatenate([tok_s, mo_s], axis=-1).reshape(bs, d)
        mix_p, mix_s = _matmul(mixed_p, wts["w_out"], tail=mixed_s, layer=layer, out_dtype=BF16, bn=WIDE_TILE)
        xp2, xs2, (hn_p,), (hn_s,) = resid(xp2, mix_p, xs2, mix_s, wts["norm_mix_post"][layer],
                                           wts["norm_ffn_pre"][layer:layer + 1])
        hid_p, hid_s, w_down_bf16 = _swiglu_matmul(hn_p, hn_s, wts["w_gate_up"], wts["w_down"], layer)
        f_p = _matmul_bf16w(hid_p, w_down_bf16)
        f_s = _matmul_bf16w(hid_s, w_down_bf16)
        w_post = wts["norm_ffn_post"][layer]
        if layer + 1 == depth:
            xp2, xs2, _, _ = resid(xp2, f_p, xs2, f_s, w_post, None)
        elif layer + 1 == n_a:
            w_next = jnp.stack([wts["norm_mix_pre"][layer + 1], wts["norm_kv"]])
            xp2, xs2, (xn_p, xkv_p), (xn_s, xkv_s) = resid(xp2, f_p, xs2, f_s, w_post, w_next)
            kv_p, kv_s = _matmul(xkv_p, wts["w_kv"], tail=xkv_s, bn=WIDE_TILE)
            kv_p = kv_p.reshape(bp, lp, 2 * kvw)
            new_row = lambda t: t.reshape(bs, 1, N_KV_HEADS, HEAD_DIM)
            win_k = jnp.concatenate([buf_k[:, 1:], new_row(kv_s[:, :kvw])], axis=1)
            win_v = jnp.concatenate([buf_v[:, 1:], new_row(kv_s[:, kvw:])], axis=1)
        else:
            xp2, xs2, (xn_p,), (xn_s,) = resid(xp2, f_p, xs2, f_s, w_post,
                                               wts["norm_mix_pre"][layer + 1:layer + 2])
    wp = min(WINDOW, lp)
    heads = lambda t: t.reshape(t.shape[0], -1, N_KV_HEADS, HEAD_DIM)
    return (xp2.reshape(bp, lp, d), xs2.reshape(bs, 1, d),
            heads(kv_p[:, lp - wp:, :kvw]), heads(kv_p[:, lp - wp:, kvw:]), jnp.stack(conv_p), jnp.stack(delta_p),
            heads(win_k), heads(win_v), jnp.stack(conv_s), delta_s)


def kernel(x_prompt, x_sample, cache_mem_k, cache_mem_v, cache_swa_k, cache_swa_v, state_conv, state_delta, mem_prompt, w_in_a, w_conv, a_log, dt_bias, w_gate_norm, w_in_b, sinks, norm_kv, w_kv, norm_mem, w_mem_kv, w_out, norm_mix_pre, norm_mix_post, norm_ffn_pre, norm_ffn_post, w_gate_up, w_down):
    wts = dict(w_in_a_t=jnp.swapaxes(w_in_a, 1, 2), w_conv=w_conv, a_log=a_log, dt_bias=dt_bias,
               w_gate_norm=w_gate_norm, w_in_b=w_in_b, sinks=sinks, norm_kv=norm_kv, w_kv=w_kv, w_out=w_out,
               norm_mix_pre=norm_mix_pre, norm_mix_post=norm_mix_post, norm_ffn_pre=norm_ffn_pre,
               norm_ffn_post=norm_ffn_post, w_gate_up=w_gate_up, w_down=w_down)
    depth = w_out.shape[0]
    bp, n_mem, d = mem_prompt.shape
    bs = x_sample.shape[0]
    mem_w = w_mem_kv.shape[-1] // 2
    mem_shape = (depth, bp, n_mem, N_MEM_HEADS, mem_w // N_MEM_HEADS)

    memn = _norm_cast(mem_prompt.reshape(bp * n_mem, d), norm_mem)
    mem_kv = jnp.stack([_matmul(memn[i], w_mem_kv, layer=i) for i in range(depth)])
    mem_kv = mem_kv.reshape(depth, bp, n_mem, 2 * mem_w)
    mem_k_p = mem_kv[..., :mem_w].reshape(mem_shape)
    mem_v_p = mem_kv[..., mem_w:].reshape(mem_shape)
    cmk = _tile_rows(cache_mem_k)
    cmv = _tile_rows(cache_mem_v)
    y_p, y_s, swk_p, swv_p, conv_p, delta_p, swk_s, swv_s, conv_s, delta_s = _trunk(
        x_prompt, x_sample, mem_kv, cmk, cmv, state_conv, state_delta, cache_swa_k, cache_swa_v, wts)
    return (y_p, y_s, mem_k_p, mem_v_p, swk_p, swv_p, conv_p, delta_p, swk_s, swv_s, conv_s, delta_s)
```

```python
import functools
import math

import numpy as np
import jax
import jax.numpy as jnp
from jax import lax
from jax.experimental import pallas as pl
from jax.experimental.pallas import tpu as pltpu

F32 = jnp.float32
BF16 = jnp.bfloat16
EPS = 1e-6

HEAD_DIM = 128
N_MEM_HEADS = 4
N_KV_HEADS = 8
WINDOW = 128
CONV_W = 4
DELTA_CHUNK = 128
DELTA_HEADS_PER_STEP = 2
LANES = 128
VMEM_LIMIT_BYTES = 56 * 1024 * 1024
SWIGLU_SUB_ROWS = 1024
WIDE_TILE = 1024


def _params(*sem):
    return pltpu.CompilerParams(dimension_semantics=sem, vmem_limit_bytes=VMEM_LIMIT_BYTES)


def _alibi_slopes(n):
    def pow2_slopes(m):
        start = 2.0 ** (-8.0 / m)
        return [start ** (i + 1) for i in range(m)]
    c = 2 ** int(math.floor(math.log2(n)))
    s = pow2_slopes(c)
    if c < n:
        s = s + pow2_slopes(2 * c)[0::2][: n - c]
    return np.asarray(s, np.float32)


def _rms_rows(x, w):
    return x * lax.rsqrt(jnp.mean(x * x, axis=-1, keepdims=True) + EPS) * w


def _silu(x):
    return x * jax.nn.sigmoid(x)


def _softplus(x):
    return jnp.maximum(x, 0.0) + jnp.log1p(jnp.exp(-jnp.abs(x)))


def _row_block(m, target):
    b = min(m, target)
    assert m % b == 0, (m, b)
    return b


def _norm_kernel(x_ref, w_ref, *o_refs):
    x = x_ref[...]
    for j, o_ref in enumerate(o_refs):
        o_ref[...] = _rms_rows(x, w_ref[j:j + 1, :]).astype(o_ref.dtype)


def _norm_cast(x, ws):
    m, d = x.shape
    n = ws.shape[0]
    br = _row_block(m, 256)
    return pl.pallas_call(
        _norm_kernel,
        grid=(m // br,),
        in_specs=[pl.BlockSpec((br, d), lambda i: (i, 0)),
                  pl.BlockSpec((n, d), lambda i: (0, 0))],
        out_specs=[pl.BlockSpec((br, d), lambda i: (i, 0))] * n,
        out_shape=[jax.ShapeDtypeStruct((m, d), BF16)] * n,
        compiler_params=_params("parallel"),
        name="norm_cast",
    )(x, ws)


def _resid_kernel(n_next, x_ref, f_ref, wpost_ref, *refs):
    if n_next:
        wnext_ref, xo_ref = refs[0], refs[1]
        xn_refs = refs[2:]
    else:
        xo_ref = refs[0]
        xn_refs = ()
    xnew = x_ref[...] + _rms_rows(f_ref[...].astype(F32), wpost_ref[...])
    xo_ref[...] = xnew
    for j, o_ref in enumerate(xn_refs):
        o_ref[...] = _rms_rows(xnew, wnext_ref[j:j + 1, :]).astype(o_ref.dtype)


def _resid_norm(x, f, w_post, w_next):
    m, d = x.shape
    n_next = 0 if w_next is None else w_next.shape[0]
    br = _row_block(m, 256)
    row = pl.BlockSpec((br, d), lambda i: (i, 0))
    in_specs = [row, row, pl.BlockSpec((1, d), lambda i: (0, 0))]
    args = [x, f, w_post.reshape(1, d)]
    if n_next:
        in_specs.append(pl.BlockSpec((n_next, d), lambda i: (0, 0)))
        args.append(w_next)
    outs = pl.pallas_call(
        functools.partial(_resid_kernel, n_next),
        grid=(m // br,),
        in_specs=in_specs,
        out_specs=[row] * (1 + n_next),
        out_shape=[jax.ShapeDtypeStruct((m, d), F32)] + [jax.ShapeDtypeStruct((m, d), BF16)] * n_next,
        compiler_params=_params("parallel"),
        name="resid_norm",
    )(*args)
    return outs[0], list(outs[1:])


def _w_slab_spec(w, layer, k, bn, n_tiles, m_steps, first=0, w_is_nk=False):
    tile = lambda n: first + jnp.minimum(n, n_tiles - 1)
    if w_is_nk:
        shape, pos = (bn // m_steps, k), (lambda n, m: (tile(n) * m_steps + m, 0))
    else:
        shape, pos = (k // m_steps, bn), (lambda n, m: (m, tile(n)))
    assert shape[0] * m_steps == (bn if w_is_nk else k) and shape[0] % 16 == 0, (k, bn, m_steps)
    if w.ndim == 3:
        return pl.BlockSpec((None,) + shape, lambda n, m: (layer,) + pos(n, m))
    return pl.BlockSpec(shape, lambda n, m: pos(n, m))


def _stage_slab(w_ref, wb_ref):
    n, m = pl.program_id(0), pl.program_id(1)
    slab = w_ref.shape[0]
    rows = pl.ds(pl.multiple_of(m * slab, slab), slab)
    wb_ref[n % 2, rows, :] = w_ref[...].astype(BF16)


def _row_step(n, m):
    return jnp.where(n >= 1, m, 0)


def _rows_with_tail(a_ref, tail_ref):
    return jnp.concatenate([a_ref[...], tail_ref[...]], axis=0)


def _mm_kernel(has_tail, w_is_nk, *refs):
    if has_tail:
        a_ref, tail_ref, w_ref, o_ref, otail_ref, wb_ref = refs
    else:
        a_ref, w_ref, o_ref, wb_ref = refs
    n, step = pl.program_id(0), pl.program_id(1)

    def mm(rows):
        _stage_slab(w_ref, wb_ref)
        wb = wb_ref[(n + 1) % 2]
        if w_is_nk:
            return lax.dot_general(rows, wb, (((1,), (1,)), ((), ())), preferred_element_type=F32)
        return jnp.dot(rows, wb, preferred_element_type=F32)

    @pl.when(n == 0)
    def _():
        _stage_slab(w_ref, wb_ref)

    if not has_tail:
        @pl.when(n >= 1)
        def _():
            o_ref[...] = mm(a_ref[...]).astype(o_ref.dtype)
        return
    last = pl.num_programs(1) - 1

    @pl.when((n >= 1) & (step < last))
    def _():
        o_ref[...] = mm(a_ref[...]).astype(o_ref.dtype)

    @pl.when((n >= 1) & (step == last))
    def _():
        bm = a_ref.shape[0]
        r = mm(_rows_with_tail(a_ref, tail_ref))
        o_ref[...] = r[:bm].astype(o_ref.dtype)
        otail_ref[...] = r[bm:].astype(otail_ref.dtype)


def _matmul(a, w, *, tail=None, layer=0, col_off=0, n_cols=None, out_dtype=F32, bn=512, bm=1024, w_is_nk=False):
    m, k = a.shape
    n_total = w.shape[-2] if w_is_nk else w.shape[-1]
    n_cols = n_total - col_off if n_cols is None else n_cols
    bn = min(bn, n_cols)
    assert n_cols % bn == 0 and col_off % bn == 0, (n_cols, col_off, bn)
    bm = _row_block(m, bm)
    off = col_off // bn
    out_tile = lambda n: jnp.maximum(n - 1, 0)
    in_specs = [pl.BlockSpec((bm, k), lambda n, i: (_row_step(n, i), 0))]
    out_specs = [pl.BlockSpec((bm, bn), lambda n, i: (_row_step(n, i), out_tile(n)))]
    out_shape = [jax.ShapeDtypeStruct((m, n_cols), out_dtype)]
    args = [a]
    if tail is not None:
        mt = tail.shape[0]
        in_specs.append(pl.BlockSpec((mt, k), lambda n, i: (0, 0)))
        out_specs.append(pl.BlockSpec((mt, bn), lambda n, i: (0, out_tile(n))))
        out_shape.append(jax.ShapeDtypeStruct((mt, n_cols), out_dtype))
        args.append(tail)
    n_tiles, m_steps = n_cols // bn, m // bm
    in_specs.append(_w_slab_spec(w, layer, k, bn, n_tiles, m_steps, off, w_is_nk))
    outs = pl.pallas_call(
        functools.partial(_mm_kernel, tail is not None, w_is_nk),
        grid=(n_tiles + 1, m_steps),
        in_specs=in_specs,
        out_specs=out_specs,
        out_shape=out_shape,
        scratch_shapes=[pltpu.VMEM((2,) + ((bn, k) if w_is_nk else (k, bn)), BF16)],
        compiler_params=_params("arbitrary", "arbitrary"),
        name="matmul",
    )(*args, w)
    return outs[0] if tail is None else tuple(outs)


def _swiglu_kernel(a_ref, tail_ref, wg_ref, wu_ref, wd_ref, o_ref, otail_ref, wdb_ref, wgb_ref, wub_ref):
    n, step = pl.program_id(0), pl.program_id(1)
    last = pl.num_programs(1) - 1

    def stage():
        _stage_slab(wg_ref, wgb_ref)
        _stage_slab(wu_ref, wub_ref)

    bm = a_ref.shape[0]
    sub = min(bm, SWIGLU_SUB_ROWS)
    assert bm % sub == 0

    def gated(rows):
        slot = (n + 1) % 2
        g = jnp.dot(rows, wgb_ref[slot], preferred_element_type=F32)
        u = jnp.dot(rows, wub_ref[slot], preferred_element_type=F32)
        return _silu(g) * u

    def run(with_tail):
        wdb_ref[...] = wd_ref[...].astype(BF16)
        stage()
        for r0 in range(0, bm, sub):
            if with_tail and r0 + sub == bm:
                r = gated(jnp.concatenate([a_ref[r0:, :], tail_ref[...]], axis=0))
                o_ref[r0:, :] = r[:sub].astype(o_ref.dtype)
                otail_ref[...] = r[sub:].astype(otail_ref.dtype)
            else:
                o_ref[r0:r0 + sub, :] = gated(a_ref[r0:r0 + sub, :]).astype(o_ref.dtype)

    @pl.when(n == 0)
    def _():
        stage()

    @pl.when((n >= 1) & (step < last))
    def _():
        run(False)

    @pl.when((n >= 1) & (step == last))
    def _():
        run(True)


def _swiglu_matmul(a, tail, w, w_down, layer, *, bn=256, bm=2048):
    m, k = a.shape
    mt = tail.shape[0]
    d_ff, n_down = w_down.shape[1:]
    assert d_ff % bn == 0
    bm = _row_block(m, bm)
    nb = d_ff // bn
    m_steps = m // bm
    slab = d_ff // (nb * m_steps)
    assert slab * nb * m_steps == d_ff and slab % 16 == 0, (d_ff, nb, m_steps)
    out_tile = lambda n: jnp.maximum(n - 1, 0)
    down_slab = lambda n, i: out_tile(n) * m_steps + _row_step(n, i)
    return pl.pallas_call(
        _swiglu_kernel,
        grid=(nb + 1, m_steps),
        in_specs=[pl.BlockSpec((bm, k), lambda n, i: (_row_step(n, i), 0)),
                  pl.BlockSpec((mt, k), lambda n, i: (0, 0)),
                  _w_slab_spec(w, layer, k, bn, nb, m_steps),
                  _w_slab_spec(w, layer, k, bn, nb, m_steps, nb),
                  pl.BlockSpec((None, slab, n_down), lambda n, i: (layer, down_slab(n, i), 0))],
        out_specs=[pl.BlockSpec((bm, bn), lambda n, i: (_row_step(n, i), out_tile(n))),
                   pl.BlockSpec((mt, bn), lambda n, i: (0, out_tile(n))),
                   pl.BlockSpec((slab, n_down), lambda n, i: (down_slab(n, i), 0))],
        out_shape=[jax.ShapeDtypeStruct((m, d_ff), BF16), jax.ShapeDtypeStruct((mt, d_ff), BF16),
                   jax.ShapeDtypeStruct((d_ff, n_down), BF16)],
        scratch_shapes=[pltpu.VMEM((2, k, bn), BF16), pltpu.VMEM((2, k, bn), BF16)],
        compiler_params=_params("arbitrary", "arbitrary"),
        name="swiglu_matmul",
    )(a, tail, w, w, w_down)


def _mm_bf16w_kernel(a_ref, w_ref, o_ref):
    o_ref[...] = jnp.dot(a_ref[...], w_ref[...], preferred_element_type=F32).astype(o_ref.dtype)


def _matmul_bf16w(a, w, *, bn=512, bm=512):
    m, k = a.shape
    n = w.shape[-1]
    assert n % bn == 0
    bm = _row_block(m, bm)
    return pl.pallas_call(
        _mm_bf16w_kernel,
        grid=(n // bn, m // bm),
        in_specs=[pl.BlockSpec((bm, k), lambda j, i: (i, 0)),
                  pl.BlockSpec((k, bn), lambda j, i: (0, j))],
        out_specs=pl.BlockSpec((bm, bn), lambda j, i: (i, j)),
        out_shape=jax.ShapeDtypeStruct((m, n), BF16),
        compiler_params=_params("parallel", "parallel"),
        name="matmul_bf16w",
    )(a, w)


def _mem_attn_kernel(head_dim, q_ref, k_ref, v_ref, *refs):
    o_ref = refs[-1]
    scale = head_dim ** -0.5
    for h in range(N_MEM_HEADS):
        cols = slice(h * head_dim, (h + 1) * head_dim)
        q = q_ref[:, cols]
        k = k_ref[:, cols].astype(BF16)
        v = v_ref[:, cols].astype(BF16)
        s = lax.dot_general(q, k, (((1,), (1,)), ((), ())), preferred_element_type=F32) * scale
        m = jnp.max(s, axis=-1, keepdims=True)
        p = jnp.exp(s - m)
        p = p * (1.0 / jnp.sum(p, axis=-1, keepdims=True))
        o_ref[:, cols] = jnp.dot(p.astype(BF16), v, preferred_element_type=F32).astype(o_ref.dtype)


def _mem_attn(q, q_col, width, mk, mv, layer, k_col, v_col, dest=None):
    b, l, _ = q.shape
    n_mem = mk.shape[2]
    head_dim = width // N_MEM_HEADS
    bl = _row_block(l, 512)
    in_specs = [pl.BlockSpec((None, bl, width), lambda i, j: (i, j, q_col)),
                pl.BlockSpec((None, None, n_mem, width), lambda i, j: (layer, i, 0, k_col)),
                pl.BlockSpec((None, None, n_mem, width), lambda i, j: (layer, i, 0, v_col))]
    args = [q, mk, mv]
    out_cols, out_col, aliases = width, 0, {}
    if dest is not None:
        out_cols = dest.shape[-1]
        out_col = out_cols // width - 1
        in_specs.append(pl.BlockSpec(memory_space=pl.ANY))
        args.append(dest)
        aliases = {3: 0}
    return pl.pallas_call(
        functools.partial(_mem_attn_kernel, head_dim),
        grid=(b, l // bl),
        in_specs=in_specs,
        out_specs=pl.BlockSpec((None, bl, width), lambda i, j: (i, j, out_col)),
        out_shape=jax.ShapeDtypeStruct((b, l, out_cols), BF16),
        input_output_aliases=aliases,
        compiler_params=_params("parallel", "parallel"),
        name="mem_attn",
    )(*args)


def _tile_rows(cache):
    lyr, b, n_mem, h, hd = cache.shape
    t = hd // LANES
    v = cache.reshape(lyr, b, n_mem, h, t, LANES)
    return jnp.swapaxes(v, 3, 4).reshape(lyr, b, n_mem * t * h, LANES)


def _mem_attn_decode_kernel(n_heads, q_ref, k_ref, v_ref, o_ref):
    tiles = q_ref.shape[-1] // (n_heads * LANES)
    head_dim = tiles * LANES
    group = tiles * n_heads
    n_mem = k_ref.shape[0] // group
    scale = head_dim ** -0.5

    def head_tile(ref, h, t):
        return ref[pl.ds(t * n_heads + h, n_mem, stride=group), :].astype(BF16)

    for h in range(n_heads):
        s = jnp.zeros((q_ref.shape[0], n_mem), F32)
        for t in range(tiles):
            q = q_ref[:, h * head_dim + t * LANES:h * head_dim + (t + 1) * LANES]
            s = s + lax.dot_general(q, head_tile(k_ref, h, t), (((1,), (1,)), ((), ())),
                                    preferred_element_type=F32)
        s = s * scale
        m = jnp.max(s, axis=-1, keepdims=True)
        p = jnp.exp(s - m)
        p = (p * (1.0 / jnp.sum(p, axis=-1, keepdims=True))).astype(BF16)
        for t in range(tiles):
            o = jnp.dot(p, head_tile(v_ref, h, t), preferred_element_type=F32)
            o_ref[:, h * head_dim + t * LANES:h * head_dim + (t + 1) * LANES] = o.astype(o_ref.dtype)


def _mem_attn_decode(q, mk_rows, mv_rows, layer):
    b, l, width = q.shape
    rows = mk_rows.shape[2]
    kv_spec = pl.BlockSpec((None, None, rows, LANES), lambda i: (layer, i, 0, 0))
    return pl.pallas_call(
        functools.partial(_mem_attn_decode_kernel, N_MEM_HEADS),
        grid=(b,),
        in_specs=[pl.BlockSpec((None, l, width), lambda i: (i, 0, 0)), kv_spec, kv_spec],
        out_specs=pl.BlockSpec((None, l, width), lambda i: (i, 0, 0)),
        out_shape=jax.ShapeDtypeStruct((b, l, width), BF16),
        compiler_params=_params("parallel"),
        name="mem_attn_decode",
    )(q, mk_rows, mv_rows)


def _sink_softmax(s, sink):
    m = jnp.maximum(jnp.max(s, axis=-1, keepdims=True), sink)
    p = jnp.exp(s - m)
    return p * (1.0 / (jnp.sum(p, axis=-1, keepdims=True) + jnp.exp(sink - m)))


def _swa_prompt_kernel(slopes, sink_ref, q_ref, kp_ref, kc_ref, vp_ref, vc_ref, dest_ref, o_ref):
    w = WINDOW
    blk = pl.program_id(1)
    qi = lax.broadcasted_iota(jnp.int32, (w, 2 * w), 0)
    kj = lax.broadcasted_iota(jnp.int32, (w, 2 * w), 1)
    dist = w + qi - kj
    valid = (dist >= 0) & (dist < WINDOW) & ((blk - 1) * w + kj >= 0)
    distf = dist.astype(F32)
    scale = HEAD_DIM ** -0.5
    group = len(slopes) // N_KV_HEADS
    for hk in range(N_KV_HEADS):
        kcols = slice(hk * HEAD_DIM, (hk + 1) * HEAD_DIM)
        kcat = jnp.concatenate([kp_ref[:, kcols], kc_ref[:, kcols]], axis=0).astype(BF16)
        vcat = jnp.concatenate([vp_ref[:, kcols], vc_ref[:, kcols]], axis=0).astype(BF16)
        for g in range(group):
            h = hk * group + g
            cols = slice(h * HEAD_DIM, (h + 1) * HEAD_DIM)
            s = lax.dot_general(q_ref[:, cols], kcat, (((1,), (1,)), ((), ())),
                                preferred_element_type=F32) * scale
            s = s - float(slopes[h]) * distf
            s = jnp.where(valid, s, -jnp.inf)
            p = _sink_softmax(s, sink_ref[h])
            o_ref[:, cols] = jnp.dot(p.astype(BF16), vcat, preferred_element_type=F32).astype(o_ref.dtype)


def _swa_prompt(q, kv, sinks, slopes, out_cols):
    b, l, _ = q.shape
    w = WINDOW
    kvw = N_KV_HEADS * HEAD_DIM
    tw = len(slopes) * HEAD_DIM
    assert l % w == 0
    prev = lambda i, j: jnp.maximum(j - 1, 0)
    return pl.pallas_call(
        functools.partial(_swa_prompt_kernel, slopes),
        grid=(b, l // w),
        in_specs=[pl.BlockSpec(memory_space=pltpu.SMEM),
                  pl.BlockSpec((None, w, tw), lambda i, j: (i, j, 0)),
                  pl.BlockSpec((None, w, kvw), lambda i, j: (i, prev(i, j), 0)),
                  pl.BlockSpec((None, w, kvw), lambda i, j: (i, j, 0)),
                  pl.BlockSpec((None, w, kvw), lambda i, j: (i, prev(i, j), 1)),
                  pl.BlockSpec((None, w, kvw), lambda i, j: (i, j, 1)),
                  pl.BlockSpec(memory_space=pl.ANY)],
        out_specs=pl.BlockSpec((None, w, tw), lambda i, j: (i, j, 0)),
        out_shape=jax.ShapeDtypeStruct((b, l, out_cols), BF16),
        input_output_aliases={6: 0},
        compiler_params=_params("parallel", "parallel"),
        name="swa_prompt",
    )(sinks, q, kv, kv, kv, kv, jnp.zeros((b, l, out_cols), BF16))


def _swa_sample_kernel(slopes, sink_ref, q_ref, k_ref, v_ref, o_ref):
    n_heads = len(slopes)
    group = n_heads // N_KV_HEADS
    wb = k_ref.shape[0] // N_KV_HEADS
    q = q_ref[...].astype(BF16)

    def kv_head(ref, hk):
        return ref[pl.ds(hk, wb, stride=N_KV_HEADS), :].astype(BF16)

    head = lax.broadcasted_iota(jnp.int32, (n_heads, wb), 0)
    pos = lax.broadcasted_iota(jnp.int32, (n_heads, wb), 1)
    distf = (wb - 1 - pos).astype(F32)
    slope = jnp.zeros((n_heads, wb), F32)
    sink = jnp.zeros((n_heads, 1), F32)
    head_col = lax.broadcasted_iota(jnp.int32, (n_heads, 1), 0)
    for h in range(n_heads):
        slope = jnp.where(head == h, float(slopes[h]), slope)
        sink = jnp.where(head_col == h, sink_ref[h], sink)
    scale = HEAD_DIM ** -0.5
    s = jnp.zeros((n_heads, wb), F32)
    for hk in range(N_KV_HEADS):
        s_hk = lax.dot_general(q, kv_head(k_ref, hk), (((1,), (1,)), ((), ())), preferred_element_type=F32)
        s = jnp.where(head // group == hk, s_hk, s)
    s = s * scale - slope * distf
    p = _sink_softmax(s, sink).astype(BF16)
    o = jnp.zeros((n_heads, HEAD_DIM), F32)
    head_o = lax.broadcasted_iota(jnp.int32, (n_heads, HEAD_DIM), 0)
    for hk in range(N_KV_HEADS):
        o_hk = jnp.dot(p, kv_head(v_ref, hk), preferred_element_type=F32)
        o = jnp.where(head_o // group == hk, o_hk, o)
    o_ref[...] = o.astype(o_ref.dtype)


def _swa_sample(q, win_k, win_v, sinks, slopes):
    b, n_heads, _ = q.shape
    rows = win_k.shape[1] * N_KV_HEADS
    win_k = win_k.reshape(b, rows, HEAD_DIM)
    win_v = win_v.reshape(b, rows, HEAD_DIM)
    return pl.pallas_call(
        functools.partial(_swa_sample_kernel, slopes),
        grid=(b,),
        in_specs=[pl.BlockSpec(memory_space=pltpu.SMEM),
                  pl.BlockSpec((None, n_heads, HEAD_DIM), lambda i: (i, 0, 0)),
                  pl.BlockSpec((None, rows, HEAD_DIM), lambda i: (i, 0, 0)),
                  pl.BlockSpec((None, rows, HEAD_DIM), lambda i: (i, 0, 0))],
        out_specs=pl.BlockSpec((None, n_heads, HEAD_DIM), lambda i: (i, 0, 0)),
        out_shape=jax.ShapeDtypeStruct((b, n_heads, HEAD_DIM), BF16),
        compiler_params=_params("parallel"),
        name="swa_sample",
    )(sinks, q, win_k, win_v)


def _lane_pick(x, lane):
    idx = lax.broadcasted_iota(jnp.int32, x.shape, x.ndim - 1)
    return jnp.sum(jnp.where(idx == lane, x, 0.0), axis=-1, keepdims=True)


def _gates(ba, alog, dtb):
    beta = jax.nn.sigmoid(ba)
    g = -jnp.exp(alog) * _softplus(ba + dtb)
    return beta, g


def _bdot(a, b, contract_b_last, precision=None):
    dims = (((2,), (2 if contract_b_last else 1,)), ((0,), (0,)))
    return lax.dot_general(a, b, dims, preferred_element_type=F32, precision=precision)


def _delta_prompt_kernel(n_heads, hb, q_ref, k_ref, v_ref, z_ref, ba_ref, wq_ref, wk_ref, wv_ref,
                         alog_ref, dtb_ref, gn_ref, dest_ref, o_ref, s_ref, pad_ref, beta_ref, g_ref):
    l = q_ref.shape[0]
    c = DELTA_CHUNK
    n = l // c
    hd = HEAD_DIM
    hstep = pl.program_id(1)

    @pl.when(hstep == 0)
    def _():
        beta_all, g_all = _gates(ba_ref[...], alog_ref[...], dtb_ref[...])
        beta_ref[...] = beta_all
        g_ref[...] = g_all

    pad_ref[0:8, :] = jnp.zeros((8, hd), F32)

    def conv(x_ref, w_ref, cols):
        pad_ref[8:8 + l, :] = x_ref[:, cols]
        base = 8 - (CONV_W - 1)
        out = pad_ref[base:base + l, :] * w_ref[0:1, cols]
        for j in range(1, CONV_W):
            out = out + pad_ref[base + j:base + j + l, :] * w_ref[j:j + 1, cols]
        return _silu(out)

    def l2norm(x):
        return x * lax.rsqrt(jnp.sum(x * x, axis=-1, keepdims=True) + EPS)

    row = lax.broadcasted_iota(jnp.int32, (1, c, c), 1)
    col = lax.broadcasted_iota(jnp.int32, (1, c, c), 2)
    incl = row >= col
    strict = row > col
    eye = row == col

    def lower_left(shift):
        return (((row >> (shift + 1)) == (col >> (shift + 1)))
                & (((row >> shift) & 1) == 1) & (((col >> shift) & 1) == 0))

    def prepare(hh):
        cols = slice(hh * hd, (hh + 1) * hd)
        head = hstep * hb + hh
        q3 = (l2norm(conv(q_ref, wq_ref, cols)) * (hd ** -0.5)).reshape(n, c, hd)
        k3 = l2norm(conv(k_ref, wk_ref, cols)).reshape(n, c, hd)
        v3 = conv(v_ref, wv_ref, cols).reshape(n, c, hd)
        beta = _lane_pick(beta_ref[...], head).reshape(n, c, 1)
        g = _lane_pick(g_ref[...], n_heads + head).reshape(n, c, 1)
        gc_row = jnp.sum(jnp.where(row <= col, jnp.broadcast_to(g, (n, c, c)), 0.0), axis=1, keepdims=True)
        gc_col = jnp.sum(jnp.where(eye, jnp.broadcast_to(gc_row, (n, c, c)), 0.0), axis=2, keepdims=True)
        decay = jnp.where(incl, jnp.exp(jnp.where(incl, gc_col - gc_row, 0.0)), 0.0)

        kb = k3.astype(BF16)
        qkk = _bdot(jnp.concatenate([q3, k3], axis=1).astype(BF16), kb, True)
        qk = qkk[:, :c] * decay
        lmat = jnp.where(strict, qkk[:, c:] * decay * beta, 0.0)
        tinv = jnp.where(eye, 1.0, 0.0) - jnp.where(lower_left(0), lmat, 0.0)
        for shift in range(1, int(math.log2(c))):
            cs = jnp.where(lower_left(shift), lmat, 0.0).astype(BF16)
            tb = tinv.astype(BF16)
            tinv = tinv - _bdot(_bdot(tb, cs, False).astype(BF16), tb, False)
        egc = jnp.exp(gc_col)
        rhs = jnp.concatenate([v3 * beta, k3 * (beta * egc)], axis=2).astype(BF16)
        uw = _bdot(tinv.astype(BF16), rhs, False)
        g_last = gc_col[:, c - 1:c, :]
        return dict(
            u=uw[:, :, :hd],
            wq=jnp.concatenate([uw[:, :, hd:], q3 * egc], axis=1).astype(BF16),
            qk=qk.astype(BF16),
            kt=(k3 * jnp.exp(g_last - gc_col)).astype(BF16),
            eg_last=jnp.exp(g_last))

    heads = [prepare(hh) for hh in range(hb)]
    states = [jnp.zeros((hd, hd), F32) for _ in range(hb)]
    for i in range(n):
        for hh, p in enumerate(heads):
            cols = slice(hh * hd, (hh + 1) * hd)
            tok = slice(i * c, (i + 1) * c)
            sb = states[hh].astype(BF16)
            ws = jnp.dot(p["wq"][i], sb, preferred_element_type=F32)
            vnb = (p["u"][i] - ws[:c]).astype(BF16)
            o = ws[c:] + jnp.dot(p["qk"][i], vnb, preferred_element_type=F32)
            states[hh] = states[hh] * p["eg_last"][i] + lax.dot_general(
                p["kt"][i], vnb, (((0,), (0,)), ((), ())), preferred_element_type=F32)
            o = o * lax.rsqrt(jnp.mean(o * o, axis=-1, keepdims=True) + EPS) * gn_ref[...]
            o_ref[tok, cols] = (o * _silu(z_ref[tok, cols])).astype(o_ref.dtype)
    for hh in range(hb):
        s_ref[hh] = states[hh]


def _delta_prompt(proj, ba, w_conv, layer, alog_pad, dtb_pad, gate_norm, n_heads, out_cols):
    b, l, _ = proj.shape
    hd = HEAD_DIM
    tw = n_heads * hd
    hb = DELTA_HEADS_PER_STEP
    assert n_heads % hb == 0
    steps = n_heads // hb
    col = lambda part: pl.BlockSpec((None, l, hb * hd), lambda i, h: (i, 0, part * steps + h))
    wcol = lambda part: pl.BlockSpec((None, CONV_W, hb * hd), lambda i, h: (layer, 0, part * steps + h))
    vec = pl.BlockSpec((1, LANES), lambda i, h: (0, 0))
    return pl.pallas_call(
        functools.partial(_delta_prompt_kernel, n_heads, hb),
        grid=(b, steps),
        in_specs=[col(0), col(1), col(2), col(3),
                  pl.BlockSpec((None, l, LANES), lambda i, h: (i, 0, 0)),
                  wcol(0), wcol(1), wcol(2), vec, vec, vec, pl.BlockSpec(memory_space=pl.ANY)],
        input_output_aliases={11: 0},
        out_specs=[pl.BlockSpec((None, l, hb * hd), lambda i, h: (i, 0, h)),
                   pl.BlockSpec((None, hb, hd, hd), lambda i, h: (i, h, 0, 0))],
        out_shape=[jax.ShapeDtypeStruct((b, l, out_cols), BF16),
                   jax.ShapeDtypeStruct((b, n_heads, hd, hd), F32)],
        scratch_shapes=[pltpu.VMEM((l + 8, hd), F32), pltpu.VMEM((l, LANES), F32), pltpu.VMEM((l, LANES), F32)],
        compiler_params=_params("parallel", "arbitrary"),
        name="delta_prompt",
    )(proj, proj, proj, proj, ba, w_conv, w_conv, w_conv, alog_pad, dtb_pad, gate_norm,
      jnp.zeros((b, l, out_cols), BF16))


def _delta_sample_kernel(n_heads, x_ref, z_ref, ba_ref, hist_ref, wc_ref, s_ref, alog_ref, dtb_ref, gn_ref, *refs):
    o_ref, hist_o_ref, s_o_ref = refs[-3:]
    hd = HEAD_DIM
    tw = n_heads * hd
    x = x_ref[...]
    acc = hist_ref[0:1, :] * wc_ref[0:1, :]
    for j in range(1, CONV_W - 1):
        acc = acc + hist_ref[j:j + 1, :] * wc_ref[j:j + 1, :]
    acc = _silu(acc + x * wc_ref[CONV_W - 1:CONV_W, :])
    hist_o_ref[0:CONV_W - 2, :] = hist_ref[1:CONV_W - 1, :]
    hist_o_ref[CONV_W - 2:CONV_W - 1, :] = x

    beta_all, g_all = _gates(ba_ref[...], alog_ref[...], dtb_ref[...])

    def head_columns(off, scale):
        rows = jnp.concatenate([acc[:, off + h * hd: off + (h + 1) * hd] for h in range(n_heads)]
                               + [jnp.zeros((hd - n_heads, hd), F32)], axis=0)
        rows = rows * (lax.rsqrt(jnp.sum(rows * rows, axis=-1, keepdims=True) + EPS) * scale)
        return rows.T

    def per_head(make):
        return jnp.stack([make(h) for h in range(n_heads)], axis=0)

    q_cols = head_columns(0, hd ** -0.5)
    k_cols = head_columns(tw, 1.0)
    qc = per_head(lambda h: q_cols[:, h:h + 1])
    kc = per_head(lambda h: k_cols[:, h:h + 1])
    v = per_head(lambda h: acc[:, 2 * tw + h * hd: 2 * tw + (h + 1) * hd])
    z = per_head(lambda h: z_ref[:, h * hd:(h + 1) * hd])
    beta = per_head(lambda h: _lane_pick(beta_all, h))
    g = per_head(lambda h: _lane_pick(g_all, n_heads + h))
    s = s_ref[...] * jnp.exp(g)
    vn = beta * (v - jnp.sum(s * kc, axis=1, keepdims=True))
    s = s + kc * vn
    s_o_ref[...] = s
    o = jnp.sum(s * qc, axis=1, keepdims=True)
    o = o * lax.rsqrt(jnp.mean(o * o, axis=-1, keepdims=True) + EPS) * gn_ref[...]
    o = (o * _silu(z)).astype(o_ref.dtype)
    for h in range(n_heads):
        o_ref[:, h * hd:(h + 1) * hd] = o[h]


def _delta_sample(proj, ba, state_conv, w_conv, state_delta, layer, alog_pad, dtb_pad, gate_norm, n_heads,
                  new_states=None):
    b = proj.shape[0]
    hd = HEAD_DIM
    tw = n_heads * hd
    hist = CONV_W - 1
    vec = pl.BlockSpec((1, LANES), lambda i: (0, 0))
    in_specs = [pl.BlockSpec((None, 1, 3 * tw), lambda i: (i, 0, 0)),
                pl.BlockSpec((None, 1, tw), lambda i: (i, 0, 3)),
                pl.BlockSpec((None, 1, LANES), lambda i: (i, 0, 0)),
                pl.BlockSpec((None, None, hist, 3 * tw), lambda i: (layer, i, 0, 0)),
                pl.BlockSpec((None, CONV_W, 3 * tw), lambda i: (layer, 0, 0)),
                pl.BlockSpec((None, None, n_heads, hd, hd), lambda i: (layer, i, 0, 0, 0)),
                vec, vec, vec]
    args = [proj, proj, ba, state_conv, w_conv, state_delta, alog_pad, dtb_pad, gate_norm]
    aliases = {}
    if new_states is not None:
        in_specs.append(pl.BlockSpec(memory_space=pl.ANY))
        args.append(new_states)
        aliases = {len(args) - 1: 2}
    return pl.pallas_call(
        functools.partial(_delta_sample_kernel, n_heads),
        grid=(b,),
        in_specs=in_specs,
        out_specs=[pl.BlockSpec((None, 1, tw), lambda i: (i, 0, 0)),
                   pl.BlockSpec((None, hist, 3 * tw), lambda i: (i, 0, 0)),
                   pl.BlockSpec((None, None, n_heads, hd, hd), lambda i: (layer, i, 0, 0, 0))],
        out_shape=[jax.ShapeDtypeStruct((b, 1, tw), BF16),
                   jax.ShapeDtypeStruct((b, hist, 3 * tw), F32),
                   jax.ShapeDtypeStruct(state_delta.shape, F32)],
        input_output_aliases=aliases,
        compiler_params=_params("parallel"),
        name="delta_sample",
    )(*args)


def _trunk(x_p, x_s, mem_kv_p, cache_mk, cache_mv, conv_state, delta_state, buf_k, buf_v, wts):
    bp, lp, d = x_p.shape
    bs, ls, _ = x_s.shape
    assert ls == 1
    depth = wts["w_out"].shape[0]
    n_a = wts["w_in_a_t"].shape[0]
    d_ff = wts["w_down"].shape[1]
    tw = wts["w_conv"].shape[-1] // 3
    n_heads = tw // HEAD_DIM
    mem_w = d - tw
    kvw = N_KV_HEADS * HEAD_DIM
    slopes = _alibi_slopes(n_heads)
    off_b = 4 * tw
    off_qm = off_b + 2 * n_heads

    def lane_pad(vals):
        return jnp.zeros((1, LANES), F32).at[0, n_heads:2 * n_heads].set(vals.astype(F32))

    def resid(xp, fp, xs, fs, w_post, w_next):
        xp, np_ = _resid_norm(xp, fp, w_post, w_next)
        xs, ns_ = _resid_norm(xs, fs, w_post, w_next)
        return xp, xs, np_, ns_

    xp2 = x_p.reshape(bp * lp, d)
    xs2 = x_s.reshape(bs, d)
    (xn_p,) = _norm_cast(xp2, wts["norm_mix_pre"][0:1])
    (xn_s,) = _norm_cast(xs2, wts["norm_mix_pre"][0:1])
    conv_p, conv_s, delta_p = [], [], []
    delta_s = jnp.zeros(delta_state.shape, F32)
    kv_p = win_k = win_v = None
    for layer in range(depth):
        if layer < n_a:
            wt = wts["w_in_a_t"]
            proj_p, proj_s = _matmul(xn_p, wt, tail=xn_s, layer=layer, n_cols=off_b, w_is_nk=True, bn=WIDE_TILE)
            w_ba = jnp.pad(wt[layer, off_b:off_qm, :], ((0, LANES - 2 * n_heads), (0, 0)))
            ba_p, ba_s = _matmul(xn_p, w_ba, tail=xn_s, w_is_nk=True)
            qm_p, qm_s = _matmul(xn_p, wt[layer, off_qm:, :], tail=xn_s, out_dtype=BF16, w_is_nk=True,
                                 bn=WIDE_TILE)
            alog_pad = lane_pad(wts["a_log"][layer])
            dtb_pad = lane_pad(wts["dt_bias"][layer])
            gate_norm = wts["w_gate_norm"][layer].reshape(1, HEAD_DIM)
            proj_p = proj_p.reshape(bp, lp, off_b)
            tok_p, s_p = _delta_prompt(proj_p, ba_p.reshape(bp, lp, LANES), wts["w_conv"], layer,
                                       alog_pad, dtb_pad, gate_norm, n_heads, d)
            conv_p.append(proj_p[:, lp - (CONV_W - 1):, :3 * tw])
            tok_s, hist_s, delta_s = _delta_sample(proj_s.reshape(bs, 1, off_b), ba_s.reshape(bs, 1, LANES),
                                                   conv_state, wts["w_conv"], delta_state, layer, alog_pad, dtb_pad,
                                                   gate_norm, n_heads, new_states=delta_s)
            conv_s.append(hist_s)
            delta_p.append(s_p)
            mixed_p = _mem_attn(qm_p.reshape(bp, lp, mem_w), 0, mem_w, mem_kv_p, mem_kv_p, layer, 0, 1, dest=tok_p)
            mo_s = _mem_attn_decode(qm_s.reshape(bs, 1, mem_w), cache_mk, cache_mv, layer)
        else:
            lb = layer - n_a
            sinks = wts["sinks"][lb].astype(F32)
            proj_p, proj_s = _matmul(xn_p, wts["w_in_b"], tail=xn_s, layer=lb, out_dtype=BF16, bn=WIDE_TILE)
            proj_p = proj_p.reshape(bp, lp, d)
            tok_p = _swa_prompt(proj_p, kv_p, sinks, slopes, d)
            mixed_p = _mem_attn(proj_p, tw // mem_w, mem_w, mem_kv_p, mem_kv_p, layer, 0, 1, dest=tok_p)
            q_s = proj_s[:, :tw].reshape(bs, n_heads, HEAD_DIM)
            tok_s = _swa_sample(q_s, win_k, win_v, sinks, slopes).reshape(bs, 1, tw)
            mo_s = _mem_attn_decode(proj_s[:, tw:].reshape(bs, 1, mem_w), cache_mk, cache_mv, layer)
        mixed_p = mixed_p.reshape(bp * lp, d)
        mixed_s = jnp.concatenate([tok_s, mo_s], axis=-1).reshape(bs, d)
        mix_p, mix_s = _matmul(mixed_p, wts["w_out"], tail=mixed_s, layer=layer, out_dtype=BF16, bn=WIDE_TILE)
        xp2, xs2, (hn_p,), (hn_s,) = resid(xp2, mix_p, xs2, mix_s, wts["norm_mix_post"][layer],
                                           wts["norm_ffn_pre"][layer:layer + 1])
        hid_p, hid_s, w_down_bf16 = _swiglu_matmul(hn_p, hn_s, wts["w_gate_up"], wts["w_down"], layer)
        f_p = _matmul_bf16w(hid_p, w_down_bf16)
        f_s = _matmul_bf16w(hid_s, w_down_bf16)
        w_post = wts["norm_ffn_post"][layer]
        if layer + 1 == depth:
            xp2, xs2, _, _ = resid(xp2, f_p, xs2, f_s, w_post, None)
        elif layer + 1 == n_a:
            w_next = jnp.stack([wts["norm_mix_pre"][layer + 1], wts["norm_kv"]])
            xp2, xs2, (xn_p, xkv_p), (xn_s, xkv_s) = resid(xp2, f_p, xs2, f_s, w_post, w_next)
            kv_p, kv_s = _matmul(xkv_p, wts["w_kv"], tail=xkv_s, bn=WIDE_TILE)
            kv_p = kv_p.reshape(bp, lp, 2 * kvw)
            new_row = lambda t: t.reshape(bs, 1, N_KV_HEADS, HEAD_DIM)
            win_k = jnp.concatenate([buf_k[:, 1:], new_row(kv_s[:, :kvw])], axis=1)
            win_v = jnp.concatenate([buf_v[:, 1:], new_row(kv_s[:, kvw:])], axis=1)
        else:
            xp2, xs2, (xn_p,), (xn_s,) = resid(xp2, f_p, xs2, f_s, w_post,
                                               wts["norm_mix_pre"][layer + 1:layer + 2])
    wp = min(WINDOW, lp)
    heads = lambda t: t.reshape(t.shape[0], -1, N_KV_HEADS, HEAD_DIM)
    return (xp2.reshape(bp, lp, d), xs2.reshape(bs, 1, d),
            heads(kv_p[:, lp - wp:, :kvw]), heads(kv_p[:, lp - wp:, kvw:]), jnp.stack(conv_p), jnp.stack(delta_p),
            heads(win_k), heads(win_v), jnp.stack(conv_s), delta_s)


def kernel(x_prompt, x_sample, cache_mem_k, cache_mem_v, cache_swa_k, cache_swa_v, state_conv, state_delta, mem_prompt, w_in_a, w_conv, a_log, dt_bias, w_gate_norm, w_in_b, sinks, norm_kv, w_kv, norm_mem, w_mem_kv, w_out, norm_mix_pre, norm_mix_post, norm_ffn_pre, norm_ffn_post, w_gate_up, w_down):
    wts = dict(w_in_a_t=jnp.swapaxes(w_in_a, 1, 2), w_conv=w_conv, a_log=a_log, dt_bias=dt_bias,
               w_gate_norm=w_gate_norm, w_in_b=w_in_b, sinks=sinks, norm_kv=norm_kv, w_kv=w_kv, w_out=w_out,
               norm_mix_pre=norm_mix_pre, norm_mix_post=norm_mix_post, norm_ffn_pre=norm_ffn_pre,
               norm_ffn_post=norm_ffn_post, w_gate_up=w_gate_up, w_down=w_down)
    depth = w_out.shape[0]
    bp, n_mem, d = mem_prompt.shape
    bs = x_sample.shape[0]
    mem_w = w_mem_kv.shape[-1] // 2
    mem_shape = (depth, bp, n_mem, N_MEM_HEADS, mem_w // N_MEM_HEADS)

    memn = _norm_cast(mem_prompt.reshape(bp * n_mem, d), norm_mem)
    mem_kv = jnp.stack([_matmul(memn[i], w_mem_kv, layer=i) for i in range(depth)])
    mem_kv = mem_kv.reshape(depth, bp, n_mem, 2 * mem_w)
    mem_k_p = mem_kv[..., :mem_w].reshape(mem_shape)
    mem_v_p = mem_kv[..., mem_w:].reshape(mem_shape)
    cmk = _tile_rows(cache_mem_k)
    cmv = _tile_rows(cache_mem_v)
    y_p, y_s, swk_p, swv_p, conv_p, delta_p, swk_s, swv_s, conv_s, delta_s = _trunk(
        x_prompt, x_sample, mem_kv, cmk, cmv, state_conv, state_delta, cache_swa_k, cache_swa_v, wts)
    return (y_p, y_s, mem_k_p, mem_v_p, swk_p, swv_p, conv_p, delta_p, swk_s, swv_s, conv_s, delta_s)
```
